```python
import jax, jax.numpy as jnp
from jax import lax
import numpy as np

D_MODEL = 1024
BATCH = 2
SEQ = 8192
DEPTH = 2
DEC_BATCH = 128
DEC_SEQ = 4
PAST_LEN = 2048
PAGE_SIZE = 128

HEAD_DIM = 64
N_A_LAYERS = (DEPTH + 1) // 2
N_C_LAYERS = DEPTH // 2
GLA_HEADS = 4
GLA_DK = 64
GLA_DV = 128
GLA_LOWRANK = 16
GLA_TAU = 16.0
GLA_CHUNK = 64
NSA_HEADS = 8
NSA_KV_HEADS = 2
NSA_GROUP = NSA_HEADS // NSA_KV_HEADS
CMP_LEN = 32
CMP_STRIDE = 16
SLC_BLOCK = 64
SLC_TOPK = 16
NSA_WINDOW = 512
DIL_PAIRS = ((128, 1), (512, 4), (2048, 16))
N_DIL = 3
DIL_HEADS = 8
D_FF = 4 * D_MODEL
PLE_DIM = 256
ROPE_THETA = 10000.0
EPS = 1e-6
Q_BLOCK = 128
NEG = -1e30
BIG = 1e30
TINY = 1e-20

A_SIZES = (GLA_HEADS * GLA_DK, GLA_HEADS * GLA_DK, GLA_HEADS * GLA_DV, GLA_HEADS * GLA_DV, GLA_LOWRANK,
           NSA_HEADS * HEAD_DIM, 6 * NSA_KV_HEADS * HEAD_DIM, 3 * NSA_HEADS)
A_IN = sum(A_SIZES)
A_SPLIT = [sum(A_SIZES[:i + 1]) for i in range(len(A_SIZES) - 1)]
A_OUT = GLA_HEADS * GLA_DV + NSA_HEADS * HEAD_DIM
C_IN = N_DIL * 3 * DIL_HEADS * HEAD_DIM
C_OUT = DIL_HEADS * HEAD_DIM

kernel_name = 'hybrid_gla_nsa_dilated_decode_step'


def rmsnorm(x, g):
    xf = x.astype(jnp.float32)
    y = xf * lax.rsqrt(jnp.mean(xf * xf, axis=-1, keepdims=True) + EPS)
    return (y * g.astype(jnp.float32)).astype(x.dtype)


def rope(x, pos):
    half = x.shape[-1] // 2
    freq = ROPE_THETA ** (-jnp.arange(half, dtype=jnp.float32) / half)
    ang = pos.astype(jnp.float32)[:, None] * freq[None, :]
    cos, sin = jnp.cos(ang)[:, None, :], jnp.sin(ang)[:, None, :]
    xf = x.astype(jnp.float32)
    x1, x2 = xf[..., :half], xf[..., half:]
    return jnp.concatenate([x1 * cos - x2 * sin, x2 * cos + x1 * sin], axis=-1).astype(x.dtype)


def masked_softmax_stats(s, mask):
    s = jnp.where(mask, s, NEG)
    m = jnp.max(s, axis=-1, keepdims=True)
    p = jnp.where(mask, jnp.exp(s - m), 0.0)
    return p, m, jnp.sum(p, axis=-1, keepdims=True)


def pad_rows(x, mult):
    pad = (-x.shape[1]) % mult
    return jnp.pad(x, [(0, 0), (0, pad)] + [(0, 0)] * (x.ndim - 2))


def gather_pages(pool, page_table):
    g = pool[page_table]
    return g.reshape((g.shape[0], g.shape[1] * g.shape[2]) + pool.shape[2:])


def gla_chunk(S0, q, k, v, lg):
    b = jnp.cumsum(lg, axis=2)
    C = q.shape[2]
    causal = jnp.tril(jnp.ones((C, C), bool))[:, :, None]
    diff = b[:, :, :, None, :] - b[:, :, None, :, :]
    decay = jnp.exp(jnp.where(causal, diff, NEG))
    A = jnp.einsum('bhtd,bhsd,bhtsd->bhts', q, k, decay)
    o = jnp.einsum('bhts,bhsv->bhtv', A, v) + jnp.einsum('bhtd,bhdv->bhtv', q * jnp.exp(b), S0)
    b_last = b[:, :, -1:, :]
    S = jnp.exp(b_last[:, :, 0, :])[..., None] * S0 + jnp.einsum('bhsd,bhsv->bhdv', k * jnp.exp(b_last - b), v)
    return S, o


def gla_mixer(gq, gk, gv, gr, ga, w_a2, b_a, g_norm, S0, chunk):
    B, T, _ = gq.shape
    f32 = jnp.float32

    def heads(t, d):
        return t.reshape(B, T, GLA_HEADS, d).transpose(0, 2, 1, 3).astype(f32)
    q = heads(gq, GLA_DK) * GLA_DK ** -0.5
    k = heads(gk, GLA_DK)
    v = heads(gv, GLA_DV)
    lg = heads(jax.nn.log_sigmoid((ga @ w_a2 + b_a).astype(f32)) / GLA_TAU, GLA_DK)
    n = T // chunk

    def split(t):
        return jnp.moveaxis(t.reshape(B, GLA_HEADS, n, chunk, t.shape[-1]), 2, 0)

    def step(S, inp):
        return gla_chunk(S, *inp)
    S, o = lax.scan(step, S0.astype(f32), (split(q), split(k), split(v), split(lg)))
    o = jnp.moveaxis(o, 0, 2).reshape(B, GLA_HEADS, T, GLA_DV).transpose(0, 2, 1, 3)
    o = rmsnorm(o, g_norm) * jax.nn.silu(gr.reshape(B, T, GLA_HEADS, GLA_DV).astype(f32))
    return o.reshape(B, T, GLA_HEADS * GLA_DV).astype(gq.dtype), S


def nsa_compress(rows, w_phi, pe):
    B, L, G, D = rows.shape
    nchunk = L // CMP_STRIDE
    c = rows[:, :nchunk * CMP_STRIDE].reshape(B, nchunk, CMP_STRIDE, G, D)
    blocks = jnp.concatenate([c[:, :-1], c[:, 1:]], axis=2) + pe[None, None, :, None, :]
    return jnp.einsum('bnpgd,pde->bnge', blocks, w_phi)


def cmp_to_slc(n_cmp, n_slc):
    cs = jnp.arange(n_cmp)[:, None] * CMP_STRIDE
    ss = jnp.arange(n_slc)[None, :] * SLC_BLOCK
    ov = jnp.maximum(jnp.minimum(cs + CMP_LEN, ss + SLC_BLOCK) - jnp.maximum(cs, ss), 0)
    return ov.astype(jnp.float32) / CMP_LEN


def nsa_attend(q, q_pos, gates, kc, vc, ks, vs, kw, vw, kw_pos):
    B, Tq, H, D = q.shape
    G, J = NSA_KV_HEADS, NSA_GROUP
    f32 = jnp.float32
    scale = D ** -0.5
    q_c = q.astype(f32).reshape(B, Tq, G, J, D)
    q_r = rope(q, q_pos).astype(f32).reshape(B, Tq, G, J, D)
    n_cmp = kc.shape[1]
    c_end = jnp.arange(n_cmp) * CMP_STRIDE + CMP_LEN - 1
    cmask = (c_end[None, :] <= q_pos[:, None])[None, :, None, None, :]
    p, _, l = masked_softmax_stats(jnp.einsum('btgjd,bngd->btgjn', q_c, kc) * scale, cmask)
    p = p / jnp.maximum(l, TINY)
    o_cmp = jnp.einsum('btgjn,bngd->btgjd', p, vc)
    n_slc = ks.shape[1] // SLC_BLOCK
    imp = jnp.einsum('btgjn,ns->btgs', p, cmp_to_slc(n_cmp, n_slc))
    blk = jnp.arange(n_slc)[None, :]
    cur = (q_pos // SLC_BLOCK)[:, None]
    forced = ((blk == cur) | (blk == 0))[None, :, None, :]
    valid = (blk <= cur)[None, :, None, :]
    imp = jnp.where(forced, BIG, jnp.where(valid, imp, NEG))
    _, idx = lax.top_k(imp, min(SLC_TOPK, n_slc))
    idx = idx.transpose(0, 2, 1, 3)
    take = jax.vmap(jax.vmap(lambda a, i: a[i]))

    def gather(x):
        xb = x.reshape(B, n_slc, SLC_BLOCK, G, D).transpose(0, 3, 1, 2, 4)
        return take(xb, idx).reshape(B, G, Tq, -1, D)
    k_sel, v_sel = gather(ks), gather(vs)
    sel_pos = (idx[..., None] * SLC_BLOCK + jnp.arange(SLC_BLOCK)).reshape(B, G, Tq, 1, -1)
    smask = sel_pos <= q_pos[None, None, :, None, None]
    p, _, l = masked_softmax_stats(jnp.einsum('bgtjd,bgtkd->bgtjk', q_r.transpose(0, 2, 1, 3, 4), k_sel) * scale, smask)
    o_slc = (jnp.einsum('bgtjk,bgtkd->bgtjd', p, v_sel) / jnp.maximum(l, TINY)).transpose(0, 2, 1, 3, 4)
    wmask = ((kw_pos[None, :] <= q_pos[:, None]) & (kw_pos[None, :] > q_pos[:, None] - NSA_WINDOW)
             & (kw_pos[None, :] >= 0))[None, :, None, None, :]
    p, _, l = masked_softmax_stats(jnp.einsum('btgjd,bsgd->btgjs', q_r, kw) * scale, wmask)
    o_win = jnp.einsum('btgjs,bsgd->btgjd', p, vw) / jnp.maximum(l, TINY)
    g = gates.reshape(B, Tq, G, J, 3, 1)
    o = g[..., 0, :] * o_cmp + g[..., 1, :] * o_slc + g[..., 2, :] * o_win
    return o.reshape(B, Tq, H * D)


def a_project(hn, pos, w_in, g_qk):
    B, T, _ = hn.shape
    gq, gk, gv, gr, ga, nq, nkv, ng = jnp.split(hn @ w_in, A_SPLIT, axis=-1)
    nq = rmsnorm(nq.reshape(B, T, NSA_HEADS, HEAD_DIM), g_qk[0])
    nkv = nkv.reshape(B, T, 6, NSA_KV_HEADS, HEAD_DIM)
    kc = rmsnorm(nkv[:, :, 0], g_qk[1])
    ks = rope(rmsnorm(nkv[:, :, 2], g_qk[2]), pos)
    kw = rope(rmsnorm(nkv[:, :, 4], g_qk[3]), pos)
    rows_c = jnp.stack([kc, nkv[:, :, 1]], axis=2)
    rows_s = jnp.stack([ks, nkv[:, :, 3]], axis=2)
    rows_w = jnp.stack([kw, nkv[:, :, 5]], axis=2)
    gates = jax.nn.sigmoid(ng.astype(jnp.float32)).reshape(B, T, NSA_HEADS, 3)
    return (gq, gk, gv, gr, ga), nq, gates, rows_c, rows_s, rows_w


def a_mixer_prompt(hn, w_in, w_out, w_a2, b_a, g_norm, g_qk, w_phi, pe):
    B, T, _ = hn.shape
    gla_in, nq, gates, rows_c, rows_s, rows_w = a_project(hn, jnp.arange(T), w_in, g_qk)
    S0 = jnp.zeros((B, GLA_HEADS, GLA_DK, GLA_DV), jnp.float32)
    o_gla, S = gla_mixer(*gla_in, w_a2, b_a, g_norm, S0, min(GLA_CHUNK, T))
    kc = nsa_compress(rows_c[:, :, 0], w_phi[0], pe[0])
    vc = nsa_compress(rows_c[:, :, 1], w_phi[1], pe[1])
    rs = pad_rows(rows_s, SLC_BLOCK)
    rw = jnp.pad(rows_w, [(0, 0), (NSA_WINDOW, 0), (0, 0), (0, 0), (0, 0)])
    qb = min(Q_BLOCK, T)
    nb = T // qb

    def blocks(x):
        return jnp.moveaxis(x.reshape((B, nb, qb) + x.shape[2:]), 1, 0)

    def block(args):
        qi, gi, t0 = args
        qpos = t0 + jnp.arange(qb)
        band = lax.dynamic_slice_in_dim(rw, t0, NSA_WINDOW + qb, axis=1)
        bpos = t0 - NSA_WINDOW + jnp.arange(NSA_WINDOW + qb)
        return nsa_attend(qi, qpos, gi, kc, vc, rs[:, :, 0], rs[:, :, 1], band[:, :, 0], band[:, :, 1], bpos)
    o_nsa = lax.map(block, (blocks(nq), blocks(gates), jnp.arange(nb) * qb))
    o_nsa = jnp.moveaxis(o_nsa, 0, 1).reshape(B, T, NSA_HEADS * HEAD_DIM).astype(hn.dtype)
    y = jnp.concatenate([o_gla, o_nsa], axis=-1) @ w_out
    return y, S, rows_c, rows_s, rows_w[:, T - min(NSA_WINDOW, T):]


def a_mixer_sample(hn, S0, pool_c, pool_s, win_buf, page_table, w_in, w_out, w_a2, b_a, g_norm, g_qk, w_phi, pe):
    B, T, _ = hn.shape
    past = page_table.shape[1] * PAGE_SIZE
    qpos = past + jnp.arange(T)
    gla_in, nq, gates, rows_c, rows_s, rows_w = a_project(hn, qpos, w_in, g_qk)
    o_gla, S = gla_mixer(*gla_in, w_a2, b_a, g_norm, S0, T)
    all_c = jnp.concatenate([gather_pages(pool_c, page_table), rows_c.astype(pool_c.dtype)], axis=1)
    all_s = pad_rows(jnp.concatenate([gather_pages(pool_s, page_table), rows_s.astype(pool_s.dtype)], axis=1), SLC_BLOCK)
    kc = nsa_compress(all_c[:, :, 0], w_phi[0], pe[0])
    vc = nsa_compress(all_c[:, :, 1], w_phi[1], pe[1])
    wb = win_buf.shape[1]
    all_w = jnp.concatenate([win_buf, rows_w.astype(win_buf.dtype)], axis=1)
    wpos = past - wb + jnp.arange(wb + T)
    o_nsa = nsa_attend(nq, qpos, gates, kc, vc, all_s[:, :, 0], all_s[:, :, 1], all_w[:, :, 0], all_w[:, :, 1], wpos)
    o_nsa = o_nsa.astype(hn.dtype)
    y = jnp.concatenate([o_gla, o_nsa], axis=-1) @ w_out
    return y, S, rows_c, rows_s, all_w[:, T:]


def dilated_attend(q, k, v, q_idx, dilation, n_keys):
    idx = q_idx[:, None] - dilation * jnp.arange(n_keys)[None, :]
    valid = idx >= 0
    idx = jnp.maximum(idx, 0)
    kg, vg = k[:, idx], v[:, idx]
    s = jnp.einsum('bthd,btnhd->bthn', q.astype(jnp.float32), kg) * HEAD_DIM ** -0.5
    p, m, l = masked_softmax_stats(s, valid[None, :, None, :])
    return jnp.einsum('bthn,btnhd->bthd', p, vg), m, l


def combine_by_denominator(parts):
    mx = jnp.max(jnp.stack([m for _, m, _ in parts]), axis=0)
    num = sum(jnp.exp(m - mx) * o for o, m, _ in parts)
    den = sum(jnp.exp(m - mx) * l for _, m, l in parts)
    return num / den


def c_project(hn, pos, w_in, g_qk):
    B, T, _ = hn.shape
    z = (hn @ w_in).reshape(B, T, N_DIL, 3, DIL_HEADS, HEAD_DIM)
    qs, rows = [], []
    for g in range(N_DIL):
        qs.append(rope(rmsnorm(z[:, :, g, 0], g_qk[g, 0]), pos))
        k = rope(rmsnorm(z[:, :, g, 1], g_qk[g, 1]), pos)
        rows.append(jnp.stack([k, z[:, :, g, 2]], axis=2))
    return qs, rows


def c_mixer_prompt(hn, w_in, g_qk, w_out):
    B, T, _ = hn.shape
    qs, rows = c_project(hn, jnp.arange(T), w_in, g_qk)
    qb = min(Q_BLOCK, T)
    nb = T // qb
    q_blocks = jnp.moveaxis(jnp.stack(qs).reshape(N_DIL, B, nb, qb, DIL_HEADS, HEAD_DIM), 2, 0)

    def block(args):
        qi, t0 = args
        q_idx = t0 + jnp.arange(qb)
        parts = [dilated_attend(qi[g], rows[g][:, :, 0], rows[g][:, :, 1], q_idx, d, w // d + 1)
                 for g, (w, d) in enumerate(DIL_PAIRS)]
        return combine_by_denominator(parts)
    o = lax.map(block, (q_blocks, jnp.arange(nb) * qb))
    o = jnp.moveaxis(o, 0, 1).reshape(B, T, C_OUT).astype(hn.dtype)
    bufs = [r[:, T - min(w, T):] for r, (w, _) in zip(rows, DIL_PAIRS)]
    return o @ w_out, bufs


def c_mixer_sample(hn, bufs, past, w_in, g_qk, w_out):
    B, T, _ = hn.shape
    qs, rows = c_project(hn, past + jnp.arange(T), w_in, g_qk)
    parts, new_bufs = [], []
    for g, (w, d) in enumerate(DIL_PAIRS):
        wb = bufs[g].shape[1]
        ext = jnp.concatenate([bufs[g], rows[g].astype(bufs[g].dtype)], axis=1)
        parts.append(dilated_attend(qs[g], ext[:, :, 0], ext[:, :, 1], wb + jnp.arange(T), d, w // d + 1))
        new_bufs.append(ext[:, T:])
    o = combine_by_denominator(parts).reshape(B, T, C_OUT).astype(hn.dtype)
    return o @ w_out, new_bufs


def channel_and_ple(h, p_i, g_mlp, g_ple, w1, w2, w_proj, w_gate):
    u = rmsnorm(h, g_mlp) @ w1
    h = h + jnp.square(jax.nn.relu(u)) @ w2
    gate = jax.nn.sigmoid((rmsnorm(h, g_ple) @ w_gate).astype(jnp.float32))
    return h + (gate * (p_i @ w_proj).astype(jnp.float32)).astype(h.dtype)


def setup_inputs(seed: int = 0) -> dict:
    key = jax.random.key(seed)
    keys = iter(jax.random.split(key, 40))
    f32 = jnp.float32

    def nrm(shape, scale=1.0):
        return jax.random.normal(next(keys), shape, f32) * scale

    def gain(shape):
        return 1.0 + nrm(shape, 0.05)
    n_pages = PAST_LEN // PAGE_SIZE
    n_pool = (DEC_BATCH * n_pages * 5) // 4
    page_table = jax.random.permutation(next(keys), n_pool)[:DEC_BATCH * n_pages]
    page_table = page_table.reshape(DEC_BATCH, n_pages).astype(jnp.int32)
    kvrow = (2, NSA_KV_HEADS, HEAD_DIM)
    dilrow = (2, DIL_HEADS, HEAD_DIM)
    return {
        'x_prompt': nrm((BATCH, SEQ, D_MODEL)),
        'x_sample': nrm((DEC_BATCH, DEC_SEQ, D_MODEL)),
        'state_gla': nrm((N_A_LAYERS, DEC_BATCH, GLA_HEADS, GLA_DK, GLA_DV), 0.5),
        'cache_nsa_cmp': nrm((N_A_LAYERS, n_pool, PAGE_SIZE) + kvrow),
        'cache_nsa_slc': nrm((N_A_LAYERS, n_pool, PAGE_SIZE) + kvrow),
        'cache_nsa_win': nrm((N_A_LAYERS, DEC_BATCH, min(NSA_WINDOW, PAST_LEN)) + kvrow),
        'cache_dil_0': nrm((N_C_LAYERS, DEC_BATCH, min(DIL_PAIRS[0][0], PAST_LEN)) + dilrow),
        'cache_dil_1': nrm((N_C_LAYERS, DEC_BATCH, min(DIL_PAIRS[1][0], PAST_LEN)) + dilrow),
        'cache_dil_2': nrm((N_C_LAYERS, DEC_BATCH, min(DIL_PAIRS[2][0], PAST_LEN)) + dilrow),
        'page_table': page_table,
        'p_prompt': nrm((DEPTH, BATCH, SEQ, PLE_DIM)),
        'p_sample': nrm((DEPTH, DEC_BATCH, DEC_SEQ, PLE_DIM)),
        'norm_mix': gain((DEPTH, D_MODEL)),
        'norm_mlp': gain((DEPTH, D_MODEL)),
        'norm_ple': gain((DEPTH, D_MODEL)),
        'a_w_in': nrm((N_A_LAYERS, D_MODEL, A_IN), D_MODEL ** -0.5),
        'a_w_out': nrm((N_A_LAYERS, A_OUT, D_MODEL), A_OUT ** -0.5),
        'gla_w_a2': nrm((N_A_LAYERS, GLA_LOWRANK, GLA_HEADS * GLA_DK), GLA_LOWRANK ** -0.5),
        'gla_b_a': nrm((N_A_LAYERS, GLA_HEADS * GLA_DK), 0.01),
        'gla_g_norm': gain((N_A_LAYERS, GLA_DV)),
        'nsa_g_qk': gain((N_A_LAYERS, 4, HEAD_DIM)),
        'nsa_w_phi': nrm((N_A_LAYERS, 2, CMP_LEN, HEAD_DIM, HEAD_DIM), (CMP_LEN * HEAD_DIM) ** -0.5),
        'nsa_pe': nrm((N_A_LAYERS, 2, CMP_LEN, HEAD_DIM), 0.02),
        'c_w_in': nrm((N_C_LAYERS, D_MODEL, C_IN), D_MODEL ** -0.5),
        'c_g_qk': gain((N_C_LAYERS, N_DIL, 2, HEAD_DIM)),
        'c_w_out': nrm((N_C_LAYERS, C_OUT, D_MODEL), C_OUT ** -0.5),
        'mlp_w1': nrm((DEPTH, D_MODEL, D_FF), D_MODEL ** -0.5),
        'mlp_w2': nrm((DEPTH, D_FF, D_MODEL), D_FF ** -0.5),
        'ple_w_proj': nrm((DEPTH, PLE_DIM, D_MODEL), PLE_DIM ** -0.5),
        'ple_w_gate': nrm((DEPTH, D_MODEL, D_MODEL), D_MODEL ** -0.5),
    }


def reference(x_prompt, x_sample, state_gla, cache_nsa_cmp, cache_nsa_slc, cache_nsa_win,
              cache_dil_0, cache_dil_1, cache_dil_2, page_table, p_prompt, p_sample,
              norm_mix, norm_mlp, norm_ple, a_w_in, a_w_out, gla_w_a2, gla_b_a, gla_g_norm,
              nsa_g_qk, nsa_w_phi, nsa_pe, c_w_in, c_g_qk, c_w_out, mlp_w1, mlp_w2,
              ple_w_proj, ple_w_gate):
    past = page_table.shape[1] * PAGE_SIZE
    hp, hs = x_prompt, x_sample
    gla_p, gla_s, cmp_p, cmp_s, slc_p, slc_s, win_p, win_s = [], [], [], [], [], [], [], []
    dil_p = [[] for _ in DIL_PAIRS]
    dil_s = [[] for _ in DIL_PAIRS]
    dil_caches = (cache_dil_0, cache_dil_1, cache_dil_2)
    for i in range(DEPTH):
        j = i // 2
        hn_p, hn_s = rmsnorm(hp, norm_mix[i]), rmsnorm(hs, norm_mix[i])
        if i % 2 == 0:
            wa = (a_w_in[j], a_w_out[j], gla_w_a2[j], gla_b_a[j], gla_g_norm[j], nsa_g_qk[j], nsa_w_phi[j], nsa_pe[j])
            yp, s_p, c_p, l_p, w_p = a_mixer_prompt(hn_p, *wa)
            ys, s_s, c_s, l_s, w_s = a_mixer_sample(hn_s, state_gla[j], cache_nsa_cmp[j], cache_nsa_slc[j],
                                                    cache_nsa_win[j], page_table, *wa)
            gla_p.append(s_p); gla_s.append(s_s)
            cmp_p.append(c_p); cmp_s.append(c_s)
            slc_p.append(l_p); slc_s.append(l_s)
            win_p.append(w_p); win_s.append(w_s)
        else:
            wc = (c_w_in[j], c_g_qk[j], c_w_out[j])
            yp, bp = c_mixer_prompt(hn_p, *wc)
            ys, bs = c_mixer_sample(hn_s, [c[j] for c in dil_caches], past, *wc)
            for g in range(N_DIL):
                dil_p[g].append(bp[g])
                dil_s[g].append(bs[g])
        lw = (norm_mlp[i], norm_ple[i], mlp_w1[i], mlp_w2[i], ple_w_proj[i], ple_w_gate[i])
        hp = channel_and_ple(hp + yp, p_prompt[i], *lw)
        hs = channel_and_ple(hs + ys, p_sample[i], *lw)
    return (hp, hs,
            jnp.stack(gla_p), jnp.stack(gla_s),
            jnp.stack(cmp_p), jnp.stack(cmp_s),
            jnp.stack(slc_p), jnp.stack(slc_s),
            jnp.stack(win_p), jnp.stack(win_s),
            jnp.stack(dil_p[0]), jnp.stack(dil_s[0]),
            jnp.stack(dil_p[1]), jnp.stack(dil_s[1]),
            jnp.stack(dil_p[2]), jnp.stack(dil_s[2]))
```

```python
import functools

import jax
import jax.numpy as jnp
from jax import lax
from jax.experimental import pallas as pl
from jax.experimental.pallas import tpu as pltpu

F32 = jnp.float32
BF16 = jnp.bfloat16
HIGHEST = lax.Precision.HIGHEST

D_MODEL = 1024
PAGE_SIZE = 128
HEAD_DIM = 64
GLA_HEADS = 4
GLA_DK = 64
GLA_DV = 128
GLA_LOWRANK = 16
GLA_TAU = 16.0
GLA_CHUNK = 64
NSA_HEADS = 8
NSA_KV_HEADS = 2
NSA_GROUP = NSA_HEADS // NSA_KV_HEADS
CMP_LEN = 32
CMP_STRIDE = 16
SLC_BLOCK = 64
SLC_TOPK = 16
NSA_WINDOW = 512
DIL_PAIRS = ((128, 1), (512, 4), (2048, 16))
N_DIL = 3
DIL_HEADS = 8
DIL_KEYS = 129
D_FF = 4 * D_MODEL
PLE_DIM = 256
ROPE_THETA = 10000.0
EPS = 1e-6
NEG = -1e30
BIG = 1e30
TINY = 1e-20
REMOVED = -3e38

LANES = 128
VMEM_LIMIT = 56 * 1024 * 1024
QSCALE = HEAD_DIM ** -0.5

A_GQ, A_GK, A_GV, A_GR, A_GA, A_NQ, A_NKV, A_NG, A_END = 0, 256, 512, 1024, 1536, 1664, 2176, 2944, 3072
G_Q, G_K, G_V, G_R, G_LG, G_END = 0, 256, 512, 1024, 1536, 1792


def _cparams(*sem):
    return pltpu.CompilerParams(dimension_semantics=sem, vmem_limit_bytes=VMEM_LIMIT)


def _rms_rows(x, g):
    return x * lax.rsqrt(jnp.mean(x * x, axis=-1, keepdims=True) + EPS) * g


def _seg64_sum(s):
    lane = lax.broadcasted_iota(jnp.int32, s.shape, 1)
    for sh in (32, 16, 8, 4, 2, 1):
        up = pltpu.roll(s, LANES - sh, axis=1)
        dn = pltpu.roll(s, sh, axis=1)
        s = s + jnp.where((lane & sh) == 0, up, dn)
    return s


def _cols(x):
    return [x[:, c * LANES:(c + 1) * LANES] for c in range(x.shape[1] // LANES)]


def _rms_heads(x, g):
    out = []
    for c, xc in enumerate(_cols(x)):
        ms = _seg64_sum(xc * xc) * (1.0 / HEAD_DIM)
        out.append(xc * lax.rsqrt(ms + EPS) * g[:, c * LANES:(c + 1) * LANES])
    return jnp.concatenate(out, axis=1) if len(out) > 1 else out[0]


def _rope(x, cs, sn):
    out = []
    for xc in _cols(x):
        lane = lax.broadcasted_iota(jnp.int32, xc.shape, 1)
        swapped = jnp.where((lane & 32) == 0, pltpu.roll(xc, LANES - 32, axis=1), pltpu.roll(xc, 32, axis=1))
        out.append(xc * cs + swapped * sn)
    return jnp.concatenate(out, axis=1) if len(out) > 1 else out[0]


def _dot(a, b):
    return jnp.dot(a, b, preferred_element_type=F32)


def _dot_nt(a, b):
    return lax.dot_general(a, b, (((1,), (1,)), ((), ())), preferred_element_type=F32)


def _dot_tn(a, b):
    return lax.dot_general(a, b, (((0,), (0,)), ((), ())), preferred_element_type=F32)


def _softmax_full(s, mask):
    s = jnp.where(mask, s, NEG)
    m = jnp.max(s, axis=-1, keepdims=True)
    p = jnp.where(mask, jnp.exp(s - m), 0.0)
    l = jnp.sum(p, axis=-1, keepdims=True)
    return p, l


def _flash_step(state, s, mask, v):
    m, l, acc = state
    s = jnp.where(mask, s, NEG)
    m_new = jnp.maximum(m, jnp.max(s, axis=-1, keepdims=True))
    alpha = jnp.exp(m - m_new)
    p = jnp.where(mask, jnp.exp(s - m_new), 0.0)
    l = alpha * l + jnp.sum(p, axis=-1, keepdims=True)
    acc = alpha * acc + _dot(p.astype(BF16), v)
    return m_new, l, acc


def _proj_a_kernel(x_ref, gmix_ref, w_ref, wa2_ref, ba_ref, gqk_ref, cs_ref, sn_ref,
                   gla_ref, qc_ref, qr_ref, gates_ref, rc_ref, rs_ref, rw_ref, sbf_ref, wbf_ref):
    hn = _rms_rows(x_ref[...], gmix_ref[...])
    z = _dot(hn.astype(BF16), w_ref[...])
    cs, sn = cs_ref[...], sn_ref[...]
    gla_ref[:, G_Q:G_K] = z[:, A_GQ:A_GK] * (GLA_DK ** -0.5)
    gla_ref[:, G_K:G_LG] = z[:, A_GK:A_GA]
    pre = _dot(z[:, A_GA:A_NQ].astype(BF16), wa2_ref[...]) + ba_ref[...]
    gla_ref[:, G_LG:G_END] = jax.nn.log_sigmoid(pre) / GLA_TAU
    qn = _rms_heads(z[:, A_NQ:A_NKV], gqk_ref[0:1, :])
    qc_ref[...] = (qn * QSCALE).astype(BF16)
    qr_ref[...] = (_rope(qn, cs, sn) * QSCALE).astype(BF16)
    gates_ref[...] = jax.nn.sigmoid(z[:, A_NG:A_END])
    kv = [z[:, A_NKV + i * LANES:A_NKV + (i + 1) * LANES] for i in range(6)]
    rc_ref[:, 0:LANES] = _rms_heads(kv[0], gqk_ref[1:2, 0:LANES])
    rc_ref[:, LANES:] = kv[1]
    ks = _rope(_rms_heads(kv[2], gqk_ref[2:3, 0:LANES]), cs, sn)
    rs_ref[:, 0:LANES] = ks
    rs_ref[:, LANES:] = kv[3]
    kw = _rope(_rms_heads(kv[4], gqk_ref[3:4, 0:LANES]), cs, sn)
    rw_ref[:, 0:LANES] = kw
    rw_ref[:, LANES:] = kv[5]
    sbf_ref[:, 0:LANES] = ks.astype(BF16)
    sbf_ref[:, LANES:] = kv[3].astype(BF16)
    wbf_ref[:, 0:LANES] = kw.astype(BF16)
    wbf_ref[:, LANES:] = kv[5].astype(BF16)


def _const_spec(shape):
    nd = len(shape)
    return pl.BlockSpec(shape, lambda *_: (0,) * nd)


def _proj_a(x, gmix, w, wa2, ba, gqk, cs, sn, tm):
    n = x.shape[0]
    tiles_per_seq = cs.shape[0] // tm
    row = lambda width: pl.BlockSpec((tm, width), lambda i: (i, 0))
    tab = pl.BlockSpec((tm, LANES), lambda i: (i % tiles_per_seq, 0))
    widths = (G_END, 512, 512, LANES, 256, 256, 256, 256, 256)
    dtypes = (F32, BF16, BF16, F32, F32, F32, F32, BF16, BF16)
    return pl.pallas_call(
        _proj_a_kernel,
        grid=(n // tm,),
        in_specs=[row(D_MODEL), _const_spec(gmix.shape), _const_spec(w.shape), _const_spec(wa2.shape),
                  _const_spec(ba.shape), _const_spec(gqk.shape), tab, tab],
        out_specs=[row(wd) for wd in widths],
        out_shape=[jax.ShapeDtypeStruct((n, wd), dt) for wd, dt in zip(widths, dtypes)],
        compiler_params=_cparams("parallel"),
        name="proj_a",
    )(x, gmix, w, wa2, ba, gqk, cs, sn)


def _gla_kernel(gla_ref, s0_ref, gnorm_ref, o_ref, sout_ref, s_scr, b_scr, *, chunk, n_chunks, n_valid):
    i = pl.program_id(1)

    @pl.when(i == 0)
    def _():
        s_scr[...] = s0_ref[...]

    tri = (lax.broadcasted_iota(jnp.int32, (chunk, chunk), 0)
           >= lax.broadcasted_iota(jnp.int32, (chunk, chunk), 1)).astype(F32)
    t_idx = lax.broadcasted_iota(jnp.int32, (chunk, 1), 0)
    gnorm = gnorm_ref[...]

    def do_chunk(c, carry):
        r0 = pl.multiple_of(c * chunk, chunk)
        rows = pl.ds(r0, chunk)
        b_all = jnp.dot(tri, gla_ref[rows, G_LG:G_END], precision=HIGHEST, preferred_element_type=F32)
        b_scr[...] = b_all
        for h in range(GLA_HEADS):
            kc = slice(G_K + h * GLA_DK, G_K + (h + 1) * GLA_DK)
            vc = slice(G_V + h * GLA_DV, G_V + (h + 1) * GLA_DV)
            q = gla_ref[rows, G_Q + h * GLA_DK:G_Q + (h + 1) * GLA_DK]
            k = gla_ref[rows, kc]
            v = gla_ref[rows, vc]
            b = b_all[:, h * GLA_DK:(h + 1) * GLA_DK]
            s_prev = s_scr[h]
            o = _dot((q * jnp.exp(b)).astype(BF16), s_prev.astype(BF16))

            def key_rows(s8, o):
                base = pl.multiple_of(s8 * 8, 8)
                k8 = gla_ref[pl.ds(r0 + base, 8), kc]
                v8 = gla_ref[pl.ds(r0 + base, 8), vc]
                b8 = b_scr[pl.ds(base, 8), h * GLA_DK:(h + 1) * GLA_DK]
                for jj in range(min(8, n_valid)):
                    w = q * k8[jj:jj + 1] * jnp.exp(jnp.minimum(b - b8[jj:jj + 1], 0.0))
                    a_col = jnp.where(t_idx >= base + jj, jnp.sum(w, axis=-1, keepdims=True), 0.0)
                    o = o + a_col * v8[jj:jj + 1]
                return o
            o = lax.fori_loop(0, (n_valid + 7) // 8, key_rows, o)
            b_last = b[chunk - 1:chunk, :]
            kdec = k * jnp.exp(b_last - b)
            upd = _dot_tn(kdec.astype(BF16), v.astype(BF16))
            decay = jnp.exp(b_last)
            eye = (lax.broadcasted_iota(jnp.int32, (GLA_DK, GLA_DK), 0)
                   == lax.broadcasted_iota(jnp.int32, (GLA_DK, GLA_DK), 1))
            decay_col = jnp.sum(jnp.where(eye, decay, 0.0), axis=1, keepdims=True)
            s_scr[h] = decay_col * s_prev + upd
            on = _rms_rows(o, gnorm)
            r = gla_ref[rows, G_R + h * GLA_DV:G_R + (h + 1) * GLA_DV]
            o_ref[rows, h * GLA_DV:(h + 1) * GLA_DV] = (on * (r * jax.nn.sigmoid(r))).astype(o_ref.dtype)
        return carry
    lax.fori_loop(0, n_chunks, do_chunk, 0)

    @pl.when(i == pl.num_programs(1) - 1)
    def _():
        sout_ref[...] = s_scr[...]


def _gla(gla, s0, gnorm, *, nb, rows_per_seq, chunk, n_valid, out_dtype):
    step_rows = min(rows_per_seq, 8 * chunk)
    steps = rows_per_seq // step_rows
    kern = functools.partial(_gla_kernel, chunk=chunk, n_chunks=step_rows // chunk, n_valid=n_valid)
    return pl.pallas_call(
        kern,
        grid=(nb, steps),
        in_specs=[pl.BlockSpec((step_rows, G_END), lambda b, i: (b * steps + i, 0)),
                  pl.BlockSpec((None, GLA_HEADS, GLA_DK, GLA_DV), lambda b, i: (b, 0, 0, 0)),
                  _const_spec(gnorm.shape)],
        out_specs=[pl.BlockSpec((step_rows, GLA_HEADS * GLA_DV), lambda b, i: (b * steps + i, 0)),
                   pl.BlockSpec((None, GLA_HEADS, GLA_DK, GLA_DV), lambda b, i: (b, 0, 0, 0))],
        out_shape=[jax.ShapeDtypeStruct((nb * rows_per_seq, GLA_HEADS * GLA_DV), out_dtype),
                   jax.ShapeDtypeStruct((nb, GLA_HEADS, GLA_DK, GLA_DV), F32)],
        scratch_shapes=[pltpu.VMEM((GLA_HEADS, GLA_DK, GLA_DV), F32),
                        pltpu.VMEM((chunk, GLA_HEADS * GLA_DK), F32)],
        compiler_params=_cparams("parallel", "arbitrary"),
        name="gla",
    )(gla, s0, gnorm)


def _compress_kernel(*refs, n_parts, n_prefetch):
    refs = refs[n_prefetch:]
    x_refs = refs[:n_parts]
    wk_ref, wv_ref, pek_ref, pev_ref, kc_ref, vc_ref, sh_scr = refs[n_parts:]
    x = jnp.concatenate([r[...] for r in x_refs], axis=0) if n_parts > 1 else x_refs[0][...]
    m = x.shape[0]
    row_w = 2 * LANES

    def branch(off, w_ref, pe_ref, out_ref):
        lo = jnp.zeros((m, LANES), F32)
        hi = jnp.zeros((m, LANES), F32)
        for p in range(CMP_STRIDE):
            xp = x[:, p * row_w + off:p * row_w + off + LANES]
            lo = lo + _dot((xp + pe_ref[p:p + 1, :]).astype(BF16), w_ref[p])
            hi = hi + _dot((xp + pe_ref[CMP_STRIDE + p:CMP_STRIDE + p + 1, :]).astype(BF16), w_ref[CMP_STRIDE + p])
        sh_scr[pl.ds(0, m), :] = hi
        sh_scr[pl.ds(m, 8), :] = jnp.zeros((8, LANES), F32)
        out_ref[...] = (lo + sh_scr[pl.ds(1, m), :]).astype(out_ref.dtype)
    branch(0, wk_ref, pek_ref, kc_ref)
    branch(LANES, wv_ref, pev_ref, vc_ref)


def _compress_call(n_parts, m, grid, x_specs, out_map, out_rows, prefetch):
    kern = functools.partial(_compress_kernel, n_parts=n_parts, n_prefetch=prefetch)
    w_shape = (CMP_LEN, LANES, LANES)
    pe_shape = (CMP_LEN, LANES)
    in_specs = list(x_specs) + [_const_spec(w_shape), _const_spec(w_shape), _const_spec(pe_shape), _const_spec(pe_shape)]
    out_specs = [pl.BlockSpec((m, LANES), out_map), pl.BlockSpec((m, LANES), out_map)]
    gs = pltpu.PrefetchScalarGridSpec(num_scalar_prefetch=prefetch, grid=grid, in_specs=in_specs, out_specs=out_specs,
                                      scratch_shapes=[pltpu.VMEM((m + 8, LANES), F32)])
    return pl.pallas_call(
        kern, grid_spec=gs,
        out_shape=[jax.ShapeDtypeStruct((out_rows, LANES), BF16)] * 2,
        compiler_params=_cparams("parallel"),
        name="nsa_compress",
    )


def _compress_prompt(rows_c, nb, seq, wk, wv, pek, pev):
    m = seq // CMP_STRIDE
    x = rows_c.reshape(-1, CMP_STRIDE * 2 * LANES)
    call = _compress_call(1, m, (nb,), [pl.BlockSpec((m, CMP_STRIDE * 2 * LANES), lambda b: (b, 0))],
                          lambda b: (b, 0), nb * m, 0)
    return call(x, wk, wv, pek, pev)


def _compress_pages(pool, page_table, wk, wv, pek, pev):
    nb, n_pages = page_table.shape
    mp = PAGE_SIZE // CMP_STRIDE
    x = pool.reshape(pool.shape[0], mp, CMP_STRIDE * 2 * LANES)
    specs = [pl.BlockSpec((None, mp, CMP_STRIDE * 2 * LANES), functools.partial(lambda pg, b, pt: (pt[b, pg], 0, 0), pg))
             for pg in range(n_pages)]
    m = n_pages * mp
    call = _compress_call(n_pages, m, (nb,), specs, lambda b, pt: (b, 0), nb * m, 1)
    return call(page_table, *([x] * n_pages), wk, wv, pek, pev)


def _topk_mask(imp, qpos, n_slc):
    blk = lax.broadcasted_iota(jnp.int32, imp.shape, 1)
    cur = qpos // SLC_BLOCK
    val = jnp.where((blk == cur) | (blk == 0), BIG, jnp.where(blk <= cur, imp, NEG))
    val = jnp.where(blk < n_slc, val, REMOVED)
    sel = jnp.zeros(imp.shape, jnp.bool_)
    for _ in range(min(SLC_TOPK, n_slc)):
        mx = jnp.max(val, axis=-1, keepdims=True)
        idx = jnp.min(jnp.where(val == mx, blk, imp.shape[1]), axis=-1, keepdims=True)
        pick = blk == idx
        sel = sel | pick
        val = jnp.where(pick, REMOVED, val)
    return sel.astype(BF16)


def _block_expand(n_blocks_pad, k0, n_keys):
    blk = lax.broadcasted_iota(jnp.int32, (n_blocks_pad, n_keys), 0)
    col = lax.broadcasted_iota(jnp.int32, (n_blocks_pad, n_keys), 1)
    return (blk == (k0 + col) // SLC_BLOCK).astype(BF16)


def _flash_init(m_rows):
    return (jnp.full((m_rows, 1), NEG, F32), jnp.zeros((m_rows, 1), F32), jnp.zeros((m_rows, HEAD_DIM), F32))


def _nsa_prompt_kernel(qc_ref, qr_ref, gates_ref, kc_ref, vc_ref, ks_ref, kw_ref, mcs_ref, o_ref,
                       *, qb, seq, key_tile):
    t0 = pl.program_id(1) * qb
    qpos = t0 + lax.broadcasted_iota(jnp.int32, (qb, 1), 0)
    n_cmp_pad = kc_ref.shape[0]
    n_slc = seq // SLC_BLOCK
    hd = HEAD_DIM

    ncol = lax.broadcasted_iota(jnp.int32, (qb, n_cmp_pad), 1)
    cmask = (ncol * CMP_STRIDE + CMP_LEN - 1) <= qpos
    o_cmp, sels = [], []
    for g in range(NSA_KV_HEADS):
        kcg = kc_ref[:, g * hd:(g + 1) * hd]
        vcg = vc_ref[:, g * hd:(g + 1) * hd]
        psum = jnp.zeros((qb, n_cmp_pad), F32)
        for j in range(NSA_GROUP):
            h = g * NSA_GROUP + j
            p, l = _softmax_full(_dot_nt(qc_ref[:, h * hd:(h + 1) * hd], kcg), cmask)
            p = p / jnp.maximum(l, TINY)
            o_cmp.append(_dot(p.astype(BF16), vcg))
            psum = psum + p
        imp = jnp.dot(psum, mcs_ref[...], precision=HIGHEST, preferred_element_type=F32)
        sels.append(_topk_mask(imp, qpos, n_slc))

    n_tiles = (t0 + qb + key_tile - 1) // key_tile

    def tile(kt, states):
        k0 = pl.multiple_of(kt * key_tile, key_tile)
        kv = ks_ref[pl.ds(k0, key_tile), :]
        causal = (k0 + lax.broadcasted_iota(jnp.int32, (qb, key_tile), 1)) <= qpos
        expand = _block_expand(sels[0].shape[1], k0, key_tile)
        out = []
        for g in range(NSA_KV_HEADS):
            allowed = (_dot(sels[g], expand) > 0.5) & causal
            kg = kv[:, g * hd:(g + 1) * hd]
            vg = kv[:, LANES + g * hd:LANES + (g + 1) * hd]
            for j in range(NSA_GROUP):
                h = g * NSA_GROUP + j
                out.append(_flash_step(states[h], _dot_nt(qr_ref[:, h * hd:(h + 1) * hd], kg), allowed, vg))
        return tuple(out)
    states = lax.fori_loop(0, n_tiles, tile, tuple(_flash_init(qb) for _ in range(NSA_HEADS)))

    wlen = min(seq, NSA_WINDOW + qb)
    start = pl.multiple_of(jnp.clip(t0 - NSA_WINDOW, 0, seq - wlen), qb)
    band = kw_ref[pl.ds(start, wlen), :]
    kpos = start + lax.broadcasted_iota(jnp.int32, (qb, wlen), 1)
    wmask = (kpos <= qpos) & (kpos > qpos - NSA_WINDOW)
    for h in range(NSA_HEADS):
        g = h // NSA_GROUP
        p, l = _softmax_full(_dot_nt(qr_ref[:, h * hd:(h + 1) * hd], band[:, g * hd:(g + 1) * hd]), wmask)
        o_win = _dot(p.astype(BF16), band[:, LANES + g * hd:LANES + (g + 1) * hd]) / jnp.maximum(l, TINY)
        _, l_s, acc_s = states[h]
        o_slc = acc_s / jnp.maximum(l_s, TINY)
        gate = gates_ref[:, 3 * h:3 * h + 3]
        o = gate[:, 0:1] * o_cmp[h] + gate[:, 1:2] * o_slc + gate[:, 2:3] * o_win
        o_ref[:, h * hd:(h + 1) * hd] = o.astype(o_ref.dtype)


def _nsa_prompt(qc, qr, gates, kc, vc, ks, kw, mcs, *, nb, seq, qb):
    nq = seq // qb
    n_cmp_pad = seq // CMP_STRIDE
    key_tile = min(seq, 512)
    kern = functools.partial(_nsa_prompt_kernel, qb=qb, seq=seq, key_tile=key_tile)
    qspec = lambda wd: pl.BlockSpec((qb, wd), lambda b, i: (b * nq + i, 0))
    seqspec = lambda rows, wd: pl.BlockSpec((rows, wd), lambda b, i: (b, 0))
    return pl.pallas_call(
        kern,
        grid=(nb, nq),
        in_specs=[qspec(512), qspec(512), qspec(LANES), seqspec(n_cmp_pad, LANES), seqspec(n_cmp_pad, LANES),
                  seqspec(seq, 256), seqspec(seq, 256), _const_spec(mcs.shape)],
        out_specs=qspec(512),
        out_shape=jax.ShapeDtypeStruct((nb * seq, 512), BF16),
        compiler_params=_cparams("parallel", "arbitrary"),
        name="nsa_prompt",
    )(qc, qr, gates, kc, vc, ks, kw, mcs)


def _nsa_sample_kernel(*refs, n_pages, tail, win):
    pt_ref = refs[0]
    del pt_ref
    (qc_ref, qr_ref, gates_ref, kc_ref, vc_ref) = refs[1:6]
    page_refs = refs[6:6 + n_pages]
    news_ref, winbuf_ref, neww_ref, mcs_ref, amat_ref, o_ref = refs[6 + n_pages:]
    hd = HEAD_DIM
    past = n_pages * PAGE_SIZE
    m_rows = qc_ref.shape[1]
    n_tok = m_rows // NSA_GROUP
    tok = lax.broadcasted_iota(jnp.int32, (m_rows, 1), 0) % n_tok
    qpos = past + tok
    n_cmp_pad = kc_ref.shape[0]
    n_slc = (past + n_tok + SLC_BLOCK - 1) // SLC_BLOCK

    ncol = lax.broadcasted_iota(jnp.int32, (m_rows, n_cmp_pad), 1)
    cmask = ((ncol * CMP_STRIDE + CMP_LEN - 1) <= qpos) & (ncol < n_cmp_pad - 1)
    ks_all = jnp.concatenate([r[...].astype(BF16) for r in page_refs], axis=0)
    news = news_ref[...].astype(BF16)
    kw_all = winbuf_ref[...].astype(BF16)
    neww = neww_ref[...].astype(BF16)
    tail_pos = past + lax.broadcasted_iota(jnp.int32, (m_rows, tail), 1)
    tail_ok = (tail_pos <= qpos) & (tail_pos < past + n_tok)
    expand = _block_expand(LANES, 0, past)
    expand_tail = _block_expand(LANES, past, tail)
    wpos = past - win + lax.broadcasted_iota(jnp.int32, (m_rows, win), 1)
    wmask = (wpos <= qpos) & (wpos > qpos - NSA_WINDOW) & (wpos >= 0)
    wmask_tail = tail_ok & (tail_pos > qpos - NSA_WINDOW)

    for g in range(NSA_KV_HEADS):
        kslice = slice(g * hd, (g + 1) * hd)
        vslice = slice(LANES + g * hd, LANES + (g + 1) * hd)
        qc = qc_ref[g]
        qr = qr_ref[g]
        p, l = _softmax_full(_dot_nt(qc, kc_ref[:, kslice]), cmask)
        p = p / jnp.maximum(l, TINY)
        o_cmp = _dot(p.astype(BF16), vc_ref[:, kslice])
        psum = jnp.dot(amat_ref[...], p, precision=HIGHEST, preferred_element_type=F32)
        imp = jnp.dot(psum, mcs_ref[...], precision=HIGHEST, preferred_element_type=F32)
        sel = _topk_mask(imp, qpos, n_slc)

        st = _flash_init(m_rows)
        st = _flash_step(st, _dot_nt(qr, ks_all[:, kslice]), _dot(sel, expand) > 0.5, ks_all[:, vslice])
        st = _flash_step(st, _dot_nt(qr, news[:, kslice]), (_dot(sel, expand_tail) > 0.5) & tail_ok, news[:, vslice])
        o_slc = st[2] / jnp.maximum(st[1], TINY)

        st = _flash_init(m_rows)
        st = _flash_step(st, _dot_nt(qr, kw_all[:, kslice]), wmask, kw_all[:, vslice])
        st = _flash_step(st, _dot_nt(qr, neww[:, kslice]), wmask_tail, neww[:, vslice])
        o_win = st[2] / jnp.maximum(st[1], TINY)

        gate = gates_ref[g]
        o_ref[g] = gate[:, 0:1] * o_cmp + gate[:, 1:2] * o_slc + gate[:, 2:3] * o_win


def _nsa_sample(page_table, qc, qr, gates, kc, vc, pool_s, news, winbuf, neww, mcs, amat):
    nb, n_pages = page_table.shape
    m_rows = qc.shape[2]
    tail = news.shape[1]
    win = winbuf.shape[1]
    kern = functools.partial(_nsa_sample_kernel, n_pages=n_pages, tail=tail, win=win)
    per_b4 = lambda shp: pl.BlockSpec((None,) + shp, lambda b, pt: (b, 0, 0, 0))
    per_b3 = lambda shp: pl.BlockSpec((None,) + shp, lambda b, pt: (b, 0, 0))
    n_cmp_pad = kc.shape[0] // nb
    cmp_spec = pl.BlockSpec((n_cmp_pad, LANES), lambda b, pt: (b, 0))
    page_specs = [pl.BlockSpec((None, PAGE_SIZE, 256), functools.partial(lambda pg, b, pt: (pt[b, pg], 0, 0), pg))
                  for pg in range(n_pages)]
    gs = pltpu.PrefetchScalarGridSpec(
        num_scalar_prefetch=1, grid=(nb,),
        in_specs=[per_b4((NSA_KV_HEADS, m_rows, HEAD_DIM)), per_b4((NSA_KV_HEADS, m_rows, HEAD_DIM)),
                  per_b4((NSA_KV_HEADS, m_rows, LANES)), cmp_spec, cmp_spec] + page_specs +
                 [per_b3((tail, 256)), per_b3((win, 256)), per_b3((tail, 256)),
                  pl.BlockSpec(mcs.shape, lambda b, pt: (0, 0)), pl.BlockSpec(amat.shape, lambda b, pt: (0, 0))],
        out_specs=per_b4((NSA_KV_HEADS, m_rows, HEAD_DIM)))
    return pl.pallas_call(
        kern, grid_spec=gs,
        out_shape=jax.ShapeDtypeStruct((nb, NSA_KV_HEADS, m_rows, HEAD_DIM), F32),
        compiler_params=_cparams("parallel"),
        name="nsa_sample",
    )(page_table, qc, qr, gates, kc, vc, *([pool_s] * n_pages), news, winbuf, neww, mcs, amat)


C_GROUP_COLS = 3 * DIL_HEADS * HEAD_DIM


def _proj_c_kernel(x_ref, gmix_ref, w_ref, gq_ref, gk_ref, cs_ref, sn_ref, *out_refs):
    q_refs, r_refs = out_refs[:N_DIL], out_refs[N_DIL:]
    hn = _rms_rows(x_ref[...], gmix_ref[...]).astype(BF16)
    cs, sn = cs_ref[...], sn_ref[...]
    wd = DIL_HEADS * HEAD_DIM
    for g in range(N_DIL):
        z = _dot(hn, w_ref[:, g * C_GROUP_COLS:(g + 1) * C_GROUP_COLS])
        q = _rope(_rms_heads(z[:, 0:wd], gq_ref[g:g + 1, :]), cs, sn)
        q_refs[g][...] = (q * QSCALE).astype(BF16)
        r_refs[g][:, 0:wd] = _rope(_rms_heads(z[:, wd:2 * wd], gk_ref[g:g + 1, :]), cs, sn)
        r_refs[g][:, wd:] = z[:, 2 * wd:]


def _proj_c(x, gmix, w, gq, gk, cs, sn, tm):
    n = x.shape[0]
    tiles_per_seq = cs.shape[0] // tm
    row = lambda width: pl.BlockSpec((tm, width), lambda i: (i, 0))
    tab = pl.BlockSpec((tm, LANES), lambda i: (i % tiles_per_seq, 0))
    wd = DIL_HEADS * HEAD_DIM
    return pl.pallas_call(
        _proj_c_kernel,
        grid=(n // tm,),
        in_specs=[row(D_MODEL), _const_spec(gmix.shape), _const_spec(w.shape), _const_spec(gq.shape),
                  _const_spec(gk.shape), tab, tab],
        out_specs=[row(wd)] * N_DIL + [row(2 * wd)] * N_DIL,
        out_shape=[jax.ShapeDtypeStruct((n, wd), BF16)] * N_DIL + [jax.ShapeDtypeStruct((n, 2 * wd), F32)] * N_DIL,
        compiler_params=_cparams("parallel"),
        name="proj_c",
    )(x, gmix, w, gq, gk, cs, sn)


def _dil_prompt_kernel(q_ref, kp_ref, vp_ref, kc_ref, vc_ref, o_ref, ml_ref, *, tq):
    i = pl.program_id(2)
    row = lax.broadcasted_iota(jnp.int32, (tq, 2 * tq), 0)
    col = lax.broadcasted_iota(jnp.int32, (tq, 2 * tq), 1)
    delta = tq + row - col
    mask = (delta >= 0) & (delta < DIL_KEYS) & ((col >= tq) | (i > 0))
    k = jnp.concatenate([kp_ref[...], kc_ref[...]], axis=0).astype(BF16)
    v = jnp.concatenate([vp_ref[...], vc_ref[...]], axis=0).astype(BF16)
    lane = lax.broadcasted_iota(jnp.int32, (tq, LANES), 1)
    ml = jnp.zeros((tq, LANES), F32)
    hd = HEAD_DIM
    for h in range(DIL_HEADS):
        hs = slice(h * hd, (h + 1) * hd)
        s = jnp.where(mask, _dot_nt(q_ref[:, hs], k[:, hs]), NEG)
        m = jnp.max(s, axis=-1, keepdims=True)
        p = jnp.where(mask, jnp.exp(s - m), 0.0)
        l = jnp.sum(p, axis=-1, keepdims=True)
        o_ref[:, hs] = _dot(p.astype(BF16), v[:, hs])
        ml = jnp.where(lane == h, m, jnp.where(lane == DIL_HEADS + h, l, ml))
    ml_ref[...] = ml


def _dil_prompt(q, rows, *, nb, seq, dil, tq):
    wd = DIL_HEADS * HEAD_DIM
    assert tq >= DIL_KEYS - 1 and seq % (dil * tq) == 0
    nu = seq // dil // tq
    qv = q.reshape(nb * seq // dil, dil * wd)
    rv = rows.reshape(nb * seq // dil, dil * 2 * wd)
    cur = lambda off: (lambda b, r, i: (b * nu + i, 2 * r + off))
    prev = lambda off: (lambda b, r, i: (b * nu + jnp.maximum(i - 1, 0), 2 * r + off))
    blk = lambda imap: pl.BlockSpec((tq, wd), imap)
    o, ml = pl.pallas_call(
        functools.partial(_dil_prompt_kernel, tq=tq),
        grid=(nb, dil, nu),
        in_specs=[blk(lambda b, r, i: (b * nu + i, r)), blk(prev(0)), blk(prev(1)), blk(cur(0)), blk(cur(1))],
        out_specs=[blk(lambda b, r, i: (b * nu + i, r)), pl.BlockSpec((tq, LANES), lambda b, r, i: (b * nu + i, r))],
        out_shape=[jax.ShapeDtypeStruct(qv.shape, F32), jax.ShapeDtypeStruct((qv.shape[0], dil * LANES), F32)],
        compiler_params=_cparams("parallel", "parallel", "arbitrary"),
        name="dil_prompt",
    )(qv, rv, rv, rv, rv)
    return o.reshape(nb * seq, wd), ml.reshape(nb * seq, LANES)


def _seg64_sum_wide(x):
    return jnp.concatenate([_seg64_sum(xc) for xc in _cols(x)], axis=1)


def _dil_sample_kernel(q0_ref, q1_ref, q2_ref, n0_ref, n1_ref, n2_ref, c0_ref, c1_ref, c2_ref, o_ref, ext_scr,
                       *, n_tok):
    wd = DIL_HEADS * HEAD_DIM
    w0 = c0_ref.shape[0]
    q_refs, n_refs, c_refs = (q0_ref, q1_ref, q2_ref), (n0_ref, n1_ref, n2_ref), (c0_ref, c1_ref, c2_ref)
    ext_scr[pl.ds(0, w0), :] = c0_ref[...]
    ext_scr[pl.ds(w0, 8), :] = n0_ref[...]
    o_ref[...] = jnp.zeros(o_ref.shape, o_ref.dtype)
    for t in range(n_tok):
        parts = []
        for g in range(N_DIL):
            q_t = q_refs[g][t:t + 1, :].astype(F32)
            if g == 0:
                kt = ext_scr[pl.ds(t, w0), 0:wd]
                vt = ext_scr[pl.ds(t, w0), wd:2 * wd]
            else:
                kt = c_refs[g][:, t * 2 * wd:t * 2 * wd + wd]
                vt = c_refs[g][:, t * 2 * wd + wd:(t + 1) * 2 * wd]
            k0 = n_refs[g][t:t + 1, 0:wd]
            v0 = n_refs[g][t:t + 1, wd:2 * wd]
            s = _seg64_sum_wide(kt * q_t)
            s0 = _seg64_sum_wide(jnp.broadcast_to(k0 * q_t, (8, wd)))[0:1, :]
            m = jnp.maximum(jnp.max(s, axis=0, keepdims=True), s0)
            p = jnp.exp(s - m)
            p0 = jnp.exp(s0 - m)
            l = jnp.sum(p, axis=0, keepdims=True) + p0
            o = jnp.sum(p * vt, axis=0, keepdims=True) + p0 * v0
            parts.append((o, m, l))
        mx = functools.reduce(jnp.maximum, [m for _, m, _ in parts])
        num = sum(jnp.exp(m - mx) * o for o, m, _ in parts)
        den = sum(jnp.exp(m - mx) * l for _, m, l in parts)
        o_ref[pl.ds(t, 1), :] = num / den


def _dil_sample(qs, news, caches, *, n_tok):
    nb = qs[0].shape[0]
    wd = DIL_HEADS * HEAD_DIM
    views, cspecs = [], []
    for g, (w, d) in enumerate(DIL_PAIRS):
        assert caches[g].shape[1] == w and w // d == DIL_KEYS - 1 and n_tok <= d or g == 0
        cv = caches[g].reshape(nb, w // d, d * 2 * wd)
        views.append(cv)
        lanes = min(d, n_tok) * 2 * wd if g else 2 * wd
        cspecs.append(pl.BlockSpec((None, w // d, lanes), lambda b: (b, 0, 0)))
    per_b = lambda wdt: pl.BlockSpec((None, 8, wdt), lambda b: (b, 0, 0))
    return pl.pallas_call(
        functools.partial(_dil_sample_kernel, n_tok=n_tok),
        grid=(nb,),
        in_specs=[per_b(wd)] * N_DIL + [per_b(2 * wd)] * N_DIL + cspecs,
        out_specs=per_b(wd),
        out_shape=jax.ShapeDtypeStruct((nb, 8, wd), F32),
        scratch_shapes=[pltpu.VMEM((DIL_PAIRS[0][0] + 8, 2 * wd), F32)],
        compiler_params=_cparams("parallel"),
        name="dil_sample",
    )(*qs, *news, *views)


FF_CHUNK = 1024


def _post_kernel(*refs, n_cat, combine):
    h_ref = refs[0]
    mix_refs = refs[1:1 + (2 * N_DIL if combine else n_cat)]
    (wo_ref, gmlp_ref, w1_ref, w2_ref, gple_ref, wg_ref, p_ref, wp_ref, out_ref) = refs[1 + len(mix_refs):1 + len(mix_refs) + 9]
    hd = HEAD_DIM
    if combine:
        comb_scr = refs[-1]
        o_refs, ml_refs = mix_refs[:N_DIL], mix_refs[N_DIL:]
        for h in range(DIL_HEADS):
            ms = [r[:, h:h + 1] for r in ml_refs]
            ls = [r[:, DIL_HEADS + h:DIL_HEADS + h + 1] for r in ml_refs]
            mx = functools.reduce(jnp.maximum, ms)
            ws = [jnp.exp(m - mx) for m in ms]
            num = sum(w * r[:, h * hd:(h + 1) * hd] for w, r in zip(ws, o_refs))
            den = sum(w * l for w, l in zip(ws, ls))
            comb_scr[:, h * hd:(h + 1) * hd] = (num / den).astype(BF16)
        y = _dot(comb_scr[...], wo_ref[...])
    else:
        y = None
        for k, r in enumerate(mix_refs):
            part = _dot(r[...].astype(BF16), wo_ref[k * r.shape[1]:(k + 1) * r.shape[1], :])
            y = part if y is None else y + part
    h1 = h_ref[...] + y
    hn = _rms_rows(h1, gmlp_ref[...]).astype(BF16)
    acc = jnp.zeros(h1.shape, F32)
    for c in range(D_FF // FF_CHUNK):
        u = _dot(hn, w1_ref[:, c * FF_CHUNK:(c + 1) * FF_CHUNK])
        acc = acc + _dot(jnp.square(jnp.maximum(u, 0.0)).astype(BF16), w2_ref[c * FF_CHUNK:(c + 1) * FF_CHUNK, :])
    h2 = h1 + acc
    gate = jax.nn.sigmoid(_dot(_rms_rows(h2, gple_ref[...]).astype(BF16), wg_ref[...]))
    out_ref[...] = h2 + gate * _dot(p_ref[...].astype(BF16), wp_ref[...])


def _post(h, mix, wo, gmlp, w1, w2, gple, wg, p, wp, *, combine, tm):
    n = h.shape[0]
    row = lambda a: pl.BlockSpec((tm, a.shape[1]), lambda i: (i, 0))
    weights = (wo, gmlp, w1, w2, gple, wg)
    wspec = lambda a: pl.BlockSpec(a.shape, lambda i: (0, 0), pipeline_mode=pl.Buffered(1))
    kern = functools.partial(_post_kernel, n_cat=len(mix), combine=combine)
    return pl.pallas_call(
        kern,
        grid=(n // tm,),
        in_specs=[row(h)] + [row(a) for a in mix] + [wspec(a) for a in weights] + [row(p), wspec(wp)],
        out_specs=row(h),
        out_shape=jax.ShapeDtypeStruct(h.shape, F32),
        scratch_shapes=[pltpu.VMEM((tm, DIL_HEADS * HEAD_DIM), BF16)] if combine else [],
        compiler_params=_cparams("parallel"),
        name="post",
    )(h, *mix, *weights, p, wp)


def _rope_tables(pos):
    half = HEAD_DIM // 2
    freq = ROPE_THETA ** (-jnp.arange(half, dtype=F32) / half)
    ang = pos.astype(F32)[:, None] * freq[None, :]
    cos, sin = jnp.cos(ang), jnp.sin(ang)
    return jnp.tile(jnp.concatenate([cos, cos], axis=1), (1, 2)), jnp.tile(jnp.concatenate([-sin, sin], axis=1), (1, 2))


def _cmp_to_slc(n_cmp, n_slc, rows_pad, cols_pad):
    cs = jnp.arange(n_cmp)[:, None] * CMP_STRIDE
    ss = jnp.arange(n_slc)[None, :] * SLC_BLOCK
    ov = jnp.maximum(jnp.minimum(cs + CMP_LEN, ss + SLC_BLOCK) - jnp.maximum(cs, ss), 0)
    return jnp.pad(ov.astype(F32) / CMP_LEN, ((0, rows_pad - n_cmp), (0, cols_pad - n_slc)))


def _round_up(x, m):
    return (x + m - 1) // m * m


def _block_diag2(w):
    z = jnp.zeros_like(w)
    return jnp.concatenate([jnp.concatenate([w, z], axis=2), jnp.concatenate([z, w], axis=2)], axis=1)


def kernel(x_prompt, x_sample, state_gla, cache_nsa_cmp, cache_nsa_slc, cache_nsa_win, cache_dil_0, cache_dil_1,
           cache_dil_2, page_table, p_prompt, p_sample, norm_mix, norm_mlp, norm_ple, a_w_in, a_w_out, gla_w_a2,
           gla_b_a, gla_g_norm, nsa_g_qk, nsa_w_phi, nsa_pe, c_w_in, c_g_qk, c_w_out, mlp_w1, mlp_w2, ple_w_proj,
           ple_w_gate):
    nb_p, seq, _ = x_prompt.shape
    nb_s, n_tok, _ = x_sample.shape
    depth = norm_mix.shape[0]
    past = page_table.shape[1] * PAGE_SIZE
    assert n_tok < CMP_STRIDE and n_tok <= 8 and seq % 512 == 0
    tm_p = 256
    tm_s = min(256, nb_s * n_tok)
    cs_p, sn_p = _rope_tables(jnp.arange(seq))
    cs_s, sn_s = _rope_tables(past + jnp.arange(nb_s * n_tok) % n_tok)
    hp = x_prompt.reshape(nb_p * seq, D_MODEL)
    hs = x_sample.reshape(nb_s * n_tok, D_MODEL)
    dil_caches = (cache_dil_0, cache_dil_1, cache_dil_2)
    outs = {k: [] for k in ("gla_p", "gla_s", "cmp_p", "cmp_s", "slc_p", "slc_s", "win_p", "win_s")}
    dil_p = [[] for _ in DIL_PAIRS]
    dil_s = [[] for _ in DIL_PAIRS]
    kvrow = (2, NSA_KV_HEADS, HEAD_DIM)
    wd = DIL_HEADS * HEAD_DIM

    for i in range(depth):
        j = i // 2
        gmix = norm_mix[i][None, :]
        if i % 2 == 0:
            w_in = a_w_in[j]
            pad_cols = lambda a, width: jnp.pad(a, ((0, 0), (0, width - a.shape[1])))
            w_pad = jnp.concatenate([w_in[:, :1536], pad_cols(w_in[:, 1536:1552], LANES), w_in[:, 1552:2832],
                                     pad_cols(w_in[:, 2832:], LANES)], axis=1).astype(BF16)
            wa2 = jnp.pad(gla_w_a2[j], ((0, LANES - GLA_LOWRANK), (0, 0))).astype(BF16)
            ba = gla_b_a[j][None, :]
            gqk = jnp.tile(nsa_g_qk[j], (1, NSA_HEADS))
            gnorm = gla_g_norm[j][None, :]
            wk, wv = (_block_diag2(nsa_w_phi[j, c]).astype(BF16) for c in range(2))
            pek, pev = (jnp.tile(nsa_pe[j, c], (1, 2)) for c in range(2))
            wo = a_w_out[j].astype(BF16)

            gla, qc, qr, gates, rc, rs, rw, sbf, wbf = _proj_a(hp, gmix, w_pad, wa2, ba, gqk, cs_p, sn_p, tm_p)
            s0 = jnp.zeros((nb_p, GLA_HEADS, GLA_DK, GLA_DV), F32)
            chunk = min(GLA_CHUNK, seq)
            o_gla, s_p = _gla(gla, s0, gnorm, nb=nb_p, rows_per_seq=seq, chunk=chunk, n_valid=chunk, out_dtype=BF16)
            kc, vc = _compress_prompt(rc, nb_p, seq, wk, wv, pek, pev)
            n_cmp_pad = seq // CMP_STRIDE
            mcs = _cmp_to_slc(n_cmp_pad - 1, seq // SLC_BLOCK, n_cmp_pad, _round_up(seq // SLC_BLOCK, LANES))
            o_nsa = _nsa_prompt(qc, qr, gates, kc, vc, sbf, wbf, mcs, nb=nb_p, seq=seq, qb=128)
            mix_p = (o_gla, o_nsa)
            outs["gla_p"].append(s_p)
            outs["cmp_p"].append(rc.reshape((nb_p, seq) + kvrow))
            outs["slc_p"].append(rs.reshape((nb_p, seq) + kvrow))
            outs["win_p"].append(rw.reshape((nb_p, seq) + kvrow)[:, seq - min(NSA_WINDOW, seq):])

            gla, qc, qr, gates, rc, rs, rw, _, _ = _proj_a(hs, gmix, w_pad, wa2, ba, gqk, cs_s, sn_s, tm_s)
            gla8 = jnp.pad(gla.reshape(nb_s, n_tok, G_END), ((0, 0), (0, 8 - n_tok), (0, 0))).reshape(nb_s * 8, G_END)
            o_gla8, s_s = _gla(gla8, state_gla[j], gnorm, nb=nb_s, rows_per_seq=8, chunk=8, n_valid=n_tok, out_dtype=F32)
            o_gla = o_gla8.reshape(nb_s, 8, -1)[:, :n_tok].reshape(nb_s * n_tok, -1)
            kc, vc = _compress_pages(cache_nsa_cmp[j].reshape(-1, PAGE_SIZE, 256), page_table, wk, wv, pek, pev)
            n_cmp_pad = past // CMP_STRIDE
            n_slc = (past + n_tok + SLC_BLOCK - 1) // SLC_BLOCK
            mcs = _cmp_to_slc(n_cmp_pad - 1, n_slc, n_cmp_pad, LANES)
            m_rows = NSA_GROUP * n_tok
            regroup = lambda a: a.reshape(nb_s, n_tok, NSA_KV_HEADS, NSA_GROUP, -1).transpose(0, 2, 3, 1, 4).reshape(
                nb_s, NSA_KV_HEADS, m_rows, -1)
            g16 = jnp.pad(regroup(gates[:, :3 * NSA_HEADS]), ((0, 0), (0, 0), (0, 0), (0, LANES - 3)))
            tok_id = jnp.arange(m_rows) % n_tok
            amat = (tok_id[:, None] == tok_id[None, :]).astype(F32)
            pad_tail = lambda a: jnp.pad(a.reshape(nb_s, n_tok, 256), ((0, 0), (0, LANES - n_tok), (0, 0)))
            win_buf = cache_nsa_win[j].reshape(nb_s, -1, 256)
            o16 = _nsa_sample(page_table, regroup(qc), regroup(qr), g16, kc, vc,
                              cache_nsa_slc[j].reshape(-1, PAGE_SIZE, 256), pad_tail(rs), win_buf, pad_tail(rw),
                              mcs, amat)
            o_nsa = o16.reshape(nb_s, NSA_KV_HEADS, NSA_GROUP, n_tok, HEAD_DIM).transpose(0, 3, 1, 2, 4).reshape(
                nb_s * n_tok, NSA_HEADS * HEAD_DIM)
            mix_s = (o_gla, o_nsa)
            outs["gla_s"].append(s_s)
            outs["cmp_s"].append(rc.reshape((nb_s, n_tok) + kvrow))
            outs["slc_s"].append(rs.reshape((nb_s, n_tok) + kvrow))
            all_w = jnp.concatenate([cache_nsa_win[j], rw.reshape((nb_s, n_tok) + kvrow)], axis=1)
            outs["win_s"].append(all_w[:, n_tok:])
            combine = False
        else:
            w_c = c_w_in[j].astype(BF16)
            gq = jnp.tile(c_g_qk[j][:, 0], (1, DIL_HEADS))
            gk = jnp.tile(c_g_qk[j][:, 1], (1, DIL_HEADS))
            wo = c_w_out[j].astype(BF16)
            dilrow = (2, DIL_HEADS, HEAD_DIM)

            res = _proj_c(hp, gmix, w_c, gq, gk, cs_p, sn_p, tm_p)
            qs, rows = res[:N_DIL], res[N_DIL:]
            o_parts, ml_parts = [], []
            for g, (w, d) in enumerate(DIL_PAIRS):
                o_g, ml_g = _dil_prompt(qs[g], rows[g], nb=nb_p, seq=seq, dil=d, tq=128)
                o_parts.append(o_g)
                ml_parts.append(ml_g)
                dil_p[g].append(rows[g].reshape((nb_p, seq) + dilrow)[:, seq - min(w, seq):])
            mix_p = tuple(o_parts) + tuple(ml_parts)

            res = _proj_c(hs, gmix, w_c, gq, gk, cs_s, sn_s, tm_s)
            qs, rows = res[:N_DIL], res[N_DIL:]
            pad8 = lambda a: jnp.pad(a.reshape(nb_s, n_tok, -1), ((0, 0), (0, 8 - n_tok), (0, 0)))
            caches = [c[j].reshape(nb_s, -1, 2 * wd) for c in dil_caches]
            o8 = _dil_sample([pad8(q) for q in qs], [pad8(r) for r in rows], caches, n_tok=n_tok)
            mix_s = (o8[:, :n_tok].reshape(nb_s * n_tok, wd),)
            for g in range(N_DIL):
                ext = jnp.concatenate([dil_caches[g][j], rows[g].reshape((nb_s, n_tok) + dilrow)], axis=1)
                dil_s[g].append(ext[:, n_tok:])
            combine = True

        lw = (norm_mlp[i][None, :], mlp_w1[i].astype(BF16), mlp_w2[i].astype(BF16), norm_ple[i][None, :],
              ple_w_gate[i].astype(BF16))
        wp = ple_w_proj[i].astype(BF16)
        hp = _post(hp, mix_p, wo, *lw, p_prompt[i].reshape(nb_p * seq, PLE_DIM), wp, combine=combine, tm=tm_p)
        hs = _post(hs, mix_s, wo, *lw, p_sample[i].reshape(nb_s * n_tok, PLE_DIM), wp, combine=False, tm=tm_s)

    st = jnp.stack
    return (hp.reshape(x_prompt.shape), hs.reshape(x_sample.shape),
            st(outs["gla_p"]), st(outs["gla_s"]), st(outs["cmp_p"]), st(outs["cmp_s"]),
            st(outs["slc_p"]), st(outs["slc_s"]), st(outs["win_p"]), st(outs["win_s"]),
            st(dil_p[0]), st(dil_s[0]), st(dil_p[1]), st(dil_s[1]), st(dil_p[2]), st(dil_s[2]))
```

```python
import functools

import jax
import jax.numpy as jnp
from jax import lax
from jax.experimental import pallas as pl
from jax.experimental.pallas import tpu as pltpu

F32 = jnp.float32
BF16 = jnp.bfloat16
HIGHEST = lax.Precision.HIGHEST

D_MODEL = 1024
PAGE_SIZE = 128
HEAD_DIM = 64
GLA_HEADS = 4
GLA_DK = 64
GLA_DV = 128
GLA_LOWRANK = 16
GLA_TAU = 16.0
GLA_CHUNK = 64
GLA_SUB = 16
NSA_HEADS = 8
NSA_KV_HEADS = 2
NSA_GROUP = NSA_HEADS // NSA_KV_HEADS
CMP_LEN = 32
CMP_STRIDE = 16
SLC_BLOCK = 64
SLC_TOPK = 16
NSA_WINDOW = 512
DIL_PAIRS = ((128, 1), (512, 4), (2048, 16))
N_DIL = 3
DIL_HEADS = 8
DIL_KEYS = 129
D_FF = 4 * D_MODEL
PLE_DIM = 256
ROPE_THETA = 10000.0
EPS = 1e-6
NEG = -1e30
BIG = 1e30
TINY = 1e-20
REMOVED = -3e38

LANES = 128
VMEM_LIMIT = 56 * 1024 * 1024
QSCALE = HEAD_DIM ** -0.5

A_GQ, A_GK, A_GV, A_GR, A_GA, A_NQ, A_NKV, A_NG, A_END = 0, 256, 512, 1024, 1536, 1664, 2176, 2944, 3072
G_Q, G_K, G_V, G_R, G_LG, G_END = 0, 256, 512, 1024, 1536, 1792


def _cparams(*sem):
    return pltpu.CompilerParams(dimension_semantics=sem, vmem_limit_bytes=VMEM_LIMIT)


def _rms_rows(x, g):
    return x * lax.rsqrt(jnp.mean(x * x, axis=-1, keepdims=True) + EPS) * g


def _cols(x):
    return [x[:, c * LANES:(c + 1) * LANES] for c in range(x.shape[1] // LANES)]


def _rms_heads(x, g):
    out = []
    for c, xc in enumerate(_cols(x)):
        low = lax.broadcasted_iota(jnp.int32, xc.shape, 1) < HEAD_DIM
        sq = xc * xc
        s_lo = jnp.sum(jnp.where(low, sq, 0.0), axis=-1, keepdims=True)
        s_hi = jnp.sum(jnp.where(low, 0.0, sq), axis=-1, keepdims=True)
        ms = jnp.where(low, s_lo, s_hi) * (1.0 / HEAD_DIM)
        out.append(xc * lax.rsqrt(ms + EPS) * g[:, c * LANES:(c + 1) * LANES])
    return jnp.concatenate(out, axis=1) if len(out) > 1 else out[0]


def _rope(x, cs, sn):
    out = []
    for xc in _cols(x):
        lane = lax.broadcasted_iota(jnp.int32, xc.shape, 1)
        swapped = jnp.where((lane & 32) == 0, pltpu.roll(xc, LANES - 32, axis=1), pltpu.roll(xc, 32, axis=1))
        out.append(xc * cs + swapped * sn)
    return jnp.concatenate(out, axis=1) if len(out) > 1 else out[0]


def _dot(a, b):
    return jnp.dot(a, b, preferred_element_type=F32)


def _dot_nt(a, b):
    return lax.dot_general(a, b, (((1,), (1,)), ((), ())), preferred_element_type=F32)


def _dot_tn(a, b):
    return lax.dot_general(a, b, (((0,), (0,)), ((), ())), preferred_element_type=F32)


def _softmax_full(s, mask):
    s = jnp.where(mask, s, NEG)
    m = jnp.max(s, axis=-1, keepdims=True)
    p = jnp.where(mask, jnp.exp(s - m), 0.0)
    l = jnp.sum(p, axis=-1, keepdims=True)
    return p, l


def _flash_step(state, s, mask, v):
    m, l, acc = state
    s = jnp.where(mask, s, NEG)
    m_new = jnp.maximum(m, jnp.max(s, axis=-1, keepdims=True))
    alpha = jnp.exp(m - m_new)
    p = jnp.where(mask, jnp.exp(s - m_new), 0.0)
    l = alpha * l + jnp.sum(p, axis=-1, keepdims=True)
    acc = alpha * acc + _dot(p.astype(BF16), v)
    return m_new, l, acc


def _flash_step_bias(state, s, v):
    m, l, acc = state
    m_new = jnp.maximum(m, jnp.max(s, axis=-1, keepdims=True))
    alpha = jnp.exp(m - m_new)
    p = jnp.exp(s - m_new)
    l = alpha * l + jnp.sum(p, axis=-1, keepdims=True)
    acc = alpha * acc + _dot(p.astype(BF16), v)
    return m_new, l, acc


def _proj_a_kernel(x_ref, gmix_ref, w_ref, wa2_ref, ba_ref, gqk_ref, cs_ref, sn_ref,
                   gla_ref, qc_ref, qr_ref, gates_ref, rc_ref, rs_ref, rw_ref, sbf_ref, wbf_ref):
    hn = _rms_rows(x_ref[...], gmix_ref[...])
    z = _dot(hn.astype(BF16), w_ref[...])
    cs, sn = cs_ref[...], sn_ref[...]
    gla_ref[:, G_Q:G_K] = z[:, A_GQ:A_GK] * (GLA_DK ** -0.5)
    gla_ref[:, G_K:G_LG] = z[:, A_GK:A_GA]
    pre = _dot(z[:, A_GA:A_NQ].astype(BF16), wa2_ref[...]) + ba_ref[...]
    gla_ref[:, G_LG:G_END] = jax.nn.log_sigmoid(pre) / GLA_TAU
    qn = _rms_heads(z[:, A_NQ:A_NKV], gqk_ref[0:1, :])
    qc_ref[...] = (qn * QSCALE).astype(BF16)
    qr_ref[...] = (_rope(qn, cs, sn) * QSCALE).astype(BF16)
    gates_ref[...] = jax.nn.sigmoid(z[:, A_NG:A_END])
    kv = [z[:, A_NKV + i * LANES:A_NKV + (i + 1) * LANES] for i in range(6)]
    rc_ref[:, 0:LANES] = _rms_heads(kv[0], gqk_ref[1:2, 0:LANES])
    rc_ref[:, LANES:] = kv[1]
    ks = _rope(_rms_heads(kv[2], gqk_ref[2:3, 0:LANES]), cs, sn)
    rs_ref[:, 0:LANES] = ks
    rs_ref[:, LANES:] = kv[3]
    kw = _rope(_rms_heads(kv[4], gqk_ref[3:4, 0:LANES]), cs, sn)
    rw_ref[:, 0:LANES] = kw
    rw_ref[:, LANES:] = kv[5]
    sbf_ref[:, 0:LANES] = ks.astype(BF16)
    sbf_ref[:, LANES:] = kv[3].astype(BF16)
    wbf_ref[:, 0:LANES] = kw.astype(BF16)
    wbf_ref[:, LANES:] = kv[5].astype(BF16)


def _const_spec(shape):
    nd = len(shape)
    return pl.BlockSpec(shape, lambda *_: (0,) * nd)


def _proj_a(x, gmix, w, wa2, ba, gqk, cs, sn, tm):
    n = x.shape[0]
    tiles_per_seq = cs.shape[0] // tm
    row = lambda width: pl.BlockSpec((tm, width), lambda i: (i, 0))
    tab = pl.BlockSpec((tm, LANES), lambda i: (i % tiles_per_seq, 0))
    widths = (G_END, 512, 512, LANES, 256, 256, 256, 256, 256)
    dtypes = (F32, BF16, BF16, F32, F32, F32, F32, BF16, BF16)
    return pl.pallas_call(
        _proj_a_kernel,
        grid=(n // tm,),
        in_specs=[row(D_MODEL), _const_spec(gmix.shape), _const_spec(w.shape), _const_spec(wa2.shape),
                  _const_spec(ba.shape), _const_spec(gqk.shape), tab, tab],
        out_specs=[row(wd) for wd in widths],
        out_shape=[jax.ShapeDtypeStruct((n, wd), dt) for wd, dt in zip(widths, dtypes)],
        compiler_params=_cparams("parallel"),
        name="proj_a",
    )(x, gmix, w, wa2, ba, gqk, cs, sn)


def _gla_kernel(gla_ref, s0_ref, gnorm_ref, o_ref, sout_ref, s_scr, *, chunk, sub, n_chunks, n_valid):
    i = pl.program_id(1)
    n_pairs = GLA_HEADS // 2
    n_sub = chunk // sub
    dk2, dv2 = 2 * GLA_DK, 2 * GLA_DV

    @pl.when(i == 0)
    def _():
        s_scr[...] = jnp.zeros(s_scr.shape, F32)
        for h in range(GLA_HEADS):
            p, e = divmod(h, 2)
            s_scr[p, e * GLA_DK:(e + 1) * GLA_DK, e * GLA_DV:(e + 1) * GLA_DV] = s0_ref[h]

    tri = (lax.broadcasted_iota(jnp.int32, (chunk, chunk), 0)
           >= lax.broadcasted_iota(jnp.int32, (chunk, chunk), 1)).astype(F32)
    low = lax.broadcasted_iota(jnp.int32, (chunk, dk2), 1) < GLA_DK
    low_sub = lax.broadcasted_iota(jnp.int32, (sub, dk2), 1) < GLA_DK
    t_sub = lax.broadcasted_iota(jnp.int32, (sub, 1), 0)
    row_blk = lax.broadcasted_iota(jnp.int32, (chunk, chunk), 0) // sub
    col_idx = lax.broadcasted_iota(jnp.int32, (chunk, chunk), 1)
    diag_blocks = ((lax.broadcasted_iota(jnp.int32, (dk2, dv2), 0) < GLA_DK)
                   == (lax.broadcasted_iota(jnp.int32, (dk2, dv2), 1) < GLA_DV))
    eye = (lax.broadcasted_iota(jnp.int32, (dk2, dk2), 0) == lax.broadcasted_iota(jnp.int32, (dk2, dk2), 1))
    gnorm = gnorm_ref[...]

    def do_chunk(c, carry):
        rows = pl.ds(pl.multiple_of(c * chunk, chunk), chunk)
        b_all = jnp.dot(tri, gla_ref[rows, G_LG:G_END], precision=HIGHEST, preferred_element_type=F32)
        for p in range(n_pairs):
            q2 = gla_ref[rows, G_Q + p * dk2:G_Q + (p + 1) * dk2]
            k2 = gla_ref[rows, G_K + p * dk2:G_K + (p + 1) * dk2]
            v2 = gla_ref[rows, G_V + p * dv2:G_V + (p + 1) * dv2]
            b2 = b_all[:, p * dk2:(p + 1) * dk2]
            s_pair = s_scr[p]
            o2 = _dot((q2 * jnp.exp(b2)).astype(BF16), s_pair.astype(BF16))

            if n_sub > 1:
                bref_rows = jnp.concatenate(
                    [b2[0:sub]] + [jnp.broadcast_to(b2[i * sub - 1:i * sub], (sub, dk2)) for i in range(1, n_sub)], axis=0)
                qt = q2 * jnp.exp(jnp.minimum(b2 - bref_rows, 0.0))
                qt_even = jnp.where(low, qt, 0.0).astype(BF16)
                qt_odd = jnp.where(low, 0.0, qt).astype(BF16)
                a_even = jnp.zeros((chunk, chunk), F32)
                a_odd = jnp.zeros((chunk, chunk), F32)
                for i in range(1, n_sub):
                    kt = (k2 * jnp.exp(jnp.minimum(b2[i * sub - 1:i * sub] - b2, 0.0))).astype(BF16)
                    take = (row_blk == i) & (col_idx < i * sub)
                    a_even = jnp.where(take, _dot_nt(qt_even, kt), a_even)
                    a_odd = jnp.where(take, _dot_nt(qt_odd, kt), a_odd)
                o2 = o2 + jnp.concatenate([_dot(a_even.astype(BF16), v2[:, 0:GLA_DV].astype(BF16)),
                                           _dot(a_odd.astype(BF16), v2[:, GLA_DV:].astype(BF16))], axis=1)

            diag = []
            for i in range(n_sub):
                sl = slice(i * sub, (i + 1) * sub)
                qs, ks, bs, vs = q2[sl], k2[sl], b2[sl], v2[sl]
                od = jnp.zeros((sub, dv2), F32)
                for jj in range(max(0, min(sub, n_valid - i * sub))):
                    w = qs * ks[jj:jj + 1] * jnp.exp(jnp.minimum(bs - bs[jj:jj + 1], 0.0))
                    keep = t_sub >= jj
                    a_e = jnp.where(keep, jnp.sum(jnp.where(low_sub, w, 0.0), axis=-1, keepdims=True), 0.0)
                    a_o = jnp.where(keep, jnp.sum(jnp.where(low_sub, 0.0, w), axis=-1, keepdims=True), 0.0)
                    od = od + jnp.concatenate([a_e * vs[jj:jj + 1, 0:GLA_DV], a_o * vs[jj:jj + 1, GLA_DV:]], axis=1)
                diag.append(od)
            o2 = o2 + (jnp.concatenate(diag, axis=0) if n_sub > 1 else diag[0])

            b_last = b2[chunk - 1:chunk, :]
            kdec = k2 * jnp.exp(b_last - b2)
            upd = _dot_tn(kdec.astype(BF16), v2.astype(BF16))
            decay_col = jnp.sum(jnp.where(eye, jnp.exp(b_last), 0.0), axis=1, keepdims=True)
            s_scr[p] = decay_col * s_pair + jnp.where(diag_blocks, upd, 0.0)
            for e in range(2):
                h = 2 * p + e
                on = _rms_rows(o2[:, e * GLA_DV:(e + 1) * GLA_DV], gnorm)
                r = gla_ref[rows, G_R + h * GLA_DV:G_R + (h + 1) * GLA_DV]
                o_ref[rows, h * GLA_DV:(h + 1) * GLA_DV] = (on * (r * jax.nn.sigmoid(r))).astype(o_ref.dtype)
        return carry
    lax.fori_loop(0, n_chunks, do_chunk, 0)

    @pl.when(i == pl.num_programs(1) - 1)
    def _():
        for h in range(GLA_HEADS):
            p, e = divmod(h, 2)
            sout_ref[h] = s_scr[p, e * GLA_DK:(e + 1) * GLA_DK, e * GLA_DV:(e + 1) * GLA_DV]


def _gla(gla, s0, gnorm, *, nb, rows_per_seq, chunk, n_valid, out_dtype):
    step_rows = min(rows_per_seq, 8 * chunk)
    steps = rows_per_seq // step_rows
    kern = functools.partial(_gla_kernel, chunk=chunk, sub=min(chunk, GLA_SUB), n_chunks=step_rows // chunk,
                             n_valid=n_valid)
    return pl.pallas_call(
        kern,
        grid=(nb, steps),
        in_specs=[pl.BlockSpec((step_rows, G_END), lambda b, i: (b * steps + i, 0)),
                  pl.BlockSpec((None, GLA_HEADS, GLA_DK, GLA_DV), lambda b, i: (b, 0, 0, 0)),
                  _const_spec(gnorm.shape)],
        out_specs=[pl.BlockSpec((step_rows, GLA_HEADS * GLA_DV), lambda b, i: (b * steps + i, 0)),
                   pl.BlockSpec((None, GLA_HEADS, GLA_DK, GLA_DV), lambda b, i: (b, 0, 0, 0))],
        out_shape=[jax.ShapeDtypeStruct((nb * rows_per_seq, GLA_HEADS * GLA_DV), out_dtype),
                   jax.ShapeDtypeStruct((nb, GLA_HEADS, GLA_DK, GLA_DV), F32)],
        scratch_shapes=[pltpu.VMEM((GLA_HEADS // 2, 2 * GLA_DK, 2 * GLA_DV), F32)],
        compiler_params=_cparams("parallel", "arbitrary"),
        name="gla",
    )(gla, s0, gnorm)


def _compress_kernel(*refs, n_parts, n_prefetch):
    refs = refs[n_prefetch:]
    x_refs = refs[:n_parts]
    wk_ref, wv_ref, pek_ref, pev_ref, kc_ref, vc_ref, sh_scr = refs[n_parts:]
    x = jnp.concatenate([r[...] for r in x_refs], axis=0) if n_parts > 1 else x_refs[0][...]
    m = x.shape[0]
    row_w = 2 * LANES

    def branch(off, w_ref, pe_ref, out_ref):
        lo = jnp.zeros((m, LANES), F32)
        hi = jnp.zeros((m, LANES), F32)
        for p in range(CMP_STRIDE):
            xp = x[:, p * row_w + off:p * row_w + off + LANES]
            lo = lo + _dot((xp + pe_ref[p:p + 1, :]).astype(BF16), w_ref[p])
            hi = hi + _dot((xp + pe_ref[CMP_STRIDE + p:CMP_STRIDE + p + 1, :]).astype(BF16), w_ref[CMP_STRIDE + p])
        sh_scr[pl.ds(0, m), :] = hi
        sh_scr[pl.ds(m, 8), :] = jnp.zeros((8, LANES), F32)
        out_ref[...] = (lo + sh_scr[pl.ds(1, m), :]).astype(out_ref.dtype)
    branch(0, wk_ref, pek_ref, kc_ref)
    branch(LANES, wv_ref, pev_ref, vc_ref)


def _compress_call(n_parts, m, grid, x_specs, out_map, out_rows, prefetch):
    kern = functools.partial(_compress_kernel, n_parts=n_parts, n_prefetch=prefetch)
    w_shape = (CMP_LEN, LANES, LANES)
    pe_shape = (CMP_LEN, LANES)
    in_specs = list(x_specs) + [_const_spec(w_shape), _const_spec(w_shape), _const_spec(pe_shape), _const_spec(pe_shape)]
    out_specs = [pl.BlockSpec((m, LANES), out_map), pl.BlockSpec((m, LANES), out_map)]
    gs = pltpu.PrefetchScalarGridSpec(num_scalar_prefetch=prefetch, grid=grid, in_specs=in_specs, out_specs=out_specs,
                                      scratch_shapes=[pltpu.VMEM((m + 8, LANES), F32)])
    return pl.pallas_call(
        kern, grid_spec=gs,
        out_shape=[jax.ShapeDtypeStruct((out_rows, LANES), BF16)] * 2,
        compiler_params=_cparams("parallel"),
        name="nsa_compress",
    )


def _compress_prompt(rows_c, nb, seq, wk, wv, pek, pev):
    m = seq // CMP_STRIDE
    x = rows_c.reshape(-1, CMP_STRIDE * 2 * LANES)
    call = _compress_call(1, m, (nb,), [pl.BlockSpec((m, CMP_STRIDE * 2 * LANES), lambda b: (b, 0))],
                          lambda b: (b, 0), nb * m, 0)
    return call(x, wk, wv, pek, pev)


def _compress_pages(pool, page_table, wk, wv, pek, pev):
    nb, n_pages = page_table.shape
    mp = PAGE_SIZE // CMP_STRIDE
    x = pool.reshape(pool.shape[0], mp, CMP_STRIDE * 2 * LANES)
    specs = [pl.BlockSpec((None, mp, CMP_STRIDE * 2 * LANES), functools.partial(lambda pg, b, pt: (pt[b, pg], 0, 0), pg))
             for pg in range(n_pages)]
    m = n_pages * mp
    call = _compress_call(n_pages, m, (nb,), specs, lambda b, pt: (b, 0), nb * m, 1)
    return call(page_table, *([x] * n_pages), wk, wv, pek, pev)


def _topk_mask(imp, qpos, n_slc):
    blk = lax.broadcasted_iota(jnp.int32, imp.shape, 1)
    cur = qpos // SLC_BLOCK
    val = jnp.where((blk == cur) | (blk == 0), BIG, jnp.where(blk <= cur, imp, NEG))
    val = jnp.where(blk < n_slc, val, REMOVED)
    sel = jnp.zeros(imp.shape, jnp.bool_)
    for _ in range(min(SLC_TOPK, n_slc)):
        mx = jnp.max(val, axis=-1, keepdims=True)
        idx = jnp.min(jnp.where(val == mx, blk, imp.shape[1]), axis=-1, keepdims=True)
        pick = blk == idx
        sel = sel | pick
        val = jnp.where(pick, REMOVED, val)
    return sel.astype(BF16)


def _block_expand(n_blocks_pad, k0, n_keys):
    blk = lax.broadcasted_iota(jnp.int32, (n_blocks_pad, n_keys), 0)
    col = lax.broadcasted_iota(jnp.int32, (n_blocks_pad, n_keys), 1)
    return (blk == (k0 + col) // SLC_BLOCK).astype(BF16)


def _flash_init(m_rows):
    return (jnp.full((m_rows, 1), NEG, F32), jnp.zeros((m_rows, 1), F32), jnp.zeros((m_rows, HEAD_DIM), F32))


def _nsa_prompt_kernel(qc_ref, qr_ref, gates_ref, kc_ref, vc_ref, ks_ref, kw_ref, mcs_ref, o_ref,
                       *, qb, seq, key_tile):
    t0 = pl.program_id(1) * qb
    qpos = t0 + lax.broadcasted_iota(jnp.int32, (qb, 1), 0)
    n_cmp_pad = kc_ref.shape[0]
    n_slc = seq // SLC_BLOCK
    hd = HEAD_DIM

    ncol = lax.broadcasted_iota(jnp.int32, (qb, n_cmp_pad), 1)
    cmask = (ncol * CMP_STRIDE + CMP_LEN - 1) <= qpos
    o_cmp, sels = [], []
    for g in range(NSA_KV_HEADS):
        kcg = kc_ref[:, g * hd:(g + 1) * hd]
        vcg = vc_ref[:, g * hd:(g + 1) * hd]
        psum = jnp.zeros((qb, n_cmp_pad), F32)
        for j in range(NSA_GROUP):
            h = g * NSA_GROUP + j
            p, l = _softmax_full(_dot_nt(qc_ref[:, h * hd:(h + 1) * hd], kcg), cmask)
            p = p * (1.0 / jnp.maximum(l, TINY))
            o_cmp.append(_dot(p.astype(BF16), vcg))
            psum = psum + p
        imp = jnp.dot(psum, mcs_ref[...], precision=HIGHEST, preferred_element_type=F32)
        sels.append(_topk_mask(imp, qpos, n_slc))

    n_tiles = (t0 + qb + key_tile - 1) // key_tile

    def tile(kt, states):
        k0 = pl.multiple_of(kt * key_tile, key_tile)
        kv = ks_ref[pl.ds(k0, key_tile), :]
        causal = (k0 + lax.broadcasted_iota(jnp.int32, (qb, key_tile), 1)) <= qpos
        expand = _block_expand(sels[0].shape[1], k0, key_tile)
        out = []
        for g in range(NSA_KV_HEADS):
            bias = jnp.where((_dot(sels[g], expand) > 0.5) & causal, 0.0, NEG)
            kg = kv[:, g * hd:(g + 1) * hd]
            vg = kv[:, LANES + g * hd:LANES + (g + 1) * hd]
            for j in range(NSA_GROUP):
                h = g * NSA_GROUP + j
                out.append(_flash_step_bias(states[h], _dot_nt(qr_ref[:, h * hd:(h + 1) * hd], kg) + bias, vg))
        return tuple(out)
    states = lax.fori_loop(0, n_tiles, tile, tuple(_flash_init(qb) for _ in range(NSA_HEADS)))

    wlen = min(seq, NSA_WINDOW + qb)
    start = pl.multiple_of(jnp.clip(t0 - NSA_WINDOW, 0, seq - wlen), qb)
    band = kw_ref[pl.ds(start, wlen), :]
    kpos = start + lax.broadcasted_iota(jnp.int32, (qb, wlen), 1)
    wmask = (kpos <= qpos) & (kpos > qpos - NSA_WINDOW)
    for h in range(NSA_HEADS):
        g = h // NSA_GROUP
        p, l = _softmax_full(_dot_nt(qr_ref[:, h * hd:(h + 1) * hd], band[:, g * hd:(g + 1) * hd]), wmask)
        o_win = _dot(p.astype(BF16), band[:, LANES + g * hd:LANES + (g + 1) * hd]) / jnp.maximum(l, TINY)
        _, l_s, acc_s = states[h]
        o_slc = acc_s / jnp.maximum(l_s, TINY)
        gate = gates_ref[:, 3 * h:3 * h + 3]
        o = gate[:, 0:1] * o_cmp[h] + gate[:, 1:2] * o_slc + gate[:, 2:3] * o_win
        o_ref[:, h * hd:(h + 1) * hd] = o.astype(o_ref.dtype)


def _nsa_prompt(qc, qr, gates, kc, vc, ks, kw, mcs, *, nb, seq, qb):
    nq = seq // qb
    n_cmp_pad = seq // CMP_STRIDE
    key_tile = min(seq, 512)
    kern = functools.partial(_nsa_prompt_kernel, qb=qb, seq=seq, key_tile=key_tile)
    qspec = lambda wd: pl.BlockSpec((qb, wd), lambda b, i: (b * nq + i, 0))
    seqspec = lambda rows, wd: pl.BlockSpec((rows, wd), lambda b, i: (b, 0))
    return pl.pallas_call(
        kern,
        grid=(nb, nq),
        in_specs=[qspec(512), qspec(512), qspec(LANES), seqspec(n_cmp_pad, LANES), seqspec(n_cmp_pad, LANES),
                  seqspec(seq, 256), seqspec(seq, 256), _const_spec(mcs.shape)],
        out_specs=qspec(512),
        out_shape=jax.ShapeDtypeStruct((nb * seq, 512), BF16),
        compiler_params=_cparams("parallel", "arbitrary"),
        name="nsa_prompt",
    )(qc, qr, gates, kc, vc, ks, kw, mcs)


def _nsa_sample_kernel(*refs, n_pages, tail, win):
    pt_ref = refs[0]
    del pt_ref
    (qc_ref, qr_ref, gates_ref, kc_ref, vc_ref) = refs[1:6]
    page_refs = refs[6:6 + n_pages]
    news_ref, winbuf_ref, neww_ref, mcs_ref, amat_ref, o_ref = refs[6 + n_pages:]
    hd = HEAD_DIM
    past = n_pages * PAGE_SIZE
    m_rows = qc_ref.shape[1]
    n_tok = m_rows // NSA_GROUP
    tok = lax.broadcasted_iota(jnp.int32, (m_rows, 1), 0) % n_tok
    qpos = past + tok
    n_cmp_pad = kc_ref.shape[0]
    n_slc = (past + n_tok + SLC_BLOCK - 1) // SLC_BLOCK

    ncol = lax.broadcasted_iota(jnp.int32, (m_rows, n_cmp_pad), 1)
    cmask = ((ncol * CMP_STRIDE + CMP_LEN - 1) <= qpos) & (ncol < n_cmp_pad - 1)
    ks_all = jnp.concatenate([r[...].astype(BF16) for r in page_refs], axis=0)
    news = news_ref[...].astype(BF16)
    kw_all = winbuf_ref[...].astype(BF16)
    neww = neww_ref[...].astype(BF16)
    tail_pos = past + lax.broadcasted_iota(jnp.int32, (m_rows, tail), 1)
    tail_ok = (tail_pos <= qpos) & (tail_pos < past + n_tok)
    expand = _block_expand(LANES, 0, past)
    expand_tail = _block_expand(LANES, past, tail)
    wpos = past - win + lax.broadcasted_iota(jnp.int32, (m_rows, win), 1)
    wmask = (wpos <= qpos) & (wpos > qpos - NSA_WINDOW) & (wpos >= 0)
    wmask_tail = tail_ok & (tail_pos > qpos - NSA_WINDOW)

    for g in range(NSA_KV_HEADS):
        kslice = slice(g * hd, (g + 1) * hd)
        vslice = slice(LANES + g * hd, LANES + (g + 1) * hd)
        qc = qc_ref[g]
        qr = qr_ref[g]
        p, l = _softmax_full(_dot_nt(qc, kc_ref[:, kslice]), cmask)
        p = p / jnp.maximum(l, TINY)
        o_cmp = _dot(p.astype(BF16), vc_ref[:, kslice])
        psum = jnp.dot(amat_ref[...], p, precision=HIGHEST, preferred_element_type=F32)
        imp = jnp.dot(psum, mcs_ref[...], precision=HIGHEST, preferred_element_type=F32)
        sel = _topk_mask(imp, qpos, n_slc)

        st = _flash_init(m_rows)
        st = _flash_step(st, _dot_nt(qr, ks_all[:, kslice]), _dot(sel, expand) > 0.5, ks_all[:, vslice])
        st = _flash_step(st, _dot_nt(qr, news[:, kslice]), (_dot(sel, expand_tail) > 0.5) & tail_ok, news[:, vslice])
        o_slc = st[2] / jnp.maximum(st[1], TINY)

        st = _flash_init(m_rows)
        st = _flash_step(st, _dot_nt(qr, kw_all[:, kslice]), wmask, kw_all[:, vslice])
        st = _flash_step(st, _dot_nt(qr, neww[:, kslice]), wmask_tail, neww[:, vslice])
        o_win = st[2] / jnp.maximum(st[1], TINY)

        gate = gates_ref[g]
        o_ref[g] = gate[:, 0:1] * o_cmp + gate[:, 1:2] * o_slc + gate[:, 2:3] * o_win


def _nsa_sample(page_table, qc, qr, gates, kc, vc, pool_s, news, winbuf, neww, mcs, amat):
    nb, n_pages = page_table.shape
    m_rows = qc.shape[2]
    tail = news.shape[1]
    win = winbuf.shape[1]
    kern = functools.partial(_nsa_sample_kernel, n_pages=n_pages, tail=tail, win=win)
    per_b4 = lambda shp: pl.BlockSpec((None,) + shp, lambda b, pt: (b, 0, 0, 0))
    per_b3 = lambda shp: pl.BlockSpec((None,) + shp, lambda b, pt: (b, 0, 0))
    n_cmp_pad = kc.shape[0] // nb
    cmp_spec = pl.BlockSpec((n_cmp_pad, LANES), lambda b, pt: (b, 0))
    page_specs = [pl.BlockSpec((None, PAGE_SIZE, 256), functools.partial(lambda pg, b, pt: (pt[b, pg], 0, 0), pg))
                  for pg in range(n_pages)]
    gs = pltpu.PrefetchScalarGridSpec(
        num_scalar_prefetch=1, grid=(nb,),
        in_specs=[per_b4((NSA_KV_HEADS, m_rows, HEAD_DIM)), per_b4((NSA_KV_HEADS, m_rows, HEAD_DIM)),
                  per_b4((NSA_KV_HEADS, m_rows, LANES)), cmp_spec, cmp_spec] + page_specs +
                 [per_b3((tail, 256)), per_b3((win, 256)), per_b3((tail, 256)),
                  pl.BlockSpec(mcs.shape, lambda b, pt: (0, 0)), pl.BlockSpec(amat.shape, lambda b, pt: (0, 0))],
        out_specs=per_b4((NSA_KV_HEADS, m_rows, HEAD_DIM)))
    return pl.pallas_call(
        kern, grid_spec=gs,
        out_shape=jax.ShapeDtypeStruct((nb, NSA_KV_HEADS, m_rows, HEAD_DIM), F32),
        compiler_params=_cparams("parallel"),
        name="nsa_sample",
    )(page_table, qc, qr, gates, kc, vc, *([pool_s] * n_pages), news, winbuf, neww, mcs, amat)


C_GROUP_COLS = 3 * DIL_HEADS * HEAD_DIM


def _proj_c_kernel(x_ref, gmix_ref, w_ref, gq_ref, gk_ref, cs_ref, sn_ref, *out_refs):
    q_refs, r_refs = out_refs[:N_DIL], out_refs[N_DIL:]
    hn = _rms_rows(x_ref[...], gmix_ref[...]).astype(BF16)
    cs, sn = cs_ref[...], sn_ref[...]
    wd = DIL_HEADS * HEAD_DIM
    for g in range(N_DIL):
        z = _dot(hn, w_ref[:, g * C_GROUP_COLS:(g + 1) * C_GROUP_COLS])
        q = _rope(_rms_heads(z[:, 0:wd], gq_ref[g:g + 1, :]), cs, sn)
        q_refs[g][...] = (q * QSCALE).astype(BF16)
        r_refs[g][:, 0:wd] = _rope(_rms_heads(z[:, wd:2 * wd], gk_ref[g:g + 1, :]), cs, sn)
        r_refs[g][:, wd:] = z[:, 2 * wd:]


def _proj_c(x, gmix, w, gq, gk, cs, sn, tm):
    n = x.shape[0]
    tiles_per_seq = cs.shape[0] // tm
    row = lambda width: pl.BlockSpec((tm, width), lambda i: (i, 0))
    tab = pl.BlockSpec((tm, LANES), lambda i: (i % tiles_per_seq, 0))
    wd = DIL_HEADS * HEAD_DIM
    return pl.pallas_call(
        _proj_c_kernel,
        grid=(n // tm,),
        in_specs=[row(D_MODEL), _const_spec(gmix.shape), _const_spec(w.shape), _const_spec(gq.shape),
                  _const_spec(gk.shape), tab, tab],
        out_specs=[row(wd)] * N_DIL + [row(2 * wd)] * N_DIL,
        out_shape=[jax.ShapeDtypeStruct((n, wd), BF16)] * N_DIL + [jax.ShapeDtypeStruct((n, 2 * wd), F32)] * N_DIL,
        compiler_params=_cparams("parallel"),
        name="proj_c",
    )(x, gmix, w, gq, gk, cs, sn)


def _dil_prompt_kernel(q_ref, kp_ref, vp_ref, kc_ref, vc_ref, o_ref, ml_ref, *, tq):
    i = pl.program_id(2)
    row = lax.broadcasted_iota(jnp.int32, (tq, 2 * tq), 0)
    col = lax.broadcasted_iota(jnp.int32, (tq, 2 * tq), 1)
    delta = tq + row - col
    mask = (delta >= 0) & (delta < DIL_KEYS) & ((col >= tq) | (i > 0))
    k = jnp.concatenate([kp_ref[...], kc_ref[...]], axis=0).astype(BF16)
    v = jnp.concatenate([vp_ref[...], vc_ref[...]], axis=0).astype(BF16)
    lane = lax.broadcasted_iota(jnp.int32, (tq, LANES), 1)
    ml = jnp.zeros((tq, LANES), F32)
    hd = HEAD_DIM
    for h in range(DIL_HEADS):
        hs = slice(h * hd, (h + 1) * hd)
        s = jnp.where(mask, _dot_nt(q_ref[:, hs], k[:, hs]), NEG)
        m = jnp.max(s, axis=-1, keepdims=True)
        p = jnp.where(mask, jnp.exp(s - m), 0.0)
        l = jnp.sum(p, axis=-1, keepdims=True)
        o_ref[:, hs] = _dot(p.astype(BF16), v[:, hs])
        ml = jnp.where(lane == h, m, jnp.where(lane == DIL_HEADS + h, l, ml))
    ml_ref[...] = ml


def _dil_prompt(q, rows, *, nb, seq, dil, tq):
    wd = DIL_HEADS * HEAD_DIM
    assert tq >= DIL_KEYS - 1 and seq % (dil * tq) == 0
    nu = seq // dil // tq
    qv = q.reshape(nb * seq // dil, dil * wd)
    rv = rows.reshape(nb * seq // dil, dil * 2 * wd)
    cur = lambda off: (lambda b, r, i: (b * nu + i, 2 * r + off))
    prev = lambda off: (lambda b, r, i: (b * nu + jnp.maximum(i - 1, 0), 2 * r + off))
    blk = lambda imap: pl.BlockSpec((tq, wd), imap)
    o, ml = pl.pallas_call(
        functools.partial(_dil_prompt_kernel, tq=tq),
        grid=(nb, dil, nu),
        in_specs=[blk(lambda b, r, i: (b * nu + i, r)), blk(prev(0)), blk(prev(1)), blk(cur(0)), blk(cur(1))],
        out_specs=[blk(lambda b, r, i: (b * nu + i, r)), pl.BlockSpec((tq, LANES), lambda b, r, i: (b * nu + i, r))],
        out_shape=[jax.ShapeDtypeStruct(qv.shape, F32), jax.ShapeDtypeStruct((qv.shape[0], dil * LANES), F32)],
        compiler_params=_cparams("parallel", "parallel", "arbitrary"),
        name="dil_prompt",
    )(qv, rv, rv, rv, rv)
    return o.reshape(nb * seq, wd), ml.reshape(nb * seq, LANES)


def _dil_sample_kernel(q0_ref, q1_ref, q2_ref, n0_ref, n1_ref, n2_ref, c0_ref, c1_ref, c2_ref, o_ref, ext_scr,
                       *, n_tok):
    wd = DIL_HEADS * HEAD_DIM
    w0 = c0_ref.shape[0]
    q_refs, n_refs, c_refs = (q0_ref, q1_ref, q2_ref), (n0_ref, n1_ref, n2_ref), (c0_ref, c1_ref, c2_ref)
    ext_scr[pl.ds(0, w0), :] = c0_ref[...]
    ext_scr[pl.ds(w0, 8), :] = n0_ref[...]
    o_ref[...] = jnp.zeros(o_ref.shape, o_ref.dtype)
    own_head = (lax.broadcasted_iota(jnp.int32, (DIL_HEADS, wd), 1) // HEAD_DIM
                == lax.broadcasted_iota(jnp.int32, (DIL_HEADS, wd), 0))
    for t in range(n_tok):
        parts = []
        for g in range(N_DIL):
            q_bd = jnp.where(own_head, jnp.broadcast_to(q_refs[g][t:t + 1, :].astype(F32), (DIL_HEADS, wd)), 0.0)
            if g == 0:
                kt = ext_scr[pl.ds(t, w0), 0:wd]
                vt = ext_scr[pl.ds(t, w0), wd:2 * wd]
            else:
                kt = c_refs[g][:, t * 2 * wd:t * 2 * wd + wd]
                vt = c_refs[g][:, t * 2 * wd + wd:(t + 1) * 2 * wd]
            k0 = n_refs[g][t:t + 1, 0:wd]
            v0 = n_refs[g][t:t + 1, wd:2 * wd]
            s = _dot_nt(q_bd.astype(BF16), kt.astype(BF16))
            s0 = jnp.sum(q_bd * k0, axis=-1, keepdims=True)
            m = jnp.maximum(jnp.max(s, axis=-1, keepdims=True), s0)
            p = jnp.exp(s - m)
            p0 = jnp.exp(s0 - m)
            l = jnp.sum(p, axis=-1, keepdims=True) + p0
            o = _dot(p.astype(BF16), vt.astype(BF16)) + p0 * v0
            parts.append((o, m, l))
        mx = functools.reduce(jnp.maximum, [m for _, m, _ in parts])
        num = sum(jnp.exp(m - mx) * o for o, m, _ in parts)
        den = sum(jnp.exp(m - mx) * l for _, m, l in parts)
        o_ref[pl.ds(t, 1), :] = jnp.sum(jnp.where(own_head, num / den, 0.0), axis=0, keepdims=True)


def _dil_sample(qs, news, caches, *, n_tok):
    nb = qs[0].shape[0]
    wd = DIL_HEADS * HEAD_DIM
    views, cspecs = [], []
    for g, (w, d) in enumerate(DIL_PAIRS):
        assert caches[g].shape[1] == w and w // d == DIL_KEYS - 1
        assert g == 0 or n_tok <= d
        cv = caches[g].reshape(nb, w // d, d * 2 * wd)
        views.append(cv)
        lanes = min(d, n_tok) * 2 * wd if g else 2 * wd
        cspecs.append(pl.BlockSpec((None, w // d, lanes), lambda b: (b, 0, 0)))
    per_b = lambda wdt: pl.BlockSpec((None, 8, wdt), lambda b: (b, 0, 0))
    return pl.pallas_call(
        functools.partial(_dil_sample_kernel, n_tok=n_tok),
        grid=(nb,),
        in_specs=[per_b(wd)] * N_DIL + [per_b(2 * wd)] * N_DIL + cspecs,
        out_specs=per_b(wd),
        out_shape=jax.ShapeDtypeStruct((nb, 8, wd), F32),
        scratch_shapes=[pltpu.VMEM((DIL_PAIRS[0][0] + 8, 2 * wd), F32)],
        compiler_params=_cparams("parallel"),
        name="dil_sample",
    )(*qs, *news, *views)


FF_CHUNK = 1024


def _post_kernel(*refs, n_cat, combine):
    h_ref = refs[0]
    mix_refs = refs[1:1 + (2 * N_DIL if combine else n_cat)]
    (wo_ref, gmlp_ref, w1_ref, w2_ref, gple_ref, wg_ref, p_ref, wp_ref, out_ref) = refs[1 + len(mix_refs):1 + len(mix_refs) + 9]
    hd = HEAD_DIM
    if combine:
        comb_scr = refs[-1]
        o_refs, ml_refs = mix_refs[:N_DIL], mix_refs[N_DIL:]
        for h in range(DIL_HEADS):
            ms = [r[:, h:h + 1] for r in ml_refs]
            ls = [r[:, DIL_HEADS + h:DIL_HEADS + h + 1] for r in ml_refs]
            mx = functools.reduce(jnp.maximum, ms)
            ws = [jnp.exp(m - mx) for m in ms]
            num = sum(w * r[:, h * hd:(h + 1) * hd] for w, r in zip(ws, o_refs))
            den = sum(w * l for w, l in zip(ws, ls))
            comb_scr[:, h * hd:(h + 1) * hd] = (num / den).astype(BF16)
        y = _dot(comb_scr[...], wo_ref[...])
    else:
        y = None
        for k, r in enumerate(mix_refs):
            part = _dot(r[...].astype(BF16), wo_ref[k * r.shape[1]:(k + 1) * r.shape[1], :])
            y = part if y is None else y + part
    h1 = h_ref[...] + y
    hn = _rms_rows(h1, gmlp_ref[...]).astype(BF16)
    acc = jnp.zeros(h1.shape, F32)
    for c in range(D_FF // FF_CHUNK):
        u = _dot(hn, w1_ref[:, c * FF_CHUNK:(c + 1) * FF_CHUNK])
        acc = acc + _dot(jnp.square(jnp.maximum(u, 0.0)).astype(BF16), w2_ref[c * FF_CHUNK:(c + 1) * FF_CHUNK, :])
    h2 = h1 + acc
    gate = jax.nn.sigmoid(_dot(_rms_rows(h2, gple_ref[...]).astype(BF16), wg_ref[...]))
    out_ref[...] = h2 + gate * _dot(p_ref[...].astype(BF16), wp_ref[...])


def _post(h, mix, wo, gmlp, w1, w2, gple, wg, p, wp, *, combine, tm):
    n = h.shape[0]
    row = lambda a: pl.BlockSpec((tm, a.shape[1]), lambda i: (i, 0))
    weights = (wo, gmlp, w1, w2, gple, wg)
    wspec = lambda a: pl.BlockSpec(a.shape, lambda i: (0, 0), pipeline_mode=pl.Buffered(1))
    kern = functools.partial(_post_kernel, n_cat=len(mix), combine=combine)
    return pl.pallas_call(
        kern,
        grid=(n // tm,),
        in_specs=[row(h)] + [row(a) for a in mix] + [wspec(a) for a in weights] + [row(p), wspec(wp)],
        out_specs=row(h),
        out_shape=jax.ShapeDtypeStruct(h.shape, F32),
        scratch_shapes=[pltpu.VMEM((tm, DIL_HEADS * HEAD_DIM), BF16)] if combine else [],
        compiler_params=_cparams("parallel"),
        name="post",
    )(h, *mix, *weights, p, wp)


def _rope_tables(pos):
    half = HEAD_DIM // 2
    freq = ROPE_THETA ** (-jnp.arange(half, dtype=F32) / half)
    ang = pos.astype(F32)[:, None] * freq[None, :]
    cos, sin = jnp.cos(ang), jnp.sin(ang)
    return jnp.tile(jnp.concatenate([cos, cos], axis=1), (1, 2)), jnp.tile(jnp.concatenate([-sin, sin], axis=1), (1, 2))


def _cmp_to_slc(n_cmp, n_slc, rows_pad, cols_pad):
    cs = jnp.arange(n_cmp)[:, None] * CMP_STRIDE
    ss = jnp.arange(n_slc)[None, :] * SLC_BLOCK
    ov = jnp.maximum(jnp.minimum(cs + CMP_LEN, ss + SLC_BLOCK) - jnp.maximum(cs, ss), 0)
    return jnp.pad(ov.astype(F32) / CMP_LEN, ((0, rows_pad - n_cmp), (0, cols_pad - n_slc)))


def _round_up(x, m):
    return (x + m - 1) // m * m


def _block_diag2(w):
    z = jnp.zeros_like(w)
    return jnp.concatenate([jnp.concatenate([w, z], axis=2), jnp.concatenate([z, w], axis=2)], axis=1)


def kernel(x_prompt, x_sample, state_gla, cache_nsa_cmp, cache_nsa_slc, cache_nsa_win, cache_dil_0, cache_dil_1,
           cache_dil_2, page_table, p_prompt, p_sample, norm_mix, norm_mlp, norm_ple, a_w_in, a_w_out, gla_w_a2,
           gla_b_a, gla_g_norm, nsa_g_qk, nsa_w_phi, nsa_pe, c_w_in, c_g_qk, c_w_out, mlp_w1, mlp_w2, ple_w_proj,
           ple_w_gate):
    nb_p, seq, _ = x_prompt.shape
    nb_s, n_tok, _ = x_sample.shape
    depth = norm_mix.shape[0]
    past = page_table.shape[1] * PAGE_SIZE
    assert n_tok < CMP_STRIDE and n_tok <= 8 and seq % 512 == 0
    tm_p = 256
    tm_s = min(256, nb_s * n_tok)
    cs_p, sn_p = _rope_tables(jnp.arange(seq))
    cs_s, sn_s = _rope_tables(past + jnp.arange(nb_s * n_tok) % n_tok)
    hp = x_prompt.reshape(nb_p * seq, D_MODEL)
    hs = x_sample.reshape(nb_s * n_tok, D_MODEL)
    dil_caches = (cache_dil_0, cache_dil_1, cache_dil_2)
    outs = {k: [] for k in ("gla_p", "gla_s", "cmp_p", "cmp_s", "slc_p", "slc_s", "win_p", "win_s")}
    dil_p = [[] for _ in DIL_PAIRS]
    dil_s = [[] for _ in DIL_PAIRS]
    kvrow = (2, NSA_KV_HEADS, HEAD_DIM)
    wd = DIL_HEADS * HEAD_DIM

    for i in range(depth):
        j = i // 2
        gmix = norm_mix[i][None, :]
        if i % 2 == 0:
            w_in = a_w_in[j]
            pad_cols = lambda a, width: jnp.pad(a, ((0, 0), (0, width - a.shape[1])))
            w_pad = jnp.concatenate([w_in[:, :1536], pad_cols(w_in[:, 1536:1552], LANES), w_in[:, 1552:2832],
                                     pad_cols(w_in[:, 2832:], LANES)], axis=1).astype(BF16)
            wa2 = jnp.pad(gla_w_a2[j], ((0, LANES - GLA_LOWRANK), (0, 0))).astype(BF16)
            ba = gla_b_a[j][None, :]
            gqk = jnp.tile(nsa_g_qk[j], (1, NSA_HEADS))
            gnorm = gla_g_norm[j][None, :]
            wk, wv = (_block_diag2(nsa_w_phi[j, c]).astype(BF16) for c in range(2))
            pek, pev = (jnp.tile(nsa_pe[j, c], (1, 2)) for c in range(2))
            wo = a_w_out[j].astype(BF16)

            gla, qc, qr, gates, rc, rs, rw, sbf, wbf = _proj_a(hp, gmix, w_pad, wa2, ba, gqk, cs_p, sn_p, tm_p)
            s0 = jnp.zeros((nb_p, GLA_HEADS, GLA_DK, GLA_DV), F32)
            chunk = min(GLA_CHUNK, seq)
            o_gla, s_p = _gla(gla, s0, gnorm, nb=nb_p, rows_per_seq=seq, chunk=chunk, n_valid=chunk, out_dtype=BF16)
            kc, vc = _compress_prompt(rc, nb_p, seq, wk, wv, pek, pev)
            n_cmp_pad = seq // CMP_STRIDE
            mcs = _cmp_to_slc(n_cmp_pad - 1, seq // SLC_BLOCK, n_cmp_pad, _round_up(seq // SLC_BLOCK, LANES))
            o_nsa = _nsa_prompt(qc, qr, gates, kc, vc, sbf, wbf, mcs, nb=nb_p, seq=seq, qb=128)
            mix_p = (o_gla, o_nsa)
            outs["gla_p"].append(s_p)
            outs["cmp_p"].append(rc.reshape((nb_p, seq) + kvrow))
            outs["slc_p"].append(rs.reshape((nb_p, seq) + kvrow))
            outs["win_p"].append(rw.reshape((nb_p, seq) + kvrow)[:, seq - min(NSA_WINDOW, seq):])

            gla, qc, qr, gates, rc, rs, rw, _, _ = _proj_a(hs, gmix, w_pad, wa2, ba, gqk, cs_s, sn_s, tm_s)
            gla8 = jnp.pad(gla.reshape(nb_s, n_tok, G_END), ((0, 0), (0, 8 - n_tok), (0, 0))).reshape(nb_s * 8, G_END)
            o_gla8, s_s = _gla(gla8, state_gla[j], gnorm, nb=nb_s, rows_per_seq=8, chunk=8, n_valid=n_tok, out_dtype=F32)
            o_gla = o_gla8.reshape(nb_s, 8, -1)[:, :n_tok].reshape(nb_s * n_tok, -1)
            kc, vc = _compress_pages(cache_nsa_cmp[j].reshape(-1, PAGE_SIZE, 256), page_table, wk, wv, pek, pev)
            n_cmp_pad = past // CMP_STRIDE
            n_slc = (past + n_tok + SLC_BLOCK - 1) // SLC_BLOCK
            mcs = _cmp_to_slc(n_cmp_pad - 1, n_slc, n_cmp_pad, LANES)
            m_rows = NSA_GROUP * n_tok
            regroup = lambda a: a.reshape(nb_s, n_tok, NSA_KV_HEADS, NSA_GROUP, -1).transpose(0, 2, 3, 1, 4).reshape(
                nb_s, NSA_KV_HEADS, m_rows, -1)
            g16 = jnp.pad(regroup(gates[:, :3 * NSA_HEADS]), ((0, 0), (0, 0), (0, 0), (0, LANES - 3)))
            tok_id = jnp.arange(m_rows) % n_tok
            amat = (tok_id[:, None] == tok_id[None, :]).astype(F32)
            pad_tail = lambda a: jnp.pad(a.reshape(nb_s, n_tok, 256), ((0, 0), (0, LANES - n_tok), (0, 0)))
            win_buf = cache_nsa_win[j].reshape(nb_s, -1, 256)
            o16 = _nsa_sample(page_table, regroup(qc), regroup(qr), g16, kc, vc,
                              cache_nsa_slc[j].reshape(-1, PAGE_SIZE, 256), pad_tail(rs), win_buf, pad_tail(rw),
                              mcs, amat)
            o_nsa = o16.reshape(nb_s, NSA_KV_HEADS, NSA_GROUP, n_tok, HEAD_DIM).transpose(0, 3, 1, 2, 4).reshape(
                nb_s * n_tok, NSA_HEADS * HEAD_DIM)
            mix_s = (o_gla, o_nsa)
            outs["gla_s"].append(s_s)
            outs["cmp_s"].append(rc.reshape((nb_s, n_tok) + kvrow))
            outs["slc_s"].append(rs.reshape((nb_s, n_tok) + kvrow))
            outs["win_s"].append(
                jnp.concatenate([cache_nsa_win[j][:, n_tok:], rw.reshape((nb_s, n_tok) + kvrow)], axis=1))
            combine = False
        else:
            w_c = c_w_in[j].astype(BF16)
            gq = jnp.tile(c_g_qk[j][:, 0], (1, DIL_HEADS))
            gk = jnp.tile(c_g_qk[j][:, 1], (1, DIL_HEADS))
            wo = c_w_out[j].astype(BF16)
            dilrow = (2, DIL_HEADS, HEAD_DIM)

            res = _proj_c(hp, gmix, w_c, gq, gk, cs_p, sn_p, tm_p)
            qs, rows = res[:N_DIL], res[N_DIL:]
            o_parts, ml_parts = [], []
            for g, (w, d) in enumerate(DIL_PAIRS):
                o_g, ml_g = _dil_prompt(qs[g], rows[g], nb=nb_p, seq=seq, dil=d, tq=128)
                o_parts.append(o_g)
                ml_parts.append(ml_g)
                dil_p[g].append(rows[g].reshape((nb_p, seq) + dilrow)[:, seq - min(w, seq):])
            mix_p = tuple(o_parts) + tuple(ml_parts)

            res = _proj_c(hs, gmix, w_c, gq, gk, cs_s, sn_s, tm_s)
            qs, rows = res[:N_DIL], res[N_DIL:]
            pad8 = lambda a: jnp.pad(a.reshape(nb_s, n_tok, -1), ((0, 0), (0, 8 - n_tok), (0, 0)))
            caches = [c[j].reshape(nb_s, -1, 2 * wd) for c in dil_caches]
            o8 = _dil_sample([pad8(q) for q in qs], [pad8(r) for r in rows], caches, n_tok=n_tok)
            mix_s = (o8[:, :n_tok].reshape(nb_s * n_tok, wd),)
            for g in range(N_DIL):
                dil_s[g].append(jnp.concatenate(
                    [dil_caches[g][j][:, n_tok:], rows[g].reshape((nb_s, n_tok) + dilrow)], axis=1))
            combine = True

        lw = (norm_mlp[i][None, :], mlp_w1[i].astype(BF16), mlp_w2[i].astype(BF16), norm_ple[i][None, :],
              ple_w_gate[i].astype(BF16))
        wp = ple_w_proj[i].astype(BF16)
        hp = _post(hp, mix_p, wo, *lw, p_prompt[i].reshape(nb_p * seq, PLE_DIM), wp, combine=combine, tm=tm_p)
        hs = _post(hs, mix_s, wo, *lw, p_sample[i].reshape(nb_s * n_tok, PLE_DIM), wp, combine=False, tm=tm_s)

    st = jnp.stack
    return (hp.reshape(x_prompt.shape), hs.reshape(x_sample.shape),
            st(outs["gla_p"]), st(outs["gla_s"]), st(outs["cmp_p"]), st(outs["cmp_s"]),
            st(outs["slc_p"]), st(outs["slc_s"]), st(outs["win_p"]), st(outs["win_s"]),
            st(dil_p[0]), st(dil_s[0]), st(dil_p[1]), st(dil_s[1]), st(dil_p[2]), st(dil_s[2]))
```

```python
import functools

import jax
import jax.numpy as jnp
from jax import lax
from jax.experimental import pallas as pl
from jax.experimental.pallas import tpu as pltpu

F32 = jnp.float32
BF16 = jnp.bfloat16
HIGHEST = lax.Precision.HIGHEST

D_MODEL = 1024
PAGE_SIZE = 128
HEAD_DIM = 64
GLA_HEADS = 4
GLA_DK = 64
GLA_DV = 128
GLA_LOWRANK = 16
GLA_TAU = 16.0
GLA_CHUNK = 64
GLA_SUB = 16
NSA_HEADS = 8
NSA_KV_HEADS = 2
NSA_GROUP = NSA_HEADS // NSA_KV_HEADS
CMP_LEN = 32
CMP_STRIDE = 16
SLC_BLOCK = 64
SLC_TOPK = 16
NSA_WINDOW = 512
DIL_PAIRS = ((128, 1), (512, 4), (2048, 16))
N_DIL = 3
DIL_HEADS = 8
DIL_KEYS = 129
D_FF = 4 * D_MODEL
PLE_DIM = 256
ROPE_THETA = 10000.0
EPS = 1e-6
NEG = -1e30
BIG = 1e30
TINY = 1e-20
REMOVED = -3e38

LANES = 128
TOK_PAD = 8
NSA_KEY_TILE = 512
VMEM_LIMIT = 56 * 1024 * 1024
QSCALE = HEAD_DIM ** -0.5

A_GQ, A_GK, A_GV, A_GR, A_GA, A_NQ, A_NKV, A_NG, A_END = 0, 256, 512, 1024, 1536, 1664, 2176, 2944, 3072
G_Q, G_K, G_V, G_R, G_LG, G_END = 0, 256, 512, 1024, 1536, 1792


def _cparams(*sem):
    return pltpu.CompilerParams(dimension_semantics=sem, vmem_limit_bytes=VMEM_LIMIT)


def _rms_rows(x, g):
    return x * lax.rsqrt(jnp.mean(x * x, axis=-1, keepdims=True) + EPS) * g


def _cols(x):
    return [x[:, c * LANES:(c + 1) * LANES] for c in range(x.shape[1] // LANES)]


def _rms_heads(x, g):
    out = []
    for c, xc in enumerate(_cols(x)):
        low = lax.broadcasted_iota(jnp.int32, xc.shape, 1) < HEAD_DIM
        sq = xc * xc
        s_lo = jnp.sum(jnp.where(low, sq, 0.0), axis=-1, keepdims=True)
        s_hi = jnp.sum(jnp.where(low, 0.0, sq), axis=-1, keepdims=True)
        ms = jnp.where(low, s_lo, s_hi) * (1.0 / HEAD_DIM)
        out.append(xc * lax.rsqrt(ms + EPS) * g[:, c * LANES:(c + 1) * LANES])
    return jnp.concatenate(out, axis=1) if len(out) > 1 else out[0]


def _rope(x, cs, sn):
    out = []
    for xc in _cols(x):
        lane = lax.broadcasted_iota(jnp.int32, xc.shape, 1)
        swapped = jnp.where((lane & 32) == 0, pltpu.roll(xc, LANES - 32, axis=1), pltpu.roll(xc, 32, axis=1))
        out.append(xc * cs + swapped * sn)
    return jnp.concatenate(out, axis=1) if len(out) > 1 else out[0]


def _dot(a, b):
    return jnp.dot(a, b, preferred_element_type=F32)


def _dot_nt(a, b):
    return lax.dot_general(a, b, (((1,), (1,)), ((), ())), preferred_element_type=F32)


def _dot_tn(a, b):
    return lax.dot_general(a, b, (((0,), (0,)), ((), ())), preferred_element_type=F32)


def _softmax_full(s, mask):
    s = jnp.where(mask, s, NEG)
    m = jnp.max(s, axis=-1, keepdims=True)
    p = jnp.where(mask, jnp.exp(s - m), 0.0)
    l = jnp.sum(p, axis=-1, keepdims=True)
    return p, l


def _flash_step(state, s, mask, v):
    m, l, acc = state
    s = jnp.where(mask, s, NEG)
    m_new = jnp.maximum(m, jnp.max(s, axis=-1, keepdims=True))
    alpha = jnp.exp(m - m_new)
    p = jnp.where(mask, jnp.exp(s - m_new), 0.0)
    l = alpha * l + jnp.sum(p, axis=-1, keepdims=True)
    acc = alpha * acc + _dot(p.astype(BF16), v)
    return m_new, l, acc


def _flash_step_bias(state, s, v):
    m, l, acc = state
    m_new = jnp.maximum(m, jnp.max(s, axis=-1, keepdims=True))
    alpha = jnp.exp(m - m_new)
    p = jnp.exp(s - m_new)
    l = alpha * l + jnp.sum(p, axis=-1, keepdims=True)
    acc = alpha * acc + _dot(p.astype(BF16), v)
    return m_new, l, acc


def _proj_a_kernel(x_ref, gmix_ref, w_ref, wa2_ref, ba_ref, gqk_ref, cs_ref, sn_ref,
                   gla_ref, qc_ref, qr_ref, gates_ref, rc_ref, rs_ref, rw_ref, sbf_ref, wbf_ref):
    hn = _rms_rows(x_ref[...], gmix_ref[...])
    z = _dot(hn.astype(BF16), w_ref[...])
    cs, sn = cs_ref[...], sn_ref[...]
    gla_ref[:, G_Q:G_K] = z[:, A_GQ:A_GK] * (GLA_DK ** -0.5)
    gla_ref[:, G_K:G_LG] = z[:, A_GK:A_GA]
    pre = _dot(z[:, A_GA:A_NQ].astype(BF16), wa2_ref[...]) + ba_ref[...]
    gla_ref[:, G_LG:G_END] = jax.nn.log_sigmoid(pre) / GLA_TAU
    qn = _rms_heads(z[:, A_NQ:A_NKV], gqk_ref[0:1, :])
    qc_ref[...] = (qn * QSCALE).astype(BF16)
    qr_ref[...] = (_rope(qn, cs, sn) * QSCALE).astype(BF16)
    gates_ref[...] = jax.nn.sigmoid(z[:, A_NG:A_END])
    kv = [z[:, A_NKV + i * LANES:A_NKV + (i + 1) * LANES] for i in range(6)]
    rc_ref[:, 0:LANES] = _rms_heads(kv[0], gqk_ref[1:2, 0:LANES])
    rc_ref[:, LANES:] = kv[1]
    ks = _rope(_rms_heads(kv[2], gqk_ref[2:3, 0:LANES]), cs, sn)
    rs_ref[:, 0:LANES] = ks
    rs_ref[:, LANES:] = kv[3]
    kw = _rope(_rms_heads(kv[4], gqk_ref[3:4, 0:LANES]), cs, sn)
    rw_ref[:, 0:LANES] = kw
    rw_ref[:, LANES:] = kv[5]
    sbf_ref[:, 0:LANES] = ks.astype(BF16)
    sbf_ref[:, LANES:] = kv[3].astype(BF16)
    wbf_ref[:, 0:LANES] = kw.astype(BF16)
    wbf_ref[:, LANES:] = kv[5].astype(BF16)


def _const_spec(shape):
    nd = len(shape)
    return pl.BlockSpec(shape, lambda *_: (0,) * nd)


def _proj_a(x, gmix, w, wa2, ba, gqk, cs, sn, tm):
    n = x.shape[0]
    tiles_per_seq = cs.shape[0] // tm
    row = lambda width: pl.BlockSpec((tm, width), lambda i: (i, 0))
    tab = pl.BlockSpec((tm, LANES), lambda i: (i % tiles_per_seq, 0))
    widths = (G_END, 512, 512, LANES, 256, 256, 256, 256, 256)
    dtypes = (F32, BF16, BF16, F32, F32, F32, F32, BF16, BF16)
    return pl.pallas_call(
        _proj_a_kernel,
        grid=(n // tm,),
        in_specs=[row(D_MODEL), _const_spec(gmix.shape), _const_spec(w.shape), _const_spec(wa2.shape),
                  _const_spec(ba.shape), _const_spec(gqk.shape), tab, tab],
        out_specs=[row(wd) for wd in widths],
        out_shape=[jax.ShapeDtypeStruct((n, wd), dt) for wd, dt in zip(widths, dtypes)],
        compiler_params=_cparams("parallel"),
        name="proj_a",
    )(x, gmix, w, wa2, ba, gqk, cs, sn)


def _gla_kernel(gla_ref, s0_ref, gnorm_ref, o_ref, sout_ref, s_scr, *, chunk, sub, n_chunks, n_valid):
    i = pl.program_id(1)
    n_pairs = GLA_HEADS // 2
    n_sub = chunk // sub
    dk2, dv2 = 2 * GLA_DK, 2 * GLA_DV

    @pl.when(i == 0)
    def _():
        s_scr[...] = jnp.zeros(s_scr.shape, F32)
        for h in range(GLA_HEADS):
            p, e = divmod(h, 2)
            s_scr[p, e * GLA_DK:(e + 1) * GLA_DK, e * GLA_DV:(e + 1) * GLA_DV] = s0_ref[h]

    tri = (lax.broadcasted_iota(jnp.int32, (chunk, chunk), 0)
           >= lax.broadcasted_iota(jnp.int32, (chunk, chunk), 1)).astype(F32)
    low = lax.broadcasted_iota(jnp.int32, (chunk, dk2), 1) < GLA_DK
    low_sub = lax.broadcasted_iota(jnp.int32, (sub, dk2), 1) < GLA_DK
    t_sub = lax.broadcasted_iota(jnp.int32, (sub, 1), 0)
    row_blk = lax.broadcasted_iota(jnp.int32, (chunk, chunk), 0) // sub
    col_idx = lax.broadcasted_iota(jnp.int32, (chunk, chunk), 1)
    diag_blocks = ((lax.broadcasted_iota(jnp.int32, (dk2, dv2), 0) < GLA_DK)
                   == (lax.broadcasted_iota(jnp.int32, (dk2, dv2), 1) < GLA_DV))
    eye = (lax.broadcasted_iota(jnp.int32, (dk2, dk2), 0) == lax.broadcasted_iota(jnp.int32, (dk2, dk2), 1))
    gnorm = gnorm_ref[...]

    def do_chunk(c, carry):
        rows = pl.ds(pl.multiple_of(c * chunk, chunk), chunk)
        b_all = jnp.dot(tri, gla_ref[rows, G_LG:G_END], precision=HIGHEST, preferred_element_type=F32)
        for p in range(n_pairs):
            q2 = gla_ref[rows, G_Q + p * dk2:G_Q + (p + 1) * dk2]
            k2 = gla_ref[rows, G_K + p * dk2:G_K + (p + 1) * dk2]
            v2 = gla_ref[rows, G_V + p * dv2:G_V + (p + 1) * dv2]
            b2 = b_all[:, p * dk2:(p + 1) * dk2]
            s_pair = s_scr[p]
            o2 = _dot((q2 * jnp.exp(b2)).astype(BF16), s_pair.astype(BF16))

            if n_sub > 1:
                bref_rows = jnp.concatenate(
                    [b2[0:sub]] + [jnp.broadcast_to(b2[i * sub - 1:i * sub], (sub, dk2)) for i in range(1, n_sub)], axis=0)
                qt = q2 * jnp.exp(jnp.minimum(b2 - bref_rows, 0.0))
                qt_even = jnp.where(low, qt, 0.0).astype(BF16)
                qt_odd = jnp.where(low, 0.0, qt).astype(BF16)
                a_even = jnp.zeros((chunk, chunk), F32)
                a_odd = jnp.zeros((chunk, chunk), F32)
                for i in range(1, n_sub):
                    kt = (k2 * jnp.exp(jnp.minimum(b2[i * sub - 1:i * sub] - b2, 0.0))).astype(BF16)
                    take = (row_blk == i) & (col_idx < i * sub)
                    a_even = jnp.where(take, _dot_nt(qt_even, kt), a_even)
                    a_odd = jnp.where(take, _dot_nt(qt_odd, kt), a_odd)
                o2 = o2 + jnp.concatenate([_dot(a_even.astype(BF16), v2[:, 0:GLA_DV].astype(BF16)),
                                           _dot(a_odd.astype(BF16), v2[:, GLA_DV:].astype(BF16))], axis=1)

            diag = []
            for i in range(n_sub):
                sl = slice(i * sub, (i + 1) * sub)
                qs, ks, bs, vs = q2[sl], k2[sl], b2[sl], v2[sl]
                od = jnp.zeros((sub, dv2), F32)
                for jj in range(max(0, min(sub, n_valid - i * sub))):
                    w = qs * ks[jj:jj + 1] * jnp.exp(jnp.minimum(bs - bs[jj:jj + 1], 0.0))
                    keep = t_sub >= jj
                    a_e = jnp.where(keep, jnp.sum(jnp.where(low_sub, w, 0.0), axis=-1, keepdims=True), 0.0)
                    a_o = jnp.where(keep, jnp.sum(jnp.where(low_sub, 0.0, w), axis=-1, keepdims=True), 0.0)
                    od = od + jnp.concatenate([a_e * vs[jj:jj + 1, 0:GLA_DV], a_o * vs[jj:jj + 1, GLA_DV:]], axis=1)
                diag.append(od)
            o2 = o2 + (jnp.concatenate(diag, axis=0) if n_sub > 1 else diag[0])

            b_last = b2[chunk - 1:chunk, :]
            kdec = k2 * jnp.exp(b_last - b2)
            upd = _dot_tn(kdec.astype(BF16), v2.astype(BF16))
            decay_col = jnp.sum(jnp.where(eye, jnp.exp(b_last), 0.0), axis=1, keepdims=True)
            s_scr[p] = decay_col * s_pair + jnp.where(diag_blocks, upd, 0.0)
            for e in range(2):
                h = 2 * p + e
                on = _rms_rows(o2[:, e * GLA_DV:(e + 1) * GLA_DV], gnorm)
                r = gla_ref[rows, G_R + h * GLA_DV:G_R + (h + 1) * GLA_DV]
                o_ref[rows, h * GLA_DV:(h + 1) * GLA_DV] = (on * (r * jax.nn.sigmoid(r))).astype(o_ref.dtype)
        return carry
    lax.fori_loop(0, n_chunks, do_chunk, 0)

    @pl.when(i == pl.num_programs(1) - 1)
    def _():
        for h in range(GLA_HEADS):
            p, e = divmod(h, 2)
            sout_ref[h] = s_scr[p, e * GLA_DK:(e + 1) * GLA_DK, e * GLA_DV:(e + 1) * GLA_DV]


def _gla(gla, s0, gnorm, *, nb, rows_per_seq, chunk, n_valid, out_dtype):
    step_rows = min(rows_per_seq, 8 * chunk)
    steps = rows_per_seq // step_rows
    kern = functools.partial(_gla_kernel, chunk=chunk, sub=min(chunk, GLA_SUB), n_chunks=step_rows // chunk,
                             n_valid=n_valid)
    return pl.pallas_call(
        kern,
        grid=(nb, steps),
        in_specs=[pl.BlockSpec((step_rows, G_END), lambda b, i: (b * steps + i, 0)),
                  pl.BlockSpec((None, GLA_HEADS, GLA_DK, GLA_DV), lambda b, i: (b, 0, 0, 0)),
                  _const_spec(gnorm.shape)],
        out_specs=[pl.BlockSpec((step_rows, GLA_HEADS * GLA_DV), lambda b, i: (b * steps + i, 0)),
                   pl.BlockSpec((None, GLA_HEADS, GLA_DK, GLA_DV), lambda b, i: (b, 0, 0, 0))],
        out_shape=[jax.ShapeDtypeStruct((nb * rows_per_seq, GLA_HEADS * GLA_DV), out_dtype),
                   jax.ShapeDtypeStruct((nb, GLA_HEADS, GLA_DK, GLA_DV), F32)],
        scratch_shapes=[pltpu.VMEM((GLA_HEADS // 2, 2 * GLA_DK, 2 * GLA_DV), F32)],
        compiler_params=_cparams("parallel", "arbitrary"),
        name="gla",
    )(gla, s0, gnorm)


def _compress_kernel(*refs, n_parts, n_prefetch):
    refs = refs[n_prefetch:]
    x_refs = refs[:n_parts]
    wk_ref, wv_ref, pek_ref, pev_ref, kc_ref, vc_ref, sh_scr = refs[n_parts:]
    x = jnp.concatenate([r[...] for r in x_refs], axis=0) if n_parts > 1 else x_refs[0][...]
    m = x.shape[0]
    row_w = 2 * LANES

    def branch(off, w_ref, pe_ref, out_ref):
        lo = jnp.zeros((m, LANES), F32)
        hi = jnp.zeros((m, LANES), F32)
        for p in range(CMP_STRIDE):
            xp = x[:, p * row_w + off:p * row_w + off + LANES]
            lo = lo + _dot((xp + pe_ref[p:p + 1, :]).astype(BF16), w_ref[p])
            hi = hi + _dot((xp + pe_ref[CMP_STRIDE + p:CMP_STRIDE + p + 1, :]).astype(BF16), w_ref[CMP_STRIDE + p])
        sh_scr[pl.ds(0, m), :] = hi
        sh_scr[pl.ds(m, 8), :] = jnp.zeros((8, LANES), F32)
        out_ref[...] = (lo + sh_scr[pl.ds(1, m), :]).astype(out_ref.dtype)
    branch(0, wk_ref, pek_ref, kc_ref)
    branch(LANES, wv_ref, pev_ref, vc_ref)


def _compress_call(n_parts, m, grid, x_specs, out_map, out_rows, prefetch):
    kern = functools.partial(_compress_kernel, n_parts=n_parts, n_prefetch=prefetch)
    w_shape = (CMP_LEN, LANES, LANES)
    pe_shape = (CMP_LEN, LANES)
    in_specs = list(x_specs) + [_const_spec(w_shape), _const_spec(w_shape), _const_spec(pe_shape), _const_spec(pe_shape)]
    out_specs = [pl.BlockSpec((m, LANES), out_map), pl.BlockSpec((m, LANES), out_map)]
    gs = pltpu.PrefetchScalarGridSpec(num_scalar_prefetch=prefetch, grid=grid, in_specs=in_specs, out_specs=out_specs,
                                      scratch_shapes=[pltpu.VMEM((m + 8, LANES), F32)])
    return pl.pallas_call(
        kern, grid_spec=gs,
        out_shape=[jax.ShapeDtypeStruct((out_rows, LANES), BF16)] * 2,
        compiler_params=_cparams("parallel"),
        name="nsa_compress",
    )


def _compress_prompt(rows_c, nb, seq, wk, wv, pek, pev):
    m = seq // CMP_STRIDE
    x = rows_c.reshape(-1, CMP_STRIDE * 2 * LANES)
    call = _compress_call(1, m, (nb,), [pl.BlockSpec((m, CMP_STRIDE * 2 * LANES), lambda b: (b, 0))],
                          lambda b: (b, 0), nb * m, 0)
    return call(x, wk, wv, pek, pev)


def _compress_pages(pool, page_table, wk, wv, pek, pev):
    nb, n_pages = page_table.shape
    mp = PAGE_SIZE // CMP_STRIDE
    x = pool.reshape(pool.shape[0], mp, CMP_STRIDE * 2 * LANES)
    specs = [pl.BlockSpec((None, mp, CMP_STRIDE * 2 * LANES), functools.partial(lambda pg, b, pt: (pt[b, pg], 0, 0), pg))
             for pg in range(n_pages)]
    m = n_pages * mp
    call = _compress_call(n_pages, m, (nb,), specs, lambda b, pt: (b, 0), nb * m, 1)
    return call(page_table, *([x] * n_pages), wk, wv, pek, pev)


def _topk_mask(imp, qpos, n_slc):
    blk = lax.broadcasted_iota(jnp.int32, imp.shape, 1)
    cur = qpos // SLC_BLOCK
    val = jnp.where((blk == cur) | (blk == 0), BIG, jnp.where(blk <= cur, imp, NEG))
    val = jnp.where(blk < n_slc, val, REMOVED)
    sel = jnp.zeros(imp.shape, jnp.bool_)
    for _ in range(min(SLC_TOPK, n_slc)):
        mx = jnp.max(val, axis=-1, keepdims=True)
        idx = jnp.min(jnp.where(val == mx, blk, imp.shape[1]), axis=-1, keepdims=True)
        pick = blk == idx
        sel = sel | pick
        val = jnp.where(pick, REMOVED, val)
    return sel.astype(BF16)


def _block_expand(n_blocks_pad, k0, n_keys):
    blk = lax.broadcasted_iota(jnp.int32, (n_blocks_pad, n_keys), 0)
    col = lax.broadcasted_iota(jnp.int32, (n_blocks_pad, n_keys), 1)
    return (blk == (k0 + col) // SLC_BLOCK).astype(BF16)


def _flash_init(m_rows):
    return (jnp.full((m_rows, 1), NEG, F32), jnp.zeros((m_rows, 1), F32), jnp.zeros((m_rows, HEAD_DIM), F32))


def _nsa_prompt_kernel(qc_ref, qr_ref, gates_ref, kc_ref, vc_ref, ks_ref, kw_ref, mcs_ref, o_ref,
                       qa_scr, m_scr, l_scr, acc_scr, *, qb, seq, key_tile):
    t0 = pl.program_id(1) * qb
    qpos = t0 + lax.broadcasted_iota(jnp.int32, (qb, 1), 0)
    n_cmp_pad = kc_ref.shape[0]
    n_slc = seq // SLC_BLOCK
    hd = HEAD_DIM

    ncol = lax.broadcasted_iota(jnp.int32, (qb, n_cmp_pad), 1)
    cmask = (ncol * CMP_STRIDE + CMP_LEN - 1) <= qpos
    o_cmp, sels = [], []
    for g in range(NSA_KV_HEADS):
        kcg = kc_ref[:, g * hd:(g + 1) * hd]
        vcg = vc_ref[:, g * hd:(g + 1) * hd]
        psum = jnp.zeros((qb, n_cmp_pad), F32)
        for j in range(NSA_GROUP):
            h = g * NSA_GROUP + j
            p, l = _softmax_full(_dot_nt(qc_ref[:, h * hd:(h + 1) * hd], kcg), cmask)
            p = p * (1.0 / jnp.maximum(l, TINY))
            o_cmp.append(_dot(p.astype(BF16), vcg))
            psum = psum + p
        imp = jnp.dot(psum, mcs_ref[...], precision=HIGHEST, preferred_element_type=F32)
        sels.append(_topk_mask(imp, qpos, n_slc))

    n_blk = sels[0].shape[1]
    lane2 = lax.broadcasted_iota(jnp.int32, (qb, 2 * hd), 1)
    for h in range(NSA_HEADS):
        g = h // NSA_GROUP
        pair = qr_ref[:, (h // 2) * 2 * hd:(h // 2 + 1) * 2 * hd].astype(F32)
        if h % 2 != g:
            pair = pltpu.roll(pair, hd, axis=1)
        q_part = jnp.where((lane2 >= g * hd) & (lane2 < (g + 1) * hd), pair, 0.0).astype(BF16)
        qa_scr[h * qb:(h + 1) * qb, :] = jnp.concatenate([q_part, 1.0 - sels[g]], axis=1)
    rows = NSA_HEADS * qb
    m_scr[...] = jnp.full((rows, LANES), NEG, F32)
    l_scr[...] = jnp.zeros((rows, LANES), F32)
    acc_scr[...] = jnp.zeros((rows, 2 * hd), F32)

    def tile(k0, causal):
        kv = ks_ref[pl.ds(k0, key_tile), :]
        key_blk = (k0 + lax.broadcasted_iota(jnp.int32, (key_tile, n_blk), 0)) // SLC_BLOCK
        own_blk = key_blk == lax.broadcasted_iota(jnp.int32, (key_tile, n_blk), 1)
        k_aug = jnp.concatenate([kv[:, 0:2 * hd], jnp.where(own_blk, NEG, 0.0).astype(BF16)], axis=1)
        s = _dot_nt(qa_scr[...], k_aug)
        if causal:
            q_row = t0 + lax.broadcasted_iota(jnp.int32, (rows, key_tile), 0) % qb
            s = jnp.where(k0 + lax.broadcasted_iota(jnp.int32, (rows, key_tile), 1) <= q_row, s, NEG)
        m_old = m_scr[...]
        m_new = jnp.maximum(m_old, jnp.max(s, axis=-1, keepdims=True))
        alpha = jnp.exp(m_old - m_new)
        p = jnp.exp(s - jnp.concatenate([m_new] * (key_tile // LANES), axis=1))
        l_scr[...] = alpha * l_scr[...] + jnp.sum(p, axis=-1, keepdims=True)
        acc_scr[...] = alpha * acc_scr[...] + _dot(p.astype(BF16), kv[:, 2 * hd:4 * hd])
        m_scr[...] = m_new

    n_full = t0 // key_tile

    def full_tile(kt, carry):
        tile(pl.multiple_of(kt * key_tile, key_tile), False)
        return carry
    lax.fori_loop(0, n_full, full_tile, 0)
    tile(pl.multiple_of(n_full * key_tile, key_tile), True)

    wlen = min(seq, NSA_WINDOW + qb)
    start = pl.multiple_of(jnp.clip(t0 - NSA_WINDOW, 0, seq - wlen), qb)
    band = kw_ref[pl.ds(start, wlen), :]
    kpos = start + lax.broadcasted_iota(jnp.int32, (qb, wlen), 1)
    wmask = (kpos <= qpos) & (kpos > qpos - NSA_WINDOW)
    for h in range(NSA_HEADS):
        g = h // NSA_GROUP
        p, l = _softmax_full(_dot_nt(qr_ref[:, h * hd:(h + 1) * hd], band[:, g * hd:(g + 1) * hd]), wmask)
        o_win = _dot(p.astype(BF16), band[:, LANES + g * hd:LANES + (g + 1) * hd]) / jnp.maximum(l, TINY)
        o_slc = (acc_scr[h * qb:(h + 1) * qb, g * hd:(g + 1) * hd]
                 / jnp.maximum(l_scr[h * qb:(h + 1) * qb, 0:hd], TINY))
        gate = gates_ref[:, 3 * h:3 * h + 3]
        o = gate[:, 0:1] * o_cmp[h] + gate[:, 1:2] * o_slc + gate[:, 2:3] * o_win
        o_ref[:, h * hd:(h + 1) * hd] = o.astype(o_ref.dtype)


def _nsa_prompt(qc, qr, gates, kc, vc, ks, kw, mcs, *, nb, seq, qb):
    nq = seq // qb
    n_cmp_pad = seq // CMP_STRIDE
    key_tile = min(seq, NSA_KEY_TILE)
    n_blk = _round_up(seq // SLC_BLOCK, LANES)
    kern = functools.partial(_nsa_prompt_kernel, qb=qb, seq=seq, key_tile=key_tile)
    qspec = lambda wd: pl.BlockSpec((qb, wd), lambda b, i: (b * nq + i, 0))
    seqspec = lambda rows, wd: pl.BlockSpec((rows, wd), lambda b, i: (b, 0))
    return pl.pallas_call(
        kern,
        grid=(nb, nq),
        in_specs=[qspec(512), qspec(512), qspec(LANES), seqspec(n_cmp_pad, LANES), seqspec(n_cmp_pad, LANES),
                  seqspec(seq, 256), seqspec(seq, 256), _const_spec(mcs.shape)],
        out_specs=qspec(512),
        out_shape=jax.ShapeDtypeStruct((nb * seq, 512), BF16),
        scratch_shapes=[pltpu.VMEM((NSA_HEADS * qb, 2 * HEAD_DIM + n_blk), BF16),
                        pltpu.VMEM((NSA_HEADS * qb, LANES), F32), pltpu.VMEM((NSA_HEADS * qb, LANES), F32),
                        pltpu.VMEM((NSA_HEADS * qb, 2 * HEAD_DIM), F32)],
        compiler_params=_cparams("parallel", "arbitrary"),
        name="nsa_prompt",
    )(qc, qr, gates, kc, vc, ks, kw, mcs)


def _attend_transposed(q_bf, q_f32, kt, vt, mask, newk, newv, mask_new, n_tok):
    m_rows = q_bf.shape[0]
    s = jnp.where(mask, _dot(q_bf, kt), NEG)
    lane = lax.broadcasted_iota(jnp.int32, (m_rows, TOK_PAD), 1)
    sn = jnp.full((m_rows, TOK_PAD), NEG, F32)
    for t in range(n_tok):
        sn = jnp.where(lane == t, jnp.sum(q_f32 * newk[t:t + 1], axis=-1, keepdims=True), sn)
    sn = jnp.where(mask_new, sn, NEG)
    m = jnp.maximum(jnp.max(s, axis=-1, keepdims=True), jnp.max(sn, axis=-1, keepdims=True))
    p = jnp.where(mask, jnp.exp(s - m), 0.0)
    pn = jnp.where(mask_new, jnp.exp(sn - m), 0.0)
    l = jnp.sum(p, axis=-1, keepdims=True) + jnp.sum(pn, axis=-1, keepdims=True)
    o = _dot_nt(p.astype(BF16), vt)
    for t in range(n_tok):
        o = o + pn[:, t:t + 1] * newv[t:t + 1]
    return o / jnp.maximum(l, TINY)


def _nsa_sample_kernel(*refs, n_pages, n_tok):
    (qc_ref, qr_ref, gates_ref, kc_ref, vc_ref) = refs[1:6]
    page_refs = refs[6:6 + n_pages]
    news_ref, win_ref, neww_ref, newwt_ref, mcs_ref, amat_ref, o_ref, owin_ref = refs[6 + n_pages:]
    hd = HEAD_DIM
    past = n_pages * PAGE_SIZE
    win = win_ref.shape[-1]
    m_rows = qc_ref.shape[1]
    tok = lax.broadcasted_iota(jnp.int32, (m_rows, 1), 0) % n_tok
    qpos = past + tok
    n_cmp_pad = kc_ref.shape[0]
    n_slc = (past + n_tok + SLC_BLOCK - 1) // SLC_BLOCK

    ncol = lax.broadcasted_iota(jnp.int32, (m_rows, n_cmp_pad), 1)
    cmask = ((ncol * CMP_STRIDE + CMP_LEN - 1) <= qpos) & (ncol < n_cmp_pad - 1)
    new_pos = past + lax.broadcasted_iota(jnp.int32, (m_rows, TOK_PAD), 1)
    new_ok = (new_pos <= qpos) & (new_pos < past + n_tok)
    wpos = past - win + lax.broadcasted_iota(jnp.int32, (m_rows, win), 1)
    wmask = (wpos <= qpos) & (wpos > qpos - NSA_WINDOW) & (wpos >= 0)
    wmask_new = new_ok & (new_pos > qpos - NSA_WINDOW)

    o_cmp, imps = [], []
    for g in range(NSA_KV_HEADS):
        ksl = slice(g * hd, (g + 1) * hd)
        p, l = _softmax_full(_dot_nt(qc_ref[g], kc_ref[:, ksl]), cmask)
        p = p / jnp.maximum(l, TINY)
        o_cmp.append(_dot(p.astype(BF16), vc_ref[:, ksl]))
        psum = jnp.dot(amat_ref[...], p, precision=HIGHEST, preferred_element_type=F32)
        imps.append(jnp.dot(psum, mcs_ref[...], precision=HIGHEST, preferred_element_type=F32))
    sel_all = _topk_mask(jnp.concatenate(imps, axis=0), jnp.concatenate([qpos] * NSA_KV_HEADS, axis=0), n_slc)
    expand = _block_expand(LANES, 0, past)
    new_blk = lax.broadcasted_iota(jnp.int32, (m_rows, LANES), 1) == past // SLC_BLOCK

    for g in range(NSA_KV_HEADS):
        ksl = slice(g * hd, (g + 1) * hd)
        vsl = slice(LANES + g * hd, LANES + (g + 1) * hd)
        qr = qr_ref[g]
        qr32 = qr.astype(F32)
        sel = sel_all[g * m_rows:(g + 1) * m_rows]
        sel_new = jnp.sum(jnp.where(new_blk, sel.astype(F32), 0.0), axis=-1, keepdims=True) > 0.5
        kt = jnp.concatenate([r[0, g].astype(BF16) for r in page_refs], axis=1)
        vt = jnp.concatenate([r[1, g].astype(BF16) for r in page_refs], axis=1)
        o_slc = _attend_transposed(qr, qr32, kt, vt, _dot(sel, expand) > 0.5,
                                   news_ref[:, ksl], news_ref[:, vsl], new_ok & sel_new, n_tok)
        o_win = _attend_transposed(qr, qr32, win_ref[0, g].astype(BF16), win_ref[1, g].astype(BF16), wmask,
                                   neww_ref[:, ksl], neww_ref[:, vsl], wmask_new, n_tok)
        gate = gates_ref[g]
        o_ref[g] = gate[:, 0:1] * o_cmp[g] + gate[:, 1:2] * o_slc + gate[:, 2:3] * o_win

        lane = lax.broadcasted_iota(jnp.int32, (hd, LANES), 1)
        for kv in range(2):
            shifted = pltpu.roll(win_ref[kv, g], win - n_tok, axis=1)
            last = shifted[:, win - LANES:]
            for t in range(n_tok):
                last = jnp.where(lane == LANES - n_tok + t, newwt_ref[kv, g, :, t:t + 1], last)
            if win > LANES:
                owin_ref[kv, g, :, 0:win - LANES] = shifted[:, 0:win - LANES]
            owin_ref[kv, g, :, win - LANES:] = last


def _nsa_sample(page_table, qc, qr, gates, kc, vc, pool_s, news, win_t, neww, neww_t, mcs, amat, *, n_tok):
    nb, n_pages = page_table.shape
    m_rows = qc.shape[2]
    kern = functools.partial(_nsa_sample_kernel, n_pages=n_pages, n_tok=n_tok)
    per_b = lambda a: pl.BlockSpec((None,) + a.shape[1:], lambda b, pt: (b,) + (0,) * (a.ndim - 1))
    n_cmp_pad = kc.shape[0] // nb
    cmp_spec = pl.BlockSpec((n_cmp_pad, LANES), lambda b, pt: (b, 0))
    page_specs = [pl.BlockSpec((None,) + pool_s.shape[1:], functools.partial(lambda pg, b, pt: (pt[b, pg], 0, 0, 0, 0), pg))
                  for pg in range(n_pages)]
    gs = pltpu.PrefetchScalarGridSpec(
        num_scalar_prefetch=1, grid=(nb,),
        in_specs=[per_b(qc), per_b(qr), per_b(gates), cmp_spec, cmp_spec] + page_specs +
                 [per_b(news), per_b(win_t), per_b(neww), per_b(neww_t),
                  pl.BlockSpec(mcs.shape, lambda b, pt: (0, 0)), pl.BlockSpec(amat.shape, lambda b, pt: (0, 0))],
        out_specs=[per_b(qc), per_b(win_t)])
    return pl.pallas_call(
        kern, grid_spec=gs,
        out_shape=[jax.ShapeDtypeStruct((nb, NSA_KV_HEADS, m_rows, HEAD_DIM), F32),
                   jax.ShapeDtypeStruct(win_t.shape, F32)],
        compiler_params=_cparams("parallel"),
        name="nsa_sample",
    )(page_table, qc, qr, gates, kc, vc, *([pool_s] * n_pages), news, win_t, neww, neww_t, mcs, amat)


C_GROUP_COLS = 3 * DIL_HEADS * HEAD_DIM


def _proj_c_kernel(x_ref, gmix_ref, w_ref, gq_ref, gk_ref, cs_ref, sn_ref, *out_refs):
    q_refs, r_refs = out_refs[:N_DIL], out_refs[N_DIL:]
    hn = _rms_rows(x_ref[...], gmix_ref[...]).astype(BF16)
    cs, sn = cs_ref[...], sn_ref[...]
    wd = DIL_HEADS * HEAD_DIM
    for g in range(N_DIL):
        z = _dot(hn, w_ref[:, g * C_GROUP_COLS:(g + 1) * C_GROUP_COLS])
        q = _rope(_rms_heads(z[:, 0:wd], gq_ref[g:g + 1, :]), cs, sn)
        q_refs[g][...] = (q * QSCALE).astype(BF16)
        r_refs[g][:, 0:wd] = _rope(_rms_heads(z[:, wd:2 * wd], gk_ref[g:g + 1, :]), cs, sn)
        r_refs[g][:, wd:] = z[:, 2 * wd:]


def _proj_c(x, gmix, w, gq, gk, cs, sn, tm):
    n = x.shape[0]
    tiles_per_seq = cs.shape[0] // tm
    row = lambda width: pl.BlockSpec((tm, width), lambda i: (i, 0))
    tab = pl.BlockSpec((tm, LANES), lambda i: (i % tiles_per_seq, 0))
    wd = DIL_HEADS * HEAD_DIM
    return pl.pallas_call(
        _proj_c_kernel,
        grid=(n // tm,),
        in_specs=[row(D_MODEL), _const_spec(gmix.shape), _const_spec(w.shape), _const_spec(gq.shape),
                  _const_spec(gk.shape), tab, tab],
        out_specs=[row(wd)] * N_DIL + [row(2 * wd)] * N_DIL,
        out_shape=[jax.ShapeDtypeStruct((n, wd), BF16)] * N_DIL + [jax.ShapeDtypeStruct((n, 2 * wd), F32)] * N_DIL,
        compiler_params=_cparams("parallel"),
        name="proj_c",
    )(x, gmix, w, gq, gk, cs, sn)


def _dil_prompt_kernel(q_ref, kp_ref, vp_ref, kc_ref, vc_ref, o_ref, ml_ref, *, tq):
    i = pl.program_id(2)
    row = lax.broadcasted_iota(jnp.int32, (tq, 2 * tq), 0)
    col = lax.broadcasted_iota(jnp.int32, (tq, 2 * tq), 1)
    delta = tq + row - col
    mask = (delta >= 0) & (delta < DIL_KEYS) & ((col >= tq) | (i > 0))
    k = jnp.concatenate([kp_ref[...], kc_ref[...]], axis=0).astype(BF16)
    v = jnp.concatenate([vp_ref[...], vc_ref[...]], axis=0).astype(BF16)
    lane = lax.broadcasted_iota(jnp.int32, (tq, LANES), 1)
    ml = jnp.zeros((tq, LANES), F32)
    hd = HEAD_DIM
    for h in range(DIL_HEADS):
        hs = slice(h * hd, (h + 1) * hd)
        s = jnp.where(mask, _dot_nt(q_ref[:, hs], k[:, hs]), NEG)
        m = jnp.max(s, axis=-1, keepdims=True)
        p = jnp.where(mask, jnp.exp(s - m), 0.0)
        l = jnp.sum(p, axis=-1, keepdims=True)
        o_ref[:, hs] = _dot(p.astype(BF16), v[:, hs])
        ml = jnp.where(lane == h, m, jnp.where(lane == DIL_HEADS + h, l, ml))
    ml_ref[...] = ml


def _dil_prompt(q, rows, *, nb, seq, dil, tq):
    wd = DIL_HEADS * HEAD_DIM
    assert tq >= DIL_KEYS - 1 and seq % (dil * tq) == 0
    nu = seq // dil // tq
    qv = q.reshape(nb * seq // dil, dil * wd)
    rv = rows.reshape(nb * seq // dil, dil * 2 * wd)
    cur = lambda off: (lambda b, r, i: (b * nu + i, 2 * r + off))
    prev = lambda off: (lambda b, r, i: (b * nu + jnp.maximum(i - 1, 0), 2 * r + off))
    blk = lambda imap: pl.BlockSpec((tq, wd), imap)
    o, ml = pl.pallas_call(
        functools.partial(_dil_prompt_kernel, tq=tq),
        grid=(nb, dil, nu),
        in_specs=[blk(lambda b, r, i: (b * nu + i, r)), blk(prev(0)), blk(prev(1)), blk(cur(0)), blk(cur(1))],
        out_specs=[blk(lambda b, r, i: (b * nu + i, r)), pl.BlockSpec((tq, LANES), lambda b, r, i: (b * nu + i, r))],
        out_shape=[jax.ShapeDtypeStruct(qv.shape, F32), jax.ShapeDtypeStruct((qv.shape[0], dil * LANES), F32)],
        compiler_params=_cparams("parallel", "parallel", "arbitrary"),
        name="dil_prompt",
    )(qv, rv, rv, rv, rv)
    return o.reshape(nb * seq, wd), ml.reshape(nb * seq, LANES)


DIL_SAMPLE_HEADS = 4


def _dil_sample_kernel(q_ref, n0_ref, n1_ref, n2_ref, c0_ref, c1_ref, c2_ref, o_ref, oc0_ref, oc1_ref, oc2_ref,
                       *, n_tok):
    n_refs, c_refs, oc_refs = (n0_ref, n1_ref, n2_ref), (c0_ref, c1_ref, c2_ref), (oc0_ref, oc1_ref, oc2_ref)
    tp = lax.broadcasted_iota(jnp.int32, (1, TOK_PAD), 1)
    lane_t = lax.broadcasted_iota(jnp.int32, (HEAD_DIM, TOK_PAD), 1)
    for hl in range(q_ref.shape[1]):
        stats = [[None] * N_DIL for _ in range(n_tok)]
        for g, (w, dil) in enumerate(DIL_PAIRS):
            nk = n_refs[g][0, hl]
            nv = n_refs[g][1, hl]
            row = lax.broadcasted_iota(jnp.int32, (1, w), 1)
            n_ch = w // LANES
            q_cols = [q_ref[g, hl, :, t:t + 1] for t in range(n_tok)]
            q_wide = [jnp.broadcast_to(qc, (HEAD_DIM, LANES)) for qc in q_cols]
            s_parts = [[] for _ in range(n_tok)]
            for c in range(n_ch):
                ktc = c_refs[g][0, hl, :, c * LANES:(c + 1) * LANES]
                for t in range(n_tok):
                    s_parts[t].append(jnp.sum(ktc * q_wide[t], axis=0, keepdims=True))
            probs, pnews = [], []
            for t in range(n_tok):
                delta = w + t - row
                valid = ((delta & (dil - 1)) == 0) & (delta <= (DIL_KEYS - 1) * dil)
                dn = t - tp
                valid_n = (dn >= 0) & ((dn & (dil - 1)) == 0)
                s_all = jnp.concatenate(s_parts[t], axis=1) if n_ch > 1 else s_parts[t][0]
                s = jnp.where(valid, s_all, NEG)
                sn = jnp.where(valid_n, jnp.sum(nk * q_cols[t], axis=0, keepdims=True), NEG)
                m = jnp.maximum(jnp.max(s, axis=-1, keepdims=True), jnp.max(sn, axis=-1, keepdims=True))
                p = jnp.where(valid, jnp.exp(s - m), 0.0)
                pn = jnp.where(valid_n, jnp.exp(sn - m), 0.0)
                l = jnp.sum(p, axis=-1, keepdims=True) + jnp.sum(pn, axis=-1, keepdims=True)
                probs.append(p)
                pnews.append(pn)
                stats[t][g] = (None, m, l)
            accs = [jnp.zeros((HEAD_DIM, LANES), F32) for _ in range(n_tok)]
            for c in range(n_ch):
                vtc = c_refs[g][1, hl, :, c * LANES:(c + 1) * LANES]
                for t in range(n_tok):
                    accs[t] = accs[t] + vtc * probs[t][:, c * LANES:(c + 1) * LANES]
            for t in range(n_tok):
                o = jnp.sum(accs[t], axis=-1, keepdims=True) + jnp.sum(nv * pnews[t], axis=-1, keepdims=True)
                stats[t][g] = (o,) + stats[t][g][1:]
            lane = lax.broadcasted_iota(jnp.int32, (HEAD_DIM, LANES), 1)
            for kv, new in ((0, nk), (1, nv)):
                shifted = pltpu.roll(c_refs[g][kv, hl], w - n_tok, axis=1)
                last = shifted[:, w - LANES:]
                for t in range(n_tok):
                    last = jnp.where(lane == LANES - n_tok + t, new[:, t:t + 1], last)
                if w > LANES:
                    oc_refs[g][kv, hl, :, 0:w - LANES] = shifted[:, 0:w - LANES]
                oc_refs[g][kv, hl, :, w - LANES:] = last
        out = jnp.zeros((HEAD_DIM, TOK_PAD), F32)
        for t in range(n_tok):
            parts = stats[t]
            mx = functools.reduce(jnp.maximum, [m for _, m, _ in parts])
            num = sum(jnp.exp(m - mx) * o for o, m, _ in parts)
            den = sum(jnp.exp(m - mx) * l for _, m, l in parts)
            out = jnp.where(lane_t == t, num / den, out)
        o_ref[hl] = out


def _dil_sample(q, news, caches, *, n_tok):
    nb = q.shape[0]
    hs = DIL_SAMPLE_HEADS
    for g, (w, d) in enumerate(DIL_PAIRS):
        assert caches[g].shape[-1] == w and w // d == DIL_KEYS - 1
        assert d & (d - 1) == 0
    blk = lambda a: pl.BlockSpec((None, a.shape[1], hs) + a.shape[3:], lambda b, hh: (b, 0, hh, 0, 0))
    o_spec = pl.BlockSpec((None, hs, HEAD_DIM, TOK_PAD), lambda b, hh: (b, hh, 0, 0))
    return pl.pallas_call(
        functools.partial(_dil_sample_kernel, n_tok=n_tok),
        grid=(nb, DIL_HEADS // hs),
        in_specs=[blk(q)] + [blk(a) for a in news] + [blk(a) for a in caches],
        out_specs=[o_spec] + [blk(a) for a in caches],
        out_shape=[jax.ShapeDtypeStruct((nb, DIL_HEADS, HEAD_DIM, TOK_PAD), F32)]
                  + [jax.ShapeDtypeStruct(a.shape, F32) for a in caches],
        compiler_params=_cparams("parallel", "parallel"),
        name="dil_sample",
    )(q, *news, *caches)


FF_CHUNK = 1024


def _post_kernel(*refs, n_cat, combine):
    h_ref = refs[0]
    mix_refs = refs[1:1 + (2 * N_DIL if combine else n_cat)]
    (wo_ref, gmlp_ref, w1_ref, w2_ref, gple_ref, wg_ref, p_ref, wp_ref, out_ref) = refs[1 + len(mix_refs):1 + len(mix_refs) + 9]
    hd = HEAD_DIM
    if combine:
        comb_scr = refs[-1]
        o_refs, ml_refs = mix_refs[:N_DIL], mix_refs[N_DIL:]
        for h in range(DIL_HEADS):
            ms = [r[:, h:h + 1] for r in ml_refs]
            ls = [r[:, DIL_HEADS + h:DIL_HEADS + h + 1] for r in ml_refs]
            mx = functools.reduce(jnp.maximum, ms)
            ws = [jnp.exp(m - mx) for m in ms]
            num = sum(w * r[:, h * hd:(h + 1) * hd] for w, r in zip(ws, o_refs))
            den = sum(w * l for w, l in zip(ws, ls))
            comb_scr[:, h * hd:(h + 1) * hd] = (num / den).astype(BF16)
        y = _dot(comb_scr[...], wo_ref[...])
    else:
        y = None
        for k, r in enumerate(mix_refs):
            part = _dot(r[...].astype(BF16), wo_ref[k * r.shape[1]:(k + 1) * r.shape[1], :])
            y = part if y is None else y + part
    h1 = h_ref[...] + y
    hn = _rms_rows(h1, gmlp_ref[...]).astype(BF16)
    acc = jnp.zeros(h1.shape, F32)
    for c in range(D_FF // FF_CHUNK):
        u = _dot(hn, w1_ref[:, c * FF_CHUNK:(c + 1) * FF_CHUNK])
        acc = acc + _dot(jnp.square(jnp.maximum(u, 0.0)).astype(BF16), w2_ref[c * FF_CHUNK:(c + 1) * FF_CHUNK, :])
    h2 = h1 + acc
    gate = jax.nn.sigmoid(_dot(_rms_rows(h2, gple_ref[...]).astype(BF16), wg_ref[...]))
    out_ref[...] = h2 + gate * _dot(p_ref[...].astype(BF16), wp_ref[...])


def _post(h, mix, wo, gmlp, w1, w2, gple, wg, p, wp, *, combine, tm):
    n = h.shape[0]
    row = lambda a: pl.BlockSpec((tm, a.shape[1]), lambda i: (i, 0))
    weights = (wo, gmlp, w1, w2, gple, wg)
    wspec = lambda a: pl.BlockSpec(a.shape, lambda i: (0, 0), pipeline_mode=pl.Buffered(1))
    kern = functools.partial(_post_kernel, n_cat=len(mix), combine=combine)
    return pl.pallas_call(
        kern,
        grid=(n // tm,),
        in_specs=[row(h)] + [row(a) for a in mix] + [wspec(a) for a in weights] + [row(p), wspec(wp)],
        out_specs=row(h),
        out_shape=jax.ShapeDtypeStruct(h.shape, F32),
        scratch_shapes=[pltpu.VMEM((tm, DIL_HEADS * HEAD_DIM), BF16)] if combine else [],
        compiler_params=_cparams("parallel"),
        name="post",
    )(h, *mix, *weights, p, wp)


def _rope_tables(pos):
    half = HEAD_DIM // 2
    freq = ROPE_THETA ** (-jnp.arange(half, dtype=F32) / half)
    ang = pos.astype(F32)[:, None] * freq[None, :]
    cos, sin = jnp.cos(ang), jnp.sin(ang)
    return jnp.tile(jnp.concatenate([cos, cos], axis=1), (1, 2)), jnp.tile(jnp.concatenate([-sin, sin], axis=1), (1, 2))


def _cmp_to_slc(n_cmp, n_slc, rows_pad, cols_pad):
    cs = jnp.arange(n_cmp)[:, None] * CMP_STRIDE
    ss = jnp.arange(n_slc)[None, :] * SLC_BLOCK
    ov = jnp.maximum(jnp.minimum(cs + CMP_LEN, ss + SLC_BLOCK) - jnp.maximum(cs, ss), 0)
    return jnp.pad(ov.astype(F32) / CMP_LEN, ((0, rows_pad - n_cmp), (0, cols_pad - n_slc)))


def _round_up(x, m):
    return (x + m - 1) // m * m


def _block_diag2(w):
    z = jnp.zeros_like(w)
    return jnp.concatenate([jnp.concatenate([w, z], axis=2), jnp.concatenate([z, w], axis=2)], axis=1)


def kernel(x_prompt, x_sample, state_gla, cache_nsa_cmp, cache_nsa_slc, cache_nsa_win, cache_dil_0, cache_dil_1,
           cache_dil_2, page_table, p_prompt, p_sample, norm_mix, norm_mlp, norm_ple, a_w_in, a_w_out, gla_w_a2,
           gla_b_a, gla_g_norm, nsa_g_qk, nsa_w_phi, nsa_pe, c_w_in, c_g_qk, c_w_out, mlp_w1, mlp_w2, ple_w_proj,
           ple_w_gate):
    nb_p, seq, _ = x_prompt.shape
    nb_s, n_tok, _ = x_sample.shape
    depth = norm_mix.shape[0]
    past = page_table.shape[1] * PAGE_SIZE
    assert n_tok < CMP_STRIDE and n_tok <= 8 and seq % 512 == 0
    tm_p = 256
    tm_s = min(256, nb_s * n_tok)
    cs_p, sn_p = _rope_tables(jnp.arange(seq))
    cs_s, sn_s = _rope_tables(past + jnp.arange(nb_s * n_tok) % n_tok)
    hp = x_prompt.reshape(nb_p * seq, D_MODEL)
    hs = x_sample.reshape(nb_s * n_tok, D_MODEL)
    dil_caches = (cache_dil_0, cache_dil_1, cache_dil_2)
    outs = {k: [] for k in ("gla_p", "gla_s", "cmp_p", "cmp_s", "slc_p", "slc_s", "win_p", "win_s")}
    dil_p = [[] for _ in DIL_PAIRS]
    dil_s = [[] for _ in DIL_PAIRS]
    kvrow = (2, NSA_KV_HEADS, HEAD_DIM)
    wd = DIL_HEADS * HEAD_DIM

    for i in range(depth):
        j = i // 2
        gmix = norm_mix[i][None, :]
        if i % 2 == 0:
            w_in = a_w_in[j]
            pad_cols = lambda a, width: jnp.pad(a, ((0, 0), (0, width - a.shape[1])))
            w_pad = jnp.concatenate([w_in[:, :1536], pad_cols(w_in[:, 1536:1552], LANES), w_in[:, 1552:2832],
                                     pad_cols(w_in[:, 2832:], LANES)], axis=1).astype(BF16)
            wa2 = jnp.pad(gla_w_a2[j], ((0, LANES - GLA_LOWRANK), (0, 0))).astype(BF16)
            ba = gla_b_a[j][None, :]
            gqk = jnp.tile(nsa_g_qk[j], (1, NSA_HEADS))
            gnorm = gla_g_norm[j][None, :]
            wk, wv = (_block_diag2(nsa_w_phi[j, c]).astype(BF16) for c in range(2))
            pek, pev = (jnp.tile(nsa_pe[j, c], (1, 2)) for c in range(2))
            wo = a_w_out[j].astype(BF16)

            gla, qc, qr, gates, rc, rs, rw, sbf, wbf = _proj_a(hp, gmix, w_pad, wa2, ba, gqk, cs_p, sn_p, tm_p)
            s0 = jnp.zeros((nb_p, GLA_HEADS, GLA_DK, GLA_DV), F32)
            chunk = min(GLA_CHUNK, seq)
            o_gla, s_p = _gla(gla, s0, gnorm, nb=nb_p, rows_per_seq=seq, chunk=chunk, n_valid=chunk, out_dtype=BF16)
            kc, vc = _compress_prompt(rc, nb_p, seq, wk, wv, pek, pev)
            n_cmp_pad = seq // CMP_STRIDE
            mcs = _cmp_to_slc(n_cmp_pad - 1, seq // SLC_BLOCK, n_cmp_pad, _round_up(seq // SLC_BLOCK, LANES))
            o_nsa = _nsa_prompt(qc, qr, gates, kc, vc, sbf, wbf, mcs, nb=nb_p, seq=seq, qb=128)
            mix_p = (o_gla, o_nsa)
            outs["gla_p"].append(s_p)
            outs["cmp_p"].append(rc.reshape((nb_p, seq) + kvrow))
            outs["slc_p"].append(rs.reshape((nb_p, seq) + kvrow))
            outs["win_p"].append(rw.reshape((nb_p, seq) + kvrow)[:, seq - min(NSA_WINDOW, seq):])

            gla, qc, qr, gates, rc, rs, rw, _, _ = _proj_a(hs, gmix, w_pad, wa2, ba, gqk, cs_s, sn_s, tm_s)
            gla8 = jnp.pad(gla.reshape(nb_s, n_tok, G_END), ((0, 0), (0, 8 - n_tok), (0, 0))).reshape(nb_s * 8, G_END)
            o_gla8, s_s = _gla(gla8, state_gla[j], gnorm, nb=nb_s, rows_per_seq=8, chunk=8, n_valid=n_tok, out_dtype=F32)
            o_gla = o_gla8.reshape(nb_s, 8, -1)[:, :n_tok].reshape(nb_s * n_tok, -1)
            kc, vc = _compress_pages(cache_nsa_cmp[j].reshape(-1, PAGE_SIZE, 256), page_table, wk, wv, pek, pev)
            n_cmp_pad = past // CMP_STRIDE
            n_slc = (past + n_tok + SLC_BLOCK - 1) // SLC_BLOCK
            mcs = _cmp_to_slc(n_cmp_pad - 1, n_slc, n_cmp_pad, LANES)
            m_rows = NSA_GROUP * n_tok
            regroup = lambda a: a.reshape(nb_s, n_tok, NSA_KV_HEADS, NSA_GROUP, -1).transpose(0, 2, 3, 1, 4).reshape(
                nb_s, NSA_KV_HEADS, m_rows, -1)
            g16 = jnp.pad(regroup(gates[:, :3 * NSA_HEADS]), ((0, 0), (0, 0), (0, 0), (0, LANES - 3)))
            tok_id = jnp.arange(m_rows) % n_tok
            amat = (tok_id[:, None] == tok_id[None, :]).astype(F32)
            pad_rows = lambda a: jnp.pad(a.reshape(nb_s, n_tok, 256), ((0, 0), (0, TOK_PAD - n_tok), (0, 0)))
            keyt = lambda a: a.transpose(0, 2, 3, 4, 1)
            neww_t = jnp.pad(keyt(rw.reshape((nb_s, n_tok) + kvrow)), [(0, 0)] * 4 + [(0, TOK_PAD - n_tok)])
            o16, win_new = _nsa_sample(page_table, regroup(qc), regroup(qr), g16, kc, vc, keyt(cache_nsa_slc[j]),
                                       pad_rows(rs), keyt(cache_nsa_win[j]), pad_rows(rw), neww_t, mcs, amat,
                                       n_tok=n_tok)
            o_nsa = o16.reshape(nb_s, NSA_KV_HEADS, NSA_GROUP, n_tok, HEAD_DIM).transpose(0, 3, 1, 2, 4).reshape(
                nb_s * n_tok, NSA_HEADS * HEAD_DIM)
            mix_s = (o_gla, o_nsa)
            outs["gla_s"].append(s_s)
            outs["cmp_s"].append(rc.reshape((nb_s, n_tok) + kvrow))
            outs["slc_s"].append(rs.reshape((nb_s, n_tok) + kvrow))
            outs["win_s"].append(win_new.transpose(0, 4, 1, 2, 3))
            combine = False
        else:
            w_c = c_w_in[j].astype(BF16)
            gq = jnp.tile(c_g_qk[j][:, 0], (1, DIL_HEADS))
            gk = jnp.tile(c_g_qk[j][:, 1], (1, DIL_HEADS))
            wo = c_w_out[j].astype(BF16)
            dilrow = (2, DIL_HEADS, HEAD_DIM)

            res = _proj_c(hp, gmix, w_c, gq, gk, cs_p, sn_p, tm_p)
            qs, rows = res[:N_DIL], res[N_DIL:]
            o_parts, ml_parts = [], []
            for g, (w, d) in enumerate(DIL_PAIRS):
                o_g, ml_g = _dil_prompt(qs[g], rows[g], nb=nb_p, seq=seq, dil=d, tq=128)
                o_parts.append(o_g)
                ml_parts.append(ml_g)
                dil_p[g].append(rows[g].reshape((nb_p, seq) + dilrow)[:, seq - min(w, seq):])
            mix_p = tuple(o_parts) + tuple(ml_parts)

            res = _proj_c(hs, gmix, w_c, gq, gk, cs_s, sn_s, tm_s)
            qs, rows = res[:N_DIL], res[N_DIL:]
            pad_tok = lambda a: jnp.pad(a, [(0, 0)] * (a.ndim - 1) + [(0, TOK_PAD - n_tok)])
            q_t = pad_tok(jnp.stack([q.astype(F32).reshape(nb_s, n_tok, DIL_HEADS, HEAD_DIM).transpose(0, 2, 3, 1)
                                     for q in qs], axis=1))
            news_t = [pad_tok(r.reshape((nb_s, n_tok) + dilrow).transpose(0, 2, 3, 4, 1)) for r in rows]
            caches_t = [c[j].transpose(0, 2, 3, 4, 1) for c in dil_caches]
            o_t, *new_caches = _dil_sample(q_t, news_t, caches_t, n_tok=n_tok)
            mix_s = (o_t[..., :n_tok].transpose(0, 3, 1, 2).reshape(nb_s * n_tok, wd),)
            for g in range(N_DIL):
                dil_s[g].append(new_caches[g].transpose(0, 4, 1, 2, 3))
            combine = True

        lw = (norm_mlp[i][None, :], mlp_w1[i].astype(BF16), mlp_w2[i].astype(BF16), norm_ple[i][None, :],
              ple_w_gate[i].astype(BF16))
        wp = ple_w_proj[i].astype(BF16)
        hp = _post(hp, mix_p, wo, *lw, p_prompt[i].reshape(nb_p * seq, PLE_DIM), wp, combine=combine, tm=tm_p)
        hs = _post(hs, mix_s, wo, *lw, p_sample[i].reshape(nb_s * n_tok, PLE_DIM), wp, combine=False, tm=tm_s)

    st = jnp.stack
    return (hp.reshape(x_prompt.shape), hs.reshape(x_sample.shape),
            st(outs["gla_p"]), st(outs["gla_s"]), st(outs["cmp_p"]), st(outs["cmp_s"]),
            st(outs["slc_p"]), st(outs["slc_s"]), st(outs["win_p"]), st(outs["win_s"]),
            st(dil_p[0]), st(dil_s[0]), st(dil_p[1]), st(dil_s[1]), st(dil_p[2]), st(dil_s[2]))
```

```python
import functools

import jax
import jax.numpy as jnp
from jax import lax
from jax.experimental import pallas as pl
from jax.experimental.pallas import tpu as pltpu

F32 = jnp.float32
BF16 = jnp.bfloat16
HIGHEST = lax.Precision.HIGHEST

D_MODEL = 1024
PAGE_SIZE = 128
HEAD_DIM = 64
GLA_HEADS = 4
GLA_DK = 64
GLA_DV = 128
GLA_LOWRANK = 16
GLA_TAU = 16.0
GLA_CHUNK = 64
GLA_SUB = 16
NSA_HEADS = 8
NSA_KV_HEADS = 2
NSA_GROUP = NSA_HEADS // NSA_KV_HEADS
CMP_LEN = 32
CMP_STRIDE = 16
SLC_BLOCK = 64
SLC_TOPK = 16
NSA_WINDOW = 512
DIL_PAIRS = ((128, 1), (512, 4), (2048, 16))
N_DIL = 3
DIL_HEADS = 8
DIL_KEYS = 129
D_FF = 4 * D_MODEL
PLE_DIM = 256
ROPE_THETA = 10000.0
EPS = 1e-6
NEG = -1e30
BIG = 1e30
TINY = 1e-20
REMOVED = -3e38

LANES = 128
TOK_PAD = 8
NSA_KEY_TILE = 512
NSA_Q_BLOCK = 256
VMEM_LIMIT = 56 * 1024 * 1024
QSCALE = HEAD_DIM ** -0.5

A_GQ, A_GK, A_GV, A_GR, A_GA, A_NQ, A_NKV, A_NG, A_END = 0, 256, 512, 1024, 1536, 1664, 2176, 2944, 3072
G_Q, G_K, G_V, G_R, G_LG, G_END = 0, 256, 512, 1024, 1536, 1792


def _cparams(*sem):
    return pltpu.CompilerParams(dimension_semantics=sem, vmem_limit_bytes=VMEM_LIMIT)


def _rms_rows(x, g):
    return x * lax.rsqrt(jnp.mean(x * x, axis=-1, keepdims=True) + EPS) * g


def _cols(x):
    return [x[:, c * LANES:(c + 1) * LANES] for c in range(x.shape[1] // LANES)]


def _rms_heads(x, g):
    out = []
    for c, xc in enumerate(_cols(x)):
        low = lax.broadcasted_iota(jnp.int32, xc.shape, 1) < HEAD_DIM
        sq = xc * xc
        s_lo = jnp.sum(jnp.where(low, sq, 0.0), axis=-1, keepdims=True)
        s_hi = jnp.sum(jnp.where(low, 0.0, sq), axis=-1, keepdims=True)
        ms = jnp.where(low, s_lo, s_hi) * (1.0 / HEAD_DIM)
        out.append(xc * lax.rsqrt(ms + EPS) * g[:, c * LANES:(c + 1) * LANES])
    return jnp.concatenate(out, axis=1) if len(out) > 1 else out[0]


def _rope(x, cs, sn):
    out = []
    for xc in _cols(x):
        lane = lax.broadcasted_iota(jnp.int32, xc.shape, 1)
        swapped = jnp.where((lane & 32) == 0, pltpu.roll(xc, LANES - 32, axis=1), pltpu.roll(xc, 32, axis=1))
        out.append(xc * cs + swapped * sn)
    return jnp.concatenate(out, axis=1) if len(out) > 1 else out[0]


def _dot(a, b):
    return jnp.dot(a, b, preferred_element_type=F32)


def _dot_nt(a, b):
    return lax.dot_general(a, b, (((1,), (1,)), ((), ())), preferred_element_type=F32)


def _dot_tn(a, b):
    return lax.dot_general(a, b, (((0,), (0,)), ((), ())), preferred_element_type=F32)


def _softmax_full(s, mask):
    s = jnp.where(mask, s, NEG)
    m = jnp.max(s, axis=-1, keepdims=True)
    p = jnp.where(mask, jnp.exp(s - m), 0.0)
    l = jnp.sum(p, axis=-1, keepdims=True)
    return p, l


def _flash_step(state, s, mask, v):
    m, l, acc = state
    s = jnp.where(mask, s, NEG)
    m_new = jnp.maximum(m, jnp.max(s, axis=-1, keepdims=True))
    alpha = jnp.exp(m - m_new)
    p = jnp.where(mask, jnp.exp(s - m_new), 0.0)
    l = alpha * l + jnp.sum(p, axis=-1, keepdims=True)
    acc = alpha * acc + _dot(p.astype(BF16), v)
    return m_new, l, acc


def _flash_step_bias(state, s, v):
    m, l, acc = state
    m_new = jnp.maximum(m, jnp.max(s, axis=-1, keepdims=True))
    alpha = jnp.exp(m - m_new)
    p = jnp.exp(s - m_new)
    l = alpha * l + jnp.sum(p, axis=-1, keepdims=True)
    acc = alpha * acc + _dot(p.astype(BF16), v)
    return m_new, l, acc


def _proj_a_kernel(x_ref, gmix_ref, w_ref, wa2_ref, ba_ref, gqk_ref, cs_ref, sn_ref,
                   gla_ref, qc_ref, qr_ref, gates_ref, rc_ref, rs_ref, rw_ref, sbf_ref, wbf_ref):
    hn = _rms_rows(x_ref[...], gmix_ref[...])
    z = _dot(hn.astype(BF16), w_ref[...])
    cs, sn = cs_ref[...], sn_ref[...]
    gla_ref[:, G_Q:G_K] = z[:, A_GQ:A_GK] * (GLA_DK ** -0.5)
    gla_ref[:, G_K:G_LG] = z[:, A_GK:A_GA]
    pre = _dot(z[:, A_GA:A_NQ].astype(BF16), wa2_ref[...]) + ba_ref[...]
    gla_ref[:, G_LG:G_END] = jax.nn.log_sigmoid(pre) / GLA_TAU
    qn = _rms_heads(z[:, A_NQ:A_NKV], gqk_ref[0:1, :])
    qc_ref[...] = (qn * QSCALE).astype(BF16)
    qr_ref[...] = (_rope(qn, cs, sn) * QSCALE).astype(BF16)
    gates_ref[...] = jax.nn.sigmoid(z[:, A_NG:A_END])
    kv = [z[:, A_NKV + i * LANES:A_NKV + (i + 1) * LANES] for i in range(6)]
    rc_ref[:, 0:LANES] = _rms_heads(kv[0], gqk_ref[1:2, 0:LANES])
    rc_ref[:, LANES:] = kv[1]
    ks = _rope(_rms_heads(kv[2], gqk_ref[2:3, 0:LANES]), cs, sn)
    rs_ref[:, 0:LANES] = ks
    rs_ref[:, LANES:] = kv[3]
    kw = _rope(_rms_heads(kv[4], gqk_ref[3:4, 0:LANES]), cs, sn)
    rw_ref[:, 0:LANES] = kw
    rw_ref[:, LANES:] = kv[5]
    sbf_ref[:, 0:LANES] = ks.astype(BF16)
    sbf_ref[:, LANES:] = kv[3].astype(BF16)
    wbf_ref[:, 0:LANES] = kw.astype(BF16)
    wbf_ref[:, LANES:] = kv[5].astype(BF16)


def _const_spec(shape):
    nd = len(shape)
    return pl.BlockSpec(shape, lambda *_: (0,) * nd)


def _proj_a(x, gmix, w, wa2, ba, gqk, cs, sn, tm):
    n = x.shape[0]
    tiles_per_seq = cs.shape[0] // tm
    row = lambda width: pl.BlockSpec((tm, width), lambda i: (i, 0))
    tab = pl.BlockSpec((tm, LANES), lambda i: (i % tiles_per_seq, 0))
    widths = (G_END, 512, 512, LANES, 256, 256, 256, 256, 256)
    dtypes = (F32, BF16, BF16, F32, F32, F32, F32, BF16, BF16)
    return pl.pallas_call(
        _proj_a_kernel,
        grid=(n // tm,),
        in_specs=[row(D_MODEL), _const_spec(gmix.shape), _const_spec(w.shape), _const_spec(wa2.shape),
                  _const_spec(ba.shape), _const_spec(gqk.shape), tab, tab],
        out_specs=[row(wd) for wd in widths],
        out_shape=[jax.ShapeDtypeStruct((n, wd), dt) for wd, dt in zip(widths, dtypes)],
        compiler_params=_cparams("parallel"),
        name="proj_a",
    )(x, gmix, w, wa2, ba, gqk, cs, sn)


def _gla_kernel(gla_ref, s0_ref, gnorm_ref, o_ref, sout_ref, s_scr, *, chunk, sub, n_chunks, n_valid):
    i = pl.program_id(1)
    n_pairs = GLA_HEADS // 2
    n_sub = chunk // sub
    dk2, dv2 = 2 * GLA_DK, 2 * GLA_DV

    @pl.when(i == 0)
    def _():
        s_scr[...] = jnp.zeros(s_scr.shape, F32)
        for h in range(GLA_HEADS):
            p, e = divmod(h, 2)
            s_scr[p, e * GLA_DK:(e + 1) * GLA_DK, e * GLA_DV:(e + 1) * GLA_DV] = s0_ref[h]

    tri = (lax.broadcasted_iota(jnp.int32, (chunk, chunk), 0)
           >= lax.broadcasted_iota(jnp.int32, (chunk, chunk), 1)).astype(F32)
    low = lax.broadcasted_iota(jnp.int32, (chunk, dk2), 1) < GLA_DK
    low_sub = lax.broadcasted_iota(jnp.int32, (sub, dk2), 1) < GLA_DK
    t_sub = lax.broadcasted_iota(jnp.int32, (sub, 1), 0)
    row_blk = lax.broadcasted_iota(jnp.int32, (chunk, chunk), 0) // sub
    col_idx = lax.broadcasted_iota(jnp.int32, (chunk, chunk), 1)
    diag_blocks = ((lax.broadcasted_iota(jnp.int32, (dk2, dv2), 0) < GLA_DK)
                   == (lax.broadcasted_iota(jnp.int32, (dk2, dv2), 1) < GLA_DV))
    eye = (lax.broadcasted_iota(jnp.int32, (dk2, dk2), 0) == lax.broadcasted_iota(jnp.int32, (dk2, dk2), 1))
    gnorm = gnorm_ref[...]

    def do_chunk(c, carry):
        rows = pl.ds(pl.multiple_of(c * chunk, chunk), chunk)
        b_all = jnp.dot(tri, gla_ref[rows, G_LG:G_END], precision=HIGHEST, preferred_element_type=F32)
        for p in range(n_pairs):
            q2 = gla_ref[rows, G_Q + p * dk2:G_Q + (p + 1) * dk2]
            k2 = gla_ref[rows, G_K + p * dk2:G_K + (p + 1) * dk2]
            v2 = gla_ref[rows, G_V + p * dv2:G_V + (p + 1) * dv2]
            b2 = b_all[:, p * dk2:(p + 1) * dk2]
            s_pair = s_scr[p]
            o2 = _dot((q2 * jnp.exp(b2)).astype(BF16), s_pair.astype(BF16))

            if n_sub > 1:
                bref_rows = jnp.concatenate(
                    [b2[0:sub]] + [jnp.broadcast_to(b2[i * sub - 1:i * sub], (sub, dk2)) for i in range(1, n_sub)], axis=0)
                qt = q2 * jnp.exp(jnp.minimum(b2 - bref_rows, 0.0))
                qt_even = jnp.where(low, qt, 0.0).astype(BF16)
                qt_odd = jnp.where(low, 0.0, qt).astype(BF16)
                a_even = jnp.zeros((chunk, chunk), F32)
                a_odd = jnp.zeros((chunk, chunk), F32)
                for i in range(1, n_sub):
                    kt = (k2 * jnp.exp(jnp.minimum(b2[i * sub - 1:i * sub] - b2, 0.0))).astype(BF16)
                    take = (row_blk == i) & (col_idx < i * sub)
                    a_even = jnp.where(take, _dot_nt(qt_even, kt), a_even)
                    a_odd = jnp.where(take, _dot_nt(qt_odd, kt), a_odd)
                o2 = o2 + jnp.concatenate([_dot(a_even.astype(BF16), v2[:, 0:GLA_DV].astype(BF16)),
                                           _dot(a_odd.astype(BF16), v2[:, GLA_DV:].astype(BF16))], axis=1)

            diag = []
            for i in range(n_sub):
                sl = slice(i * sub, (i + 1) * sub)
                qs, ks, bs, vs = q2[sl], k2[sl], b2[sl], v2[sl]
                od = jnp.zeros((sub, dv2), F32)
                for jj in range(max(0, min(sub, n_valid - i * sub))):
                    w = qs * ks[jj:jj + 1] * jnp.exp(jnp.minimum(bs - bs[jj:jj + 1], 0.0))
                    keep = t_sub >= jj
                    a_e = jnp.where(keep, jnp.sum(jnp.where(low_sub, w, 0.0), axis=-1, keepdims=True), 0.0)
                    a_o = jnp.where(keep, jnp.sum(jnp.where(low_sub, 0.0, w), axis=-1, keepdims=True), 0.0)
                    od = od + jnp.concatenate([a_e * vs[jj:jj + 1, 0:GLA_DV], a_o * vs[jj:jj + 1, GLA_DV:]], axis=1)
                diag.append(od)
            o2 = o2 + (jnp.concatenate(diag, axis=0) if n_sub > 1 else diag[0])

            b_last = b2[chunk - 1:chunk, :]
            kdec = k2 * jnp.exp(b_last - b2)
            upd = _dot_tn(kdec.astype(BF16), v2.astype(BF16))
            decay_col = jnp.sum(jnp.where(eye, jnp.exp(b_last), 0.0), axis=1, keepdims=True)
            s_scr[p] = decay_col * s_pair + jnp.where(diag_blocks, upd, 0.0)
            for e in range(2):
                h = 2 * p + e
                on = _rms_rows(o2[:, e * GLA_DV:(e + 1) * GLA_DV], gnorm)
                r = gla_ref[rows, G_R + h * GLA_DV:G_R + (h + 1) * GLA_DV]
                o_ref[rows, h * GLA_DV:(h + 1) * GLA_DV] = (on * (r * jax.nn.sigmoid(r))).astype(o_ref.dtype)
        return carry
    lax.fori_loop(0, n_chunks, do_chunk, 0)

    @pl.when(i == pl.num_programs(1) - 1)
    def _():
        for h in range(GLA_HEADS):
            p, e = divmod(h, 2)
            sout_ref[h] = s_scr[p, e * GLA_DK:(e + 1) * GLA_DK, e * GLA_DV:(e + 1) * GLA_DV]


def _gla(gla, s0, gnorm, *, nb, rows_per_seq, chunk, n_valid, out_dtype):
    step_rows = min(rows_per_seq, 8 * chunk)
    steps = rows_per_seq // step_rows
    kern = functools.partial(_gla_kernel, chunk=chunk, sub=min(chunk, GLA_SUB), n_chunks=step_rows // chunk,
                             n_valid=n_valid)
    return pl.pallas_call(
        kern,
        grid=(nb, steps),
        in_specs=[pl.BlockSpec((step_rows, G_END), lambda b, i: (b * steps + i, 0)),
                  pl.BlockSpec((None, GLA_HEADS, GLA_DK, GLA_DV), lambda b, i: (b, 0, 0, 0)),
                  _const_spec(gnorm.shape)],
        out_specs=[pl.BlockSpec((step_rows, GLA_HEADS * GLA_DV), lambda b, i: (b * steps + i, 0)),
                   pl.BlockSpec((None, GLA_HEADS, GLA_DK, GLA_DV), lambda b, i: (b, 0, 0, 0))],
        out_shape=[jax.ShapeDtypeStruct((nb * rows_per_seq, GLA_HEADS * GLA_DV), out_dtype),
                   jax.ShapeDtypeStruct((nb, GLA_HEADS, GLA_DK, GLA_DV), F32)],
        scratch_shapes=[pltpu.VMEM((GLA_HEADS // 2, 2 * GLA_DK, 2 * GLA_DV), F32)],
        compiler_params=_cparams("parallel", "arbitrary"),
        name="gla",
    )(gla, s0, gnorm)


def _compress_kernel(*refs, n_parts, n_prefetch):
    refs = refs[n_prefetch:]
    x_refs = refs[:n_parts]
    wk_ref, wv_ref, pek_ref, pev_ref, kc_ref, vc_ref, sh_scr = refs[n_parts:]
    x = jnp.concatenate([r[...] for r in x_refs], axis=0) if n_parts > 1 else x_refs[0][...]
    m = x.shape[0]
    row_w = 2 * LANES

    def branch(off, w_ref, pe_ref, out_ref):
        lo = jnp.zeros((m, LANES), F32)
        hi = jnp.zeros((m, LANES), F32)
        for p in range(CMP_STRIDE):
            xp = x[:, p * row_w + off:p * row_w + off + LANES]
            lo = lo + _dot((xp + pe_ref[p:p + 1, :]).astype(BF16), w_ref[p])
            hi = hi + _dot((xp + pe_ref[CMP_STRIDE + p:CMP_STRIDE + p + 1, :]).astype(BF16), w_ref[CMP_STRIDE + p])
        sh_scr[pl.ds(0, m), :] = hi
        sh_scr[pl.ds(m, 8), :] = jnp.zeros((8, LANES), F32)
        out_ref[...] = (lo + sh_scr[pl.ds(1, m), :]).astype(out_ref.dtype)
    branch(0, wk_ref, pek_ref, kc_ref)
    branch(LANES, wv_ref, pev_ref, vc_ref)


def _compress_call(n_parts, m, grid, x_specs, out_map, out_rows, prefetch):
    kern = functools.partial(_compress_kernel, n_parts=n_parts, n_prefetch=prefetch)
    w_shape = (CMP_LEN, LANES, LANES)
    pe_shape = (CMP_LEN, LANES)
    in_specs = list(x_specs) + [_const_spec(w_shape), _const_spec(w_shape), _const_spec(pe_shape), _const_spec(pe_shape)]
    out_specs = [pl.BlockSpec((m, LANES), out_map), pl.BlockSpec((m, LANES), out_map)]
    gs = pltpu.PrefetchScalarGridSpec(num_scalar_prefetch=prefetch, grid=grid, in_specs=in_specs, out_specs=out_specs,
                                      scratch_shapes=[pltpu.VMEM((m + 8, LANES), F32)])
    return pl.pallas_call(
        kern, grid_spec=gs,
        out_shape=[jax.ShapeDtypeStruct((out_rows, LANES), BF16)] * 2,
        compiler_params=_cparams("parallel"),
        name="nsa_compress",
    )


def _compress_prompt(rows_c, nb, seq, wk, wv, pek, pev):
    m = seq // CMP_STRIDE
    x = rows_c.reshape(-1, CMP_STRIDE * 2 * LANES)
    call = _compress_call(1, m, (nb,), [pl.BlockSpec((m, CMP_STRIDE * 2 * LANES), lambda b: (b, 0))],
                          lambda b: (b, 0), nb * m, 0)
    return call(x, wk, wv, pek, pev)


def _compress_pages(pool, page_table, wk, wv, pek, pev):
    nb, n_pages = page_table.shape
    mp = PAGE_SIZE // CMP_STRIDE
    x = pool.reshape(pool.shape[0], mp, CMP_STRIDE * 2 * LANES)
    specs = [pl.BlockSpec((None, mp, CMP_STRIDE * 2 * LANES), functools.partial(lambda pg, b, pt: (pt[b, pg], 0, 0), pg))
             for pg in range(n_pages)]
    m = n_pages * mp
    call = _compress_call(n_pages, m, (nb,), specs, lambda b, pt: (b, 0), nb * m, 1)
    return call(page_table, *([x] * n_pages), wk, wv, pek, pev)


def _topk_mask(imp, qpos, n_slc):
    blk = lax.broadcasted_iota(jnp.int32, imp.shape, 1)
    cur = qpos // SLC_BLOCK
    val = jnp.where((blk == cur) | (blk == 0), BIG, jnp.where(blk <= cur, imp, NEG))
    val = jnp.where(blk < n_slc, val, REMOVED)
    sel = jnp.zeros(imp.shape, jnp.bool_)
    for _ in range(min(SLC_TOPK, n_slc)):
        mx = jnp.max(val, axis=-1, keepdims=True)
        idx = jnp.min(jnp.where(val == mx, blk, imp.shape[1]), axis=-1, keepdims=True)
        pick = blk == idx
        sel = sel | pick
        val = jnp.where(pick, REMOVED, val)
    return sel.astype(BF16)


def _block_expand(n_blocks_pad, k0, n_keys):
    blk = lax.broadcasted_iota(jnp.int32, (n_blocks_pad, n_keys), 0)
    col = lax.broadcasted_iota(jnp.int32, (n_blocks_pad, n_keys), 1)
    return (blk == (k0 + col) // SLC_BLOCK).astype(BF16)


def _flash_init(m_rows):
    return (jnp.full((m_rows, 1), NEG, F32), jnp.zeros((m_rows, 1), F32), jnp.zeros((m_rows, HEAD_DIM), F32))


def _nsa_prompt_kernel(qc_ref, qr_ref, gates_ref, kc_ref, vc_ref, ks_ref, kw_ref, mcs_ref, o_ref,
                       qa_scr, m_scr, l_scr, acc_scr, *, qb, seq, key_tile):
    t0 = pl.program_id(1) * qb
    qpos = t0 + lax.broadcasted_iota(jnp.int32, (qb, 1), 0)
    n_cmp_pad = kc_ref.shape[0]
    n_slc = seq // SLC_BLOCK
    hd = HEAD_DIM

    rows = NSA_HEADS * qb
    lane2 = lax.broadcasted_iota(jnp.int32, (qb, 2 * hd), 1)

    def stack_heads(q_ref):
        parts = []
        for h in range(NSA_HEADS):
            g = h // NSA_GROUP
            pair = q_ref[:, (h // 2) * 2 * hd:(h // 2 + 1) * 2 * hd].astype(F32)
            if h % 2 != g:
                pair = pltpu.roll(pair, hd, axis=1)
            parts.append(jnp.where((lane2 >= g * hd) & (lane2 < (g + 1) * hd), pair, 0.0).astype(BF16))
        return jnp.concatenate(parts, axis=0)

    def row_softmax(s, mask):
        s = jnp.where(mask, s, NEG)
        p = jnp.where(mask, jnp.exp(s - jnp.max(s, axis=-1, keepdims=True)), 0.0)
        return p, jnp.maximum(jnp.sum(p, axis=-1, keepdims=True), TINY)

    q_tok = t0 + lax.broadcasted_iota(jnp.int32, (rows, 1), 0) % qb
    qr_st = stack_heads(qr_ref)

    ncol = lax.broadcasted_iota(jnp.int32, (rows, n_cmp_pad), 1)
    p, l_cmp = row_softmax(_dot_nt(stack_heads(qc_ref), kc_ref[...]), (ncol * CMP_STRIDE + CMP_LEN - 1) <= q_tok)
    p = p * (1.0 / l_cmp)
    o_cmp = _dot(p.astype(BF16), vc_ref[...])
    psum = []
    for g in range(NSA_KV_HEADS):
        blocks = [p[(g * NSA_GROUP + j) * qb:(g * NSA_GROUP + j + 1) * qb] for j in range(NSA_GROUP)]
        psum.append(functools.reduce(lambda a, b: a + b, blocks))
    imp = jnp.dot(jnp.concatenate(psum, axis=0), mcs_ref[...], precision=HIGHEST, preferred_element_type=F32)
    sel = _topk_mask(imp, jnp.concatenate([qpos] * NSA_KV_HEADS, axis=0), n_slc)

    n_blk = sel.shape[1]
    qa_scr[:, 0:2 * hd] = qr_st
    for h in range(NSA_HEADS):
        g = h // NSA_GROUP
        qa_scr[h * qb:(h + 1) * qb, 2 * hd:] = 1.0 - sel[g * qb:(g + 1) * qb]
    m_scr[...] = jnp.full((rows, LANES), NEG, F32)
    l_scr[...] = jnp.zeros((rows, LANES), F32)
    acc_scr[...] = jnp.zeros((rows, 2 * hd), F32)

    def tile(k0, causal):
        kv = ks_ref[pl.ds(k0, key_tile), :]
        key_blk = (k0 + lax.broadcasted_iota(jnp.int32, (key_tile, n_blk), 0)) // SLC_BLOCK
        own_blk = key_blk == lax.broadcasted_iota(jnp.int32, (key_tile, n_blk), 1)
        k_aug = jnp.concatenate([kv[:, 0:2 * hd], jnp.where(own_blk, NEG, 0.0).astype(BF16)], axis=1)
        s = _dot_nt(qa_scr[...], k_aug)
        if causal:
            q_row = t0 + lax.broadcasted_iota(jnp.int32, (rows, key_tile), 0) % qb
            s = jnp.where(k0 + lax.broadcasted_iota(jnp.int32, (rows, key_tile), 1) <= q_row, s, NEG)
        m_old = m_scr[...]
        m_new = jnp.maximum(m_old, jnp.max(s, axis=-1, keepdims=True))
        alpha = jnp.exp(m_old - m_new)
        p = jnp.exp(s - jnp.concatenate([m_new] * (key_tile // LANES), axis=1))
        l_scr[...] = alpha * l_scr[...] + jnp.sum(p, axis=-1, keepdims=True)
        acc_scr[...] = alpha * acc_scr[...] + _dot(p.astype(BF16), kv[:, 2 * hd:4 * hd])
        m_scr[...] = m_new

    n_full = t0 // key_tile

    def full_tile(kt, carry):
        tile(pl.multiple_of(kt * key_tile, key_tile), False)
        return carry
    lax.fori_loop(0, n_full, full_tile, 0)
    tile(pl.multiple_of(n_full * key_tile, key_tile), True)

    wlen = min(seq, NSA_WINDOW + qb)
    start = pl.multiple_of(jnp.clip(t0 - NSA_WINDOW, 0, seq - wlen), qb)
    band = kw_ref[pl.ds(start, wlen), :]
    kpos = start + lax.broadcasted_iota(jnp.int32, (rows, wlen), 1)
    p, l_win = row_softmax(_dot_nt(qr_st, band[:, 0:2 * hd]), (kpos <= q_tok) & (kpos > q_tok - NSA_WINDOW))
    o_win = _dot(p.astype(BF16), band[:, 2 * hd:4 * hd])
    for h in range(NSA_HEADS):
        g = h // NSA_GROUP
        rs, ls = slice(h * qb, (h + 1) * qb), slice(g * hd, (g + 1) * hd)
        o_w = o_win[rs, ls] / l_win[rs]
        o_slc = acc_scr[rs, ls] / jnp.maximum(l_scr[rs, 0:hd], TINY)
        gate = gates_ref[:, 3 * h:3 * h + 3]
        o = gate[:, 0:1] * o_cmp[rs, ls] + gate[:, 1:2] * o_slc + gate[:, 2:3] * o_w
        o_ref[:, h * hd:(h + 1) * hd] = o.astype(o_ref.dtype)


def _nsa_prompt(qc, qr, gates, kc, vc, ks, kw, mcs, *, nb, seq, qb):
    nq = seq // qb
    n_cmp_pad = seq // CMP_STRIDE
    key_tile = min(seq, NSA_KEY_TILE)
    n_blk = _round_up(seq // SLC_BLOCK, LANES)
    kern = functools.partial(_nsa_prompt_kernel, qb=qb, seq=seq, key_tile=key_tile)
    qspec = lambda wd: pl.BlockSpec((qb, wd), lambda b, i: (b * nq + i, 0))
    seqspec = lambda rows, wd: pl.BlockSpec((rows, wd), lambda b, i: (b, 0))
    return pl.pallas_call(
        kern,
        grid=(nb, nq),
        in_specs=[qspec(512), qspec(512), qspec(LANES), seqspec(n_cmp_pad, LANES), seqspec(n_cmp_pad, LANES),
                  seqspec(seq, 256), seqspec(seq, 256), _const_spec(mcs.shape)],
        out_specs=qspec(512),
        out_shape=jax.ShapeDtypeStruct((nb * seq, 512), BF16),
        scratch_shapes=[pltpu.VMEM((NSA_HEADS * qb, 2 * HEAD_DIM + n_blk), BF16),
                        pltpu.VMEM((NSA_HEADS * qb, LANES), F32), pltpu.VMEM((NSA_HEADS * qb, LANES), F32),
                        pltpu.VMEM((NSA_HEADS * qb, 2 * HEAD_DIM), F32)],
        compiler_params=_cparams("parallel", "arbitrary"),
        name="nsa_prompt",
    )(qc, qr, gates, kc, vc, ks, kw, mcs)


def _attend_transposed(q_bf, q_f32, kt, vt, mask, newk, newv, mask_new, n_tok):
    m_rows = q_bf.shape[0]
    s = jnp.where(mask, _dot(q_bf, kt), NEG)
    lane = lax.broadcasted_iota(jnp.int32, (m_rows, TOK_PAD), 1)
    sn = jnp.full((m_rows, TOK_PAD), NEG, F32)
    for t in range(n_tok):
        sn = jnp.where(lane == t, jnp.sum(q_f32 * newk[t:t + 1], axis=-1, keepdims=True), sn)
    sn = jnp.where(mask_new, sn, NEG)
    m = jnp.maximum(jnp.max(s, axis=-1, keepdims=True), jnp.max(sn, axis=-1, keepdims=True))
    p = jnp.where(mask, jnp.exp(s - m), 0.0)
    pn = jnp.where(mask_new, jnp.exp(sn - m), 0.0)
    l = jnp.sum(p, axis=-1, keepdims=True) + jnp.sum(pn, axis=-1, keepdims=True)
    o = _dot_nt(p.astype(BF16), vt)
    for t in range(n_tok):
        o = o + pn[:, t:t + 1] * newv[t:t + 1]
    return o / jnp.maximum(l, TINY)


def _nsa_sample_kernel(*refs, n_pages, n_tok):
    (qc_ref, qr_ref, gates_ref, kc_ref, vc_ref) = refs[1:6]
    page_refs = refs[6:6 + n_pages]
    news_ref, win_ref, neww_ref, newwt_ref, mcs_ref, amat_ref, o_ref, owin_ref = refs[6 + n_pages:]
    hd = HEAD_DIM
    past = n_pages * PAGE_SIZE
    win = win_ref.shape[-1]
    m_rows = qc_ref.shape[1]
    tok = lax.broadcasted_iota(jnp.int32, (m_rows, 1), 0) % n_tok
    qpos = past + tok
    n_cmp_pad = kc_ref.shape[0]
    n_slc = (past + n_tok + SLC_BLOCK - 1) // SLC_BLOCK

    ncol = lax.broadcasted_iota(jnp.int32, (m_rows, n_cmp_pad), 1)
    cmask = ((ncol * CMP_STRIDE + CMP_LEN - 1) <= qpos) & (ncol < n_cmp_pad - 1)
    new_pos = past + lax.broadcasted_iota(jnp.int32, (m_rows, TOK_PAD), 1)
    new_ok = (new_pos <= qpos) & (new_pos < past + n_tok)
    wpos = past - win + lax.broadcasted_iota(jnp.int32, (m_rows, win), 1)
    wmask = (wpos <= qpos) & (wpos > qpos - NSA_WINDOW) & (wpos >= 0)
    wmask_new = new_ok & (new_pos > qpos - NSA_WINDOW)

    o_cmp, imps = [], []
    for g in range(NSA_KV_HEADS):
        ksl = slice(g * hd, (g + 1) * hd)
        p, l = _softmax_full(_dot_nt(qc_ref[g], kc_ref[:, ksl]), cmask)
        p = p / jnp.maximum(l, TINY)
        o_cmp.append(_dot(p.astype(BF16), vc_ref[:, ksl]))
        psum = jnp.dot(amat_ref[...], p, precision=HIGHEST, preferred_element_type=F32)
        imps.append(jnp.dot(psum, mcs_ref[...], precision=HIGHEST, preferred_element_type=F32))
    sel_all = _topk_mask(jnp.concatenate(imps, axis=0), jnp.concatenate([qpos] * NSA_KV_HEADS, axis=0), n_slc)
    expand = _block_expand(LANES, 0, past)
    new_blk = lax.broadcasted_iota(jnp.int32, (m_rows, LANES), 1) == past // SLC_BLOCK

    for g in range(NSA_KV_HEADS):
        ksl = slice(g * hd, (g + 1) * hd)
        vsl = slice(LANES + g * hd, LANES + (g + 1) * hd)
        qr = qr_ref[g]
        qr32 = qr.astype(F32)
        sel = sel_all[g * m_rows:(g + 1) * m_rows]
        sel_new = jnp.sum(jnp.where(new_blk, sel.astype(F32), 0.0), axis=-1, keepdims=True) > 0.5
        kt = jnp.concatenate([r[0, g].astype(BF16) for r in page_refs], axis=1)
        vt = jnp.concatenate([r[1, g].astype(BF16) for r in page_refs], axis=1)
        o_slc = _attend_transposed(qr, qr32, kt, vt, _dot(sel, expand) > 0.5,
                                   news_ref[:, ksl], news_ref[:, vsl], new_ok & sel_new, n_tok)
        o_win = _attend_transposed(qr, qr32, win_ref[0, g].astype(BF16), win_ref[1, g].astype(BF16), wmask,
                                   neww_ref[:, ksl], neww_ref[:, vsl], wmask_new, n_tok)
        gate = gates_ref[g]
        o_ref[g] = gate[:, 0:1] * o_cmp[g] + gate[:, 1:2] * o_slc + gate[:, 2:3] * o_win

        lane = lax.broadcasted_iota(jnp.int32, (hd, LANES), 1)
        for kv in range(2):
            shifted = pltpu.roll(win_ref[kv, g], win - n_tok, axis=1)
            last = shifted[:, win - LANES:]
            for t in range(n_tok):
                last = jnp.where(lane == LANES - n_tok + t, newwt_ref[kv, g, :, t:t + 1], last)
            if win > LANES:
                owin_ref[kv, g, :, 0:win - LANES] = shifted[:, 0:win - LANES]
            owin_ref[kv, g, :, win - LANES:] = last


def _nsa_sample(page_table, qc, qr, gates, kc, vc, pool_s, news, win_t, neww, neww_t, mcs, amat, *, n_tok):
    nb, n_pages = page_table.shape
    m_rows = qc.shape[2]
    kern = functools.partial(_nsa_sample_kernel, n_pages=n_pages, n_tok=n_tok)
    per_b = lambda a: pl.BlockSpec((None,) + a.shape[1:], lambda b, pt: (b,) + (0,) * (a.ndim - 1))
    n_cmp_pad = kc.shape[0] // nb
    cmp_spec = pl.BlockSpec((n_cmp_pad, LANES), lambda b, pt: (b, 0))
    page_specs = [pl.BlockSpec((None,) + pool_s.shape[1:], functools.partial(lambda pg, b, pt: (pt[b, pg], 0, 0, 0, 0), pg))
                  for pg in range(n_pages)]
    gs = pltpu.PrefetchScalarGridSpec(
        num_scalar_prefetch=1, grid=(nb,),
        in_specs=[per_b(qc), per_b(qr), per_b(gates), cmp_spec, cmp_spec] + page_specs +
                 [per_b(news), per_b(win_t), per_b(neww), per_b(neww_t),
                  pl.BlockSpec(mcs.shape, lambda b, pt: (0, 0)), pl.BlockSpec(amat.shape, lambda b, pt: (0, 0))],
        out_specs=[per_b(qc), per_b(win_t)])
    return pl.pallas_call(
        kern, grid_spec=gs,
        out_shape=[jax.ShapeDtypeStruct((nb, NSA_KV_HEADS, m_rows, HEAD_DIM), F32),
                   jax.ShapeDtypeStruct(win_t.shape, F32)],
        compiler_params=_cparams("parallel"),
        name="nsa_sample",
    )(page_table, qc, qr, gates, kc, vc, *([pool_s] * n_pages), news, win_t, neww, neww_t, mcs, amat)


C_GROUP_COLS = 3 * DIL_HEADS * HEAD_DIM


def _proj_c_kernel(x_ref, gmix_ref, w_ref, gq_ref, gk_ref, cs_ref, sn_ref, *out_refs):
    q_refs, r_refs = out_refs[:N_DIL], out_refs[N_DIL:]
    hn = _rms_rows(x_ref[...], gmix_ref[...]).astype(BF16)
    cs, sn = cs_ref[...], sn_ref[...]
    wd = DIL_HEADS * HEAD_DIM
    for g in range(N_DIL):
        z = _dot(hn, w_ref[:, g * C_GROUP_COLS:(g + 1) * C_GROUP_COLS])
        q = _rope(_rms_heads(z[:, 0:wd], gq_ref[g:g + 1, :]), cs, sn)
        q_refs[g][...] = (q * QSCALE).astype(BF16)
        r_refs[g][:, 0:wd] = _rope(_rms_heads(z[:, wd:2 * wd], gk_ref[g:g + 1, :]), cs, sn)
        r_refs[g][:, wd:] = z[:, 2 * wd:]


def _proj_c(x, gmix, w, gq, gk, cs, sn, tm):
    n = x.shape[0]
    tiles_per_seq = cs.shape[0] // tm
    row = lambda width: pl.BlockSpec((tm, width), lambda i: (i, 0))
    tab = pl.BlockSpec((tm, LANES), lambda i: (i % tiles_per_seq, 0))
    wd = DIL_HEADS * HEAD_DIM
    return pl.pallas_call(
        _proj_c_kernel,
        grid=(n // tm,),
        in_specs=[row(D_MODEL), _const_spec(gmix.shape), _const_spec(w.shape), _const_spec(gq.shape),
                  _const_spec(gk.shape), tab, tab],
        out_specs=[row(wd)] * N_DIL + [row(2 * wd)] * N_DIL,
        out_shape=[jax.ShapeDtypeStruct((n, wd), BF16)] * N_DIL + [jax.ShapeDtypeStruct((n, 2 * wd), F32)] * N_DIL,
        compiler_params=_cparams("parallel"),
        name="proj_c",
    )(x, gmix, w, gq, gk, cs, sn)


def _dil_prompt_kernel(q_ref, kp_ref, vp_ref, kc_ref, vc_ref, o_ref, ml_ref, *, tq):
    i = pl.program_id(2)
    row = lax.broadcasted_iota(jnp.int32, (tq, 2 * tq), 0)
    col = lax.broadcasted_iota(jnp.int32, (tq, 2 * tq), 1)
    delta = tq + row - col
    mask = (delta >= 0) & (delta < DIL_KEYS) & ((col >= tq) | (i > 0))
    k = jnp.concatenate([kp_ref[...], kc_ref[...]], axis=0).astype(BF16)
    v = jnp.concatenate([vp_ref[...], vc_ref[...]], axis=0).astype(BF16)
    lane = lax.broadcasted_iota(jnp.int32, (tq, LANES), 1)
    ml = jnp.zeros((tq, LANES), F32)
    hd = HEAD_DIM
    for h in range(DIL_HEADS):
        hs = slice(h * hd, (h + 1) * hd)
        s = jnp.where(mask, _dot_nt(q_ref[:, hs], k[:, hs]), NEG)
        m = jnp.max(s, axis=-1, keepdims=True)
        p = jnp.where(mask, jnp.exp(s - m), 0.0)
        l = jnp.sum(p, axis=-1, keepdims=True)
        o_ref[:, hs] = _dot(p.astype(BF16), v[:, hs])
        ml = jnp.where(lane == h, m, jnp.where(lane == DIL_HEADS + h, l, ml))
    ml_ref[...] = ml


def _dil_prompt(q, rows, *, nb, seq, dil, tq):
    wd = DIL_HEADS * HEAD_DIM
    assert tq >= DIL_KEYS - 1 and seq % (dil * tq) == 0
    nu = seq // dil // tq
    qv = q.reshape(nb * seq // dil, dil * wd)
    rv = rows.reshape(nb * seq // dil, dil * 2 * wd)
    cur = lambda off: (lambda b, r, i: (b * nu + i, 2 * r + off))
    prev = lambda off: (lambda b, r, i: (b * nu + jnp.maximum(i - 1, 0), 2 * r + off))
    blk = lambda imap: pl.BlockSpec((tq, wd), imap)
    o, ml = pl.pallas_call(
        functools.partial(_dil_prompt_kernel, tq=tq),
        grid=(nb, dil, nu),
        in_specs=[blk(lambda b, r, i: (b * nu + i, r)), blk(prev(0)), blk(prev(1)), blk(cur(0)), blk(cur(1))],
        out_specs=[blk(lambda b, r, i: (b * nu + i, r)), pl.BlockSpec((tq, LANES), lambda b, r, i: (b * nu + i, r))],
        out_shape=[jax.ShapeDtypeStruct(qv.shape, F32), jax.ShapeDtypeStruct((qv.shape[0], dil * LANES), F32)],
        compiler_params=_cparams("parallel", "parallel", "arbitrary"),
        name="dil_prompt",
    )(qv, rv, rv, rv, rv)
    return o.reshape(nb * seq, wd), ml.reshape(nb * seq, LANES)


DIL_SAMPLE_HEADS = 4


def _dil_sample_kernel(q_ref, n0_ref, n1_ref, n2_ref, c0_ref, c1_ref, c2_ref, o_ref, oc0_ref, oc1_ref, oc2_ref,
                       *, n_tok):
    n_refs, c_refs, oc_refs = (n0_ref, n1_ref, n2_ref), (c0_ref, c1_ref, c2_ref), (oc0_ref, oc1_ref, oc2_ref)
    tp = lax.broadcasted_iota(jnp.int32, (1, TOK_PAD), 1)
    lane_t = lax.broadcasted_iota(jnp.int32, (HEAD_DIM, TOK_PAD), 1)
    for hl in range(q_ref.shape[1]):
        stats = [[None] * N_DIL for _ in range(n_tok)]
        for g, (w, dil) in enumerate(DIL_PAIRS):
            nk = n_refs[g][0, hl]
            nv = n_refs[g][1, hl]
            row = lax.broadcasted_iota(jnp.int32, (1, w), 1)
            n_ch = w // LANES
            q_cols = [q_ref[g, hl, :, t:t + 1] for t in range(n_tok)]
            q_wide = [jnp.broadcast_to(qc, (HEAD_DIM, LANES)) for qc in q_cols]
            shared = dil >= n_tok
            groups = [list(range(n_tok))] if shared else [[t] for t in range(n_tok)]
            lane_tok = lax.broadcasted_iota(jnp.int32, (HEAD_DIM, LANES), 1) & (dil - 1)
            row_tok = row & (dil - 1)
            for toks in groups:
                if shared:
                    q_sel = jnp.zeros((HEAD_DIM, LANES), F32)
                    for t in toks:
                        q_sel = jnp.where(lane_tok == t, q_wide[t], q_sel)
                else:
                    q_sel = q_wide[toks[0]]
                s_parts = [jnp.sum(c_refs[g][0, hl, :, c * LANES:(c + 1) * LANES] * q_sel, axis=0, keepdims=True)
                           for c in range(n_ch)]
                s_all = jnp.concatenate(s_parts, axis=1) if n_ch > 1 else s_parts[0]
                m_lanes = jnp.zeros((1, w), F32)
                any_valid = jnp.zeros((1, w), jnp.bool_)
                valids, pnews = [], []
                for t in toks:
                    valid = (row_tok == (t & (dil - 1))) & (w + t - row <= (DIL_KEYS - 1) * dil)
                    dn = t - tp
                    valid_n = (dn >= 0) & ((dn & (dil - 1)) == 0)
                    sn = jnp.where(valid_n, jnp.sum(nk * q_cols[t], axis=0, keepdims=True), NEG)
                    m = jnp.maximum(jnp.max(jnp.where(valid, s_all, NEG), axis=-1, keepdims=True),
                                    jnp.max(sn, axis=-1, keepdims=True))
                    m_lanes = jnp.where(valid, m, m_lanes)
                    any_valid = any_valid | valid
                    valids.append(valid)
                    pnews.append(jnp.where(valid_n, jnp.exp(sn - m), 0.0))
                    stats[t][g] = (None, m, None)
                p_all = jnp.where(any_valid, jnp.exp(s_all - m_lanes), 0.0)
                acc = jnp.zeros((HEAD_DIM, LANES), F32)
                for c in range(n_ch):
                    acc = acc + c_refs[g][1, hl, :, c * LANES:(c + 1) * LANES] * p_all[:, c * LANES:(c + 1) * LANES]
                for t, valid, pn in zip(toks, valids, pnews):
                    l = jnp.sum(jnp.where(valid, p_all, 0.0), axis=-1, keepdims=True) + jnp.sum(pn, axis=-1, keepdims=True)
                    own = acc if not shared else jnp.where(lane_tok == t, acc, 0.0)
                    o = jnp.sum(own, axis=-1, keepdims=True) + jnp.sum(nv * pn, axis=-1, keepdims=True)
                    stats[t][g] = (o, stats[t][g][1], l)
            lane = lax.broadcasted_iota(jnp.int32, (HEAD_DIM, LANES), 1)
            for kv, new in ((0, nk), (1, nv)):
                shifted = pltpu.roll(c_refs[g][kv, hl], w - n_tok, axis=1)
                last = shifted[:, w - LANES:]
                for t in range(n_tok):
                    last = jnp.where(lane == LANES - n_tok + t, new[:, t:t + 1], last)
                if w > LANES:
                    oc_refs[g][kv, hl, :, 0:w - LANES] = shifted[:, 0:w - LANES]
                oc_refs[g][kv, hl, :, w - LANES:] = last
        out = jnp.zeros((HEAD_DIM, TOK_PAD), F32)
        for t in range(n_tok):
            parts = stats[t]
            mx = functools.reduce(jnp.maximum, [m for _, m, _ in parts])
            num = sum(jnp.exp(m - mx) * o for o, m, _ in parts)
            den = sum(jnp.exp(m - mx) * l for _, m, l in parts)
            out = jnp.where(lane_t == t, num / den, out)
        o_ref[hl] = out


def _dil_sample(q, news, caches, *, n_tok):
    nb = q.shape[0]
    hs = DIL_SAMPLE_HEADS
    for g, (w, d) in enumerate(DIL_PAIRS):
        assert caches[g].shape[-1] == w and w // d == DIL_KEYS - 1
        assert d & (d - 1) == 0
    blk = lambda a: pl.BlockSpec((None, a.shape[1], hs) + a.shape[3:], lambda b, hh: (b, 0, hh, 0, 0))
    o_spec = pl.BlockSpec((None, hs, HEAD_DIM, TOK_PAD), lambda b, hh: (b, hh, 0, 0))
    return pl.pallas_call(
        functools.partial(_dil_sample_kernel, n_tok=n_tok),
        grid=(nb, DIL_HEADS // hs),
        in_specs=[blk(q)] + [blk(a) for a in news] + [blk(a) for a in caches],
        out_specs=[o_spec] + [blk(a) for a in caches],
        out_shape=[jax.ShapeDtypeStruct((nb, DIL_HEADS, HEAD_DIM, TOK_PAD), F32)]
                  + [jax.ShapeDtypeStruct(a.shape, F32) for a in caches],
        compiler_params=_cparams("parallel", "parallel"),
        name="dil_sample",
    )(q, *news, *caches)


FF_CHUNK = 1024


def _post_kernel(*refs, n_cat, combine):
    h_ref = refs[0]
    mix_refs = refs[1:1 + (2 * N_DIL if combine else n_cat)]
    (wo_ref, gmlp_ref, w1_ref, w2_ref, gple_ref, wg_ref, p_ref, wp_ref, out_ref) = refs[1 + len(mix_refs):1 + len(mix_refs) + 9]
    hd = HEAD_DIM
    if combine:
        comb_scr = refs[-1]
        o_refs, ml_refs = mix_refs[:N_DIL], mix_refs[N_DIL:]
        for h in range(DIL_HEADS):
            ms = [r[:, h:h + 1] for r in ml_refs]
            ls = [r[:, DIL_HEADS + h:DIL_HEADS + h + 1] for r in ml_refs]
            mx = functools.reduce(jnp.maximum, ms)
            ws = [jnp.exp(m - mx) for m in ms]
            num = sum(w * r[:, h * hd:(h + 1) * hd] for w, r in zip(ws, o_refs))
            den = sum(w * l for w, l in zip(ws, ls))
            comb_scr[:, h * hd:(h + 1) * hd] = (num / den).astype(BF16)
        y = _dot(comb_scr[...], wo_ref[...])
    else:
        y = None
        for k, r in enumerate(mix_refs):
            part = _dot(r[...].astype(BF16), wo_ref[k * r.shape[1]:(k + 1) * r.shape[1], :])
            y = part if y is None else y + part
    h1 = h_ref[...] + y
    hn = _rms_rows(h1, gmlp_ref[...]).astype(BF16)
    acc = jnp.zeros(h1.shape, F32)
    for c in range(D_FF // FF_CHUNK):
        u = _dot(hn, w1_ref[:, c * FF_CHUNK:(c + 1) * FF_CHUNK])
        acc = acc + _dot(jnp.square(jnp.maximum(u, 0.0)).astype(BF16), w2_ref[c * FF_CHUNK:(c + 1) * FF_CHUNK, :])
    h2 = h1 + acc
    gate = jax.nn.sigmoid(_dot(_rms_rows(h2, gple_ref[...]).astype(BF16), wg_ref[...]))
    out_ref[...] = h2 + gate * _dot(p_ref[...].astype(BF16), wp_ref[...])


def _post(h, mix, wo, gmlp, w1, w2, gple, wg, p, wp, *, combine, tm):
    n = h.shape[0]
    row = lambda a: pl.BlockSpec((tm, a.shape[1]), lambda i: (i, 0))
    weights = (wo, gmlp, w1, w2, gple, wg)
    wspec = lambda a: pl.BlockSpec(a.shape, lambda i: (0, 0), pipeline_mode=pl.Buffered(1))
    kern = functools.partial(_post_kernel, n_cat=len(mix), combine=combine)
    return pl.pallas_call(
        kern,
        grid=(n // tm,),
        in_specs=[row(h)] + [row(a) for a in mix] + [wspec(a) for a in weights] + [row(p), wspec(wp)],
        out_specs=row(h),
        out_shape=jax.ShapeDtypeStruct(h.shape, F32),
        scratch_shapes=[pltpu.VMEM((tm, DIL_HEADS * HEAD_DIM), BF16)] if combine else [],
        compiler_params=_cparams("parallel"),
        name="post",
    )(h, *mix, *weights, p, wp)


def _rope_tables(pos):
    half = HEAD_DIM // 2
    freq = ROPE_THETA ** (-jnp.arange(half, dtype=F32) / half)
    ang = pos.astype(F32)[:, None] * freq[None, :]
    cos, sin = jnp.cos(ang), jnp.sin(ang)
    return jnp.tile(jnp.concatenate([cos, cos], axis=1), (1, 2)), jnp.tile(jnp.concatenate([-sin, sin], axis=1), (1, 2))


def _cmp_to_slc(n_cmp, n_slc, rows_pad, cols_pad):
    cs = jnp.arange(n_cmp)[:, None] * CMP_STRIDE
    ss = jnp.arange(n_slc)[None, :] * SLC_BLOCK
    ov = jnp.maximum(jnp.minimum(cs + CMP_LEN, ss + SLC_BLOCK) - jnp.maximum(cs, ss), 0)
    return jnp.pad(ov.astype(F32) / CMP_LEN, ((0, rows_pad - n_cmp), (0, cols_pad - n_slc)))


def _round_up(x, m):
    return (x + m - 1) // m * m


def _block_diag2(w):
    z = jnp.zeros_like(w)
    return jnp.concatenate([jnp.concatenate([w, z], axis=2), jnp.concatenate([z, w], axis=2)], axis=1)


def kernel(x_prompt, x_sample, state_gla, cache_nsa_cmp, cache_nsa_slc, cache_nsa_win, cache_dil_0, cache_dil_1,
           cache_dil_2, page_table, p_prompt, p_sample, norm_mix, norm_mlp, norm_ple, a_w_in, a_w_out, gla_w_a2,
           gla_b_a, gla_g_norm, nsa_g_qk, nsa_w_phi, nsa_pe, c_w_in, c_g_qk, c_w_out, mlp_w1, mlp_w2, ple_w_proj,
           ple_w_gate):
    nb_p, seq, _ = x_prompt.shape
    nb_s, n_tok, _ = x_sample.shape
    depth = norm_mix.shape[0]
    past = page_table.shape[1] * PAGE_SIZE
    assert n_tok < CMP_STRIDE and n_tok <= 8 and seq % 512 == 0
    tm_p = 256
    tm_s = min(256, nb_s * n_tok)
    cs_p, sn_p = _rope_tables(jnp.arange(seq))
    cs_s, sn_s = _rope_tables(past + jnp.arange(nb_s * n_tok) % n_tok)
    hp = x_prompt.reshape(nb_p * seq, D_MODEL)
    hs = x_sample.reshape(nb_s * n_tok, D_MODEL)
    dil_caches = (cache_dil_0, cache_dil_1, cache_dil_2)
    outs = {k: [] for k in ("gla_p", "gla_s", "cmp_p", "cmp_s", "slc_p", "slc_s", "win_p", "win_s")}
    dil_p = [[] for _ in DIL_PAIRS]
    dil_s = [[] for _ in DIL_PAIRS]
    kvrow = (2, NSA_KV_HEADS, HEAD_DIM)
    wd = DIL_HEADS * HEAD_DIM

    for i in range(depth):
        j = i // 2
        gmix = norm_mix[i][None, :]
        if i % 2 == 0:
            w_in = a_w_in[j]
            pad_cols = lambda a, width: jnp.pad(a, ((0, 0), (0, width - a.shape[1])))
            w_pad = jnp.concatenate([w_in[:, :1536], pad_cols(w_in[:, 1536:1552], LANES), w_in[:, 1552:2832],
                                     pad_cols(w_in[:, 2832:], LANES)], axis=1).astype(BF16)
            wa2 = jnp.pad(gla_w_a2[j], ((0, LANES - GLA_LOWRANK), (0, 0))).astype(BF16)
            ba = gla_b_a[j][None, :]
            gqk = jnp.tile(nsa_g_qk[j], (1, NSA_HEADS))
            gnorm = gla_g_norm[j][None, :]
            wk, wv = (_block_diag2(nsa_w_phi[j, c]).astype(BF16) for c in range(2))
            pek, pev = (jnp.tile(nsa_pe[j, c], (1, 2)) for c in range(2))
            wo = a_w_out[j].astype(BF16)

            gla, qc, qr, gates, rc, rs, rw, sbf, wbf = _proj_a(hp, gmix, w_pad, wa2, ba, gqk, cs_p, sn_p, tm_p)
            s0 = jnp.zeros((nb_p, GLA_HEADS, GLA_DK, GLA_DV), F32)
            chunk = min(GLA_CHUNK, seq)
            o_gla, s_p = _gla(gla, s0, gnorm, nb=nb_p, rows_per_seq=seq, chunk=chunk, n_valid=chunk, out_dtype=BF16)
            kc, vc = _compress_prompt(rc, nb_p, seq, wk, wv, pek, pev)
            n_cmp_pad = seq // CMP_STRIDE
            mcs = _cmp_to_slc(n_cmp_pad - 1, seq // SLC_BLOCK, n_cmp_pad, _round_up(seq // SLC_BLOCK, LANES))
            o_nsa = _nsa_prompt(qc, qr, gates, kc, vc, sbf, wbf, mcs, nb=nb_p, seq=seq, qb=NSA_Q_BLOCK)
            mix_p = (o_gla, o_nsa)
            outs["gla_p"].append(s_p)
            outs["cmp_p"].append(rc.reshape((nb_p, seq) + kvrow))
            outs["slc_p"].append(rs.reshape((nb_p, seq) + kvrow))
            outs["win_p"].append(rw.reshape((nb_p, seq) + kvrow)[:, seq - min(NSA_WINDOW, seq):])

            gla, qc, qr, gates, rc, rs, rw, _, _ = _proj_a(hs, gmix, w_pad, wa2, ba, gqk, cs_s, sn_s, tm_s)
            gla8 = jnp.pad(gla.reshape(nb_s, n_tok, G_END), ((0, 0), (0, 8 - n_tok), (0, 0))).reshape(nb_s * 8, G_END)
            o_gla8, s_s = _gla(gla8, state_gla[j], gnorm, nb=nb_s, rows_per_seq=8, chunk=8, n_valid=n_tok, out_dtype=F32)
            o_gla = o_gla8.reshape(nb_s, 8, -1)[:, :n_tok].reshape(nb_s * n_tok, -1)
            kc, vc = _compress_pages(cache_nsa_cmp[j].reshape(-1, PAGE_SIZE, 256), page_table, wk, wv, pek, pev)
            n_cmp_pad = past // CMP_STRIDE
            n_slc = (past + n_tok + SLC_BLOCK - 1) // SLC_BLOCK
            mcs = _cmp_to_slc(n_cmp_pad - 1, n_slc, n_cmp_pad, LANES)
            m_rows = NSA_GROUP * n_tok
            regroup = lambda a: a.reshape(nb_s, n_tok, NSA_KV_HEADS, NSA_GROUP, -1).transpose(0, 2, 3, 1, 4).reshape(
                nb_s, NSA_KV_HEADS, m_rows, -1)
            g16 = jnp.pad(regroup(gates[:, :3 * NSA_HEADS]), ((0, 0), (0, 0), (0, 0), (0, LANES - 3)))
            tok_id = jnp.arange(m_rows) % n_tok
            amat = (tok_id[:, None] == tok_id[None, :]).astype(F32)
            pad_rows = lambda a: jnp.pad(a.reshape(nb_s, n_tok, 256), ((0, 0), (0, TOK_PAD - n_tok), (0, 0)))
            keyt = lambda a: a.transpose(0, 2, 3, 4, 1)
            neww_t = jnp.pad(keyt(rw.reshape((nb_s, n_tok) + kvrow)), [(0, 0)] * 4 + [(0, TOK_PAD - n_tok)])
            o16, win_new = _nsa_sample(page_table, regroup(qc), regroup(qr), g16, kc, vc, keyt(cache_nsa_slc[j]),
                                       pad_rows(rs), keyt(cache_nsa_win[j]), pad_rows(rw), neww_t, mcs, amat,
                                       n_tok=n_tok)
            o_nsa = o16.reshape(nb_s, NSA_KV_HEADS, NSA_GROUP, n_tok, HEAD_DIM).transpose(0, 3, 1, 2, 4).reshape(
                nb_s * n_tok, NSA_HEADS * HEAD_DIM)
            mix_s = (o_gla, o_nsa)
            outs["gla_s"].append(s_s)
            outs["cmp_s"].append(rc.reshape((nb_s, n_tok) + kvrow))
            outs["slc_s"].append(rs.reshape((nb_s, n_tok) + kvrow))
            outs["win_s"].append(win_new.transpose(0, 4, 1, 2, 3))
            combine = False
        else:
            w_c = c_w_in[j].astype(BF16)
            gq = jnp.tile(c_g_qk[j][:, 0], (1, DIL_HEADS))
            gk = jnp.tile(c_g_qk[j][:, 1], (1, DIL_HEADS))
            wo = c_w_out[j].astype(BF16)
            dilrow = (2, DIL_HEADS, HEAD_DIM)

            res = _proj_c(hp, gmix, w_c, gq, gk, cs_p, sn_p, tm_p)
            qs, rows = res[:N_DIL], res[N_DIL:]
            o_parts, ml_parts = [], []
            for g, (w, d) in enumerate(DIL_PAIRS):
                o_g, ml_g = _dil_prompt(qs[g], rows[g], nb=nb_p, seq=seq, dil=d, tq=128)
                o_parts.append(o_g)
                ml_parts.append(ml_g)
                dil_p[g].append(rows[g].reshape((nb_p, seq) + dilrow)[:, seq - min(w, seq):])
            mix_p = tuple(o_parts) + tuple(ml_parts)

            res = _proj_c(hs, gmix, w_c, gq, gk, cs_s, sn_s, tm_s)
            qs, rows = res[:N_DIL], res[N_DIL:]
            pad_tok = lambda a: jnp.pad(a, [(0, 0)] * (a.ndim - 1) + [(0, TOK_PAD - n_tok)])
            q_t = pad_tok(jnp.stack([q.astype(F32).reshape(nb_s, n_tok, DIL_HEADS, HEAD_DIM).transpose(0, 2, 3, 1)
                                     for q in qs], axis=1))
            news_t = [pad_tok(r.reshape((nb_s, n_tok) + dilrow).transpose(0, 2, 3, 4, 1)) for r in rows]
            caches_t = [c[j].transpose(0, 2, 3, 4, 1) for c in dil_caches]
            o_t, *new_caches = _dil_sample(q_t, news_t, caches_t, n_tok=n_tok)
            mix_s = (o_t[..., :n_tok].transpose(0, 3, 1, 2).reshape(nb_s * n_tok, wd),)
            for g in range(N_DIL):
                dil_s[g].append(new_caches[g].transpose(0, 4, 1, 2, 3))
            combine = True

        lw = (norm_mlp[i][None, :], mlp_w1[i].astype(BF16), mlp_w2[i].astype(BF16), norm_ple[i][None, :],
              ple_w_gate[i].astype(BF16))
        wp = ple_w_proj[i].astype(BF16)
        hp = _post(hp, mix_p, wo, *lw, p_prompt[i].reshape(nb_p * seq, PLE_DIM), wp, combine=combine, tm=tm_p)
        hs = _post(hs, mix_s, wo, *lw, p_sample[i].reshape(nb_s * n_tok, PLE_DIM), wp, combine=False, tm=tm_s)

    st = jnp.stack
    return (hp.reshape(x_prompt.shape), hs.reshape(x_sample.shape),
            st(outs["gla_p"]), st(outs["gla_s"]), st(outs["cmp_p"]), st(outs["cmp_s"]),
            st(outs["slc_p"]), st(outs["slc_s"]), st(outs["win_p"]), st(outs["win_s"]),
            st(dil_p[0]), st(dil_s[0]), st(dil_p[1]), st(dil_s[1]), st(dil_p[2]), st(dil_s[2]))
```

```python
import functools

import jax
import jax.numpy as jnp
from jax import lax
from jax.experimental import pallas as pl
from jax.experimental.pallas import tpu as pltpu

F32 = jnp.float32
BF16 = jnp.bfloat16
HIGHEST = lax.Precision.HIGHEST

D_MODEL = 1024
PAGE_SIZE = 128
HEAD_DIM = 64
GLA_HEADS = 4
GLA_DK = 64
GLA_DV = 128
GLA_LOWRANK = 16
GLA_TAU = 16.0
GLA_CHUNK = 64
GLA_SUB = 16
NSA_HEADS = 8
NSA_KV_HEADS = 2
NSA_GROUP = NSA_HEADS // NSA_KV_HEADS
CMP_LEN = 32
CMP_STRIDE = 16
SLC_BLOCK = 64
SLC_TOPK = 16
NSA_WINDOW = 512
DIL_PAIRS = ((128, 1), (512, 4), (2048, 16))
N_DIL = 3
DIL_HEADS = 8
DIL_KEYS = 129
D_FF = 4 * D_MODEL
PLE_DIM = 256
ROPE_THETA = 10000.0
EPS = 1e-6
NEG = -1e30
BIG = 1e30
TINY = 1e-20
REMOVED = -3e38

LANES = 128
TOK_PAD = 8
NSA_KEY_TILE = 512
NSA_Q_BLOCK = 256
VMEM_LIMIT = 56 * 1024 * 1024
QSCALE = HEAD_DIM ** -0.5

A_GQ, A_GK, A_GV, A_GR, A_GA, A_NQ, A_NKV, A_NG, A_END = 0, 256, 512, 1024, 1536, 1664, 2176, 2944, 3072
G_Q, G_K, G_V, G_R, G_LG, G_END = 0, 256, 512, 1024, 1536, 1792


def _cparams(*sem):
    return pltpu.CompilerParams(dimension_semantics=sem, vmem_limit_bytes=VMEM_LIMIT)


def _rms_rows(x, g):
    return x * lax.rsqrt(jnp.mean(x * x, axis=-1, keepdims=True) + EPS) * g


def _cols(x):
    return [x[:, c * LANES:(c + 1) * LANES] for c in range(x.shape[1] // LANES)]


def _rms_heads(x, g):
    out = []
    for c, xc in enumerate(_cols(x)):
        low = lax.broadcasted_iota(jnp.int32, xc.shape, 1) < HEAD_DIM
        sq = xc * xc
        s_lo = jnp.sum(jnp.where(low, sq, 0.0), axis=-1, keepdims=True)
        s_hi = jnp.sum(jnp.where(low, 0.0, sq), axis=-1, keepdims=True)
        ms = jnp.where(low, s_lo, s_hi) * (1.0 / HEAD_DIM)
        out.append(xc * lax.rsqrt(ms + EPS) * g[:, c * LANES:(c + 1) * LANES])
    return jnp.concatenate(out, axis=1) if len(out) > 1 else out[0]


def _rope(x, cs, sn):
    out = []
    for xc in _cols(x):
        lane = lax.broadcasted_iota(jnp.int32, xc.shape, 1)
        swapped = jnp.where((lane & 32) == 0, pltpu.roll(xc, LANES - 32, axis=1), pltpu.roll(xc, 32, axis=1))
        out.append(xc * cs + swapped * sn)
    return jnp.concatenate(out, axis=1) if len(out) > 1 else out[0]


def _dot(a, b):
    return jnp.dot(a, b, preferred_element_type=F32)


def _dot_nt(a, b):
    return lax.dot_general(a, b, (((1,), (1,)), ((), ())), preferred_element_type=F32)


def _dot_tn(a, b):
    return lax.dot_general(a, b, (((0,), (0,)), ((), ())), preferred_element_type=F32)


def _softmax_full(s, mask):
    s = jnp.where(mask, s, NEG)
    m = jnp.max(s, axis=-1, keepdims=True)
    p = jnp.where(mask, jnp.exp(s - m), 0.0)
    l = jnp.sum(p, axis=-1, keepdims=True)
    return p, l


def _flash_step(state, s, mask, v):
    m, l, acc = state
    s = jnp.where(mask, s, NEG)
    m_new = jnp.maximum(m, jnp.max(s, axis=-1, keepdims=True))
    alpha = jnp.exp(m - m_new)
    p = jnp.where(mask, jnp.exp(s - m_new), 0.0)
    l = alpha * l + jnp.sum(p, axis=-1, keepdims=True)
    acc = alpha * acc + _dot(p.astype(BF16), v)
    return m_new, l, acc


def _flash_step_bias(state, s, v):
    m, l, acc = state
    m_new = jnp.maximum(m, jnp.max(s, axis=-1, keepdims=True))
    alpha = jnp.exp(m - m_new)
    p = jnp.exp(s - m_new)
    l = alpha * l + jnp.sum(p, axis=-1, keepdims=True)
    acc = alpha * acc + _dot(p.astype(BF16), v)
    return m_new, l, acc


def _proj_a_kernel(x_ref, gmix_ref, w_ref, wa2_ref, ba_ref, gqk_ref, cs_ref, sn_ref,
                   gla_ref, qc_ref, qr_ref, gates_ref, rc_ref, rs_ref, rw_ref, sbf_ref, wbf_ref,
                   rct_ref, rst_ref, rwt_ref):
    hn = _rms_rows(x_ref[...], gmix_ref[...])
    z = _dot(hn.astype(BF16), w_ref[...])
    cs, sn = cs_ref[...], sn_ref[...]
    gla_ref[:, G_Q:G_K] = z[:, A_GQ:A_GK] * (GLA_DK ** -0.5)
    gla_ref[:, G_K:G_LG] = z[:, A_GK:A_GA]
    pre = _dot(z[:, A_GA:A_NQ].astype(BF16), wa2_ref[...]) + ba_ref[...]
    gla_ref[:, G_LG:G_END] = jax.nn.log_sigmoid(pre) / GLA_TAU
    qn = _rms_heads(z[:, A_NQ:A_NKV], gqk_ref[0:1, :])
    qc_ref[...] = (qn * QSCALE).astype(BF16)
    qr_ref[...] = (_rope(qn, cs, sn) * QSCALE).astype(BF16)
    gates_ref[...] = jax.nn.sigmoid(z[:, A_NG:A_END])
    kv = [z[:, A_NKV + i * LANES:A_NKV + (i + 1) * LANES] for i in range(6)]
    rc_ref[:, 0:LANES] = _rms_heads(kv[0], gqk_ref[1:2, 0:LANES])
    rc_ref[:, LANES:] = kv[1]
    ks = _rope(_rms_heads(kv[2], gqk_ref[2:3, 0:LANES]), cs, sn)
    rs_ref[:, 0:LANES] = ks
    rs_ref[:, LANES:] = kv[3]
    kw = _rope(_rms_heads(kv[4], gqk_ref[3:4, 0:LANES]), cs, sn)
    rw_ref[:, 0:LANES] = kw
    rw_ref[:, LANES:] = kv[5]
    sbf_ref[:, 0:LANES] = ks.astype(BF16)
    sbf_ref[:, LANES:] = kv[3].astype(BF16)
    wbf_ref[:, 0:LANES] = kw.astype(BF16)
    wbf_ref[:, LANES:] = kv[5].astype(BF16)
    for src, dst in ((rc_ref, rct_ref), (rs_ref, rst_ref), (rw_ref, rwt_ref)):
        dst[...] = src[...].T


def _const_spec(shape):
    nd = len(shape)
    return pl.BlockSpec(shape, lambda *_: (0,) * nd)


def _proj_a(x, gmix, w, wa2, ba, gqk, cs, sn, tm):
    n = x.shape[0]
    tiles_per_seq = cs.shape[0] // tm
    row = lambda width: pl.BlockSpec((tm, width), lambda i: (i, 0))
    tab = pl.BlockSpec((tm, LANES), lambda i: (i % tiles_per_seq, 0))
    widths = (G_END, 512, 512, LANES, 256, 256, 256, 256, 256)
    dtypes = (F32, BF16, BF16, F32, F32, F32, F32, BF16, BF16)
    return pl.pallas_call(
        _proj_a_kernel,
        grid=(n // tm,),
        in_specs=[row(D_MODEL), _const_spec(gmix.shape), _const_spec(w.shape), _const_spec(wa2.shape),
                  _const_spec(ba.shape), _const_spec(gqk.shape), tab, tab],
        out_specs=[row(wd) for wd in widths]
                  + [pl.BlockSpec((None, 256, tm), lambda i: (i // tiles_per_seq, 0, i % tiles_per_seq))] * 3,
        out_shape=[jax.ShapeDtypeStruct((n, wd), dt) for wd, dt in zip(widths, dtypes)]
                  + [jax.ShapeDtypeStruct((n // cs.shape[0], 256, cs.shape[0]), F32)] * 3,
        compiler_params=_cparams("parallel"),
        name="proj_a",
    )(x, gmix, w, wa2, ba, gqk, cs, sn)


def _gla_kernel(gla_ref, s0_ref, gnorm_ref, o_ref, sout_ref, s_scr, *, chunk, sub, n_chunks, n_valid):
    i = pl.program_id(1)
    n_pairs = GLA_HEADS // 2
    n_sub = chunk // sub
    dk2, dv2 = 2 * GLA_DK, 2 * GLA_DV

    @pl.when(i == 0)
    def _():
        s_scr[...] = jnp.zeros(s_scr.shape, F32)
        for h in range(GLA_HEADS):
            p, e = divmod(h, 2)
            s_scr[p, e * GLA_DK:(e + 1) * GLA_DK, e * GLA_DV:(e + 1) * GLA_DV] = s0_ref[h]

    tri = (lax.broadcasted_iota(jnp.int32, (chunk, chunk), 0)
           >= lax.broadcasted_iota(jnp.int32, (chunk, chunk), 1)).astype(F32)
    low = lax.broadcasted_iota(jnp.int32, (chunk, dk2), 1) < GLA_DK
    low_sub = lax.broadcasted_iota(jnp.int32, (sub, dk2), 1) < GLA_DK
    t_sub = lax.broadcasted_iota(jnp.int32, (sub, 1), 0)
    row_blk = lax.broadcasted_iota(jnp.int32, (chunk, chunk), 0) // sub
    col_idx = lax.broadcasted_iota(jnp.int32, (chunk, chunk), 1)
    diag_blocks = ((lax.broadcasted_iota(jnp.int32, (dk2, dv2), 0) < GLA_DK)
                   == (lax.broadcasted_iota(jnp.int32, (dk2, dv2), 1) < GLA_DV))
    eye = (lax.broadcasted_iota(jnp.int32, (dk2, dk2), 0) == lax.broadcasted_iota(jnp.int32, (dk2, dk2), 1))
    gnorm = gnorm_ref[...]

    def do_chunk(c, carry):
        rows = pl.ds(pl.multiple_of(c * chunk, chunk), chunk)
        b_all = jnp.dot(tri, gla_ref[rows, G_LG:G_END], precision=HIGHEST, preferred_element_type=F32)
        for p in range(n_pairs):
            q2 = gla_ref[rows, G_Q + p * dk2:G_Q + (p + 1) * dk2]
            k2 = gla_ref[rows, G_K + p * dk2:G_K + (p + 1) * dk2]
            v2 = gla_ref[rows, G_V + p * dv2:G_V + (p + 1) * dv2]
            b2 = b_all[:, p * dk2:(p + 1) * dk2]
            s_pair = s_scr[p]
            o2 = _dot((q2 * jnp.exp(b2)).astype(BF16), s_pair.astype(BF16))

            if n_sub > 1:
                bref_rows = jnp.concatenate(
                    [b2[0:sub]] + [jnp.broadcast_to(b2[i * sub - 1:i * sub], (sub, dk2)) for i in range(1, n_sub)], axis=0)
                qt = q2 * jnp.exp(jnp.minimum(b2 - bref_rows, 0.0))
                qt_even = jnp.where(low, qt, 0.0).astype(BF16)
                qt_odd = jnp.where(low, 0.0, qt).astype(BF16)
                a_even = jnp.zeros((chunk, chunk), F32)
                a_odd = jnp.zeros((chunk, chunk), F32)
                for i in range(1, n_sub):
                    kt = (k2 * jnp.exp(jnp.minimum(b2[i * sub - 1:i * sub] - b2, 0.0))).astype(BF16)
                    take = (row_blk == i) & (col_idx < i * sub)
                    a_even = jnp.where(take, _dot_nt(qt_even, kt), a_even)
                    a_odd = jnp.where(take, _dot_nt(qt_odd, kt), a_odd)
                o2 = o2 + jnp.concatenate([_dot(a_even.astype(BF16), v2[:, 0:GLA_DV].astype(BF16)),
                                           _dot(a_odd.astype(BF16), v2[:, GLA_DV:].astype(BF16))], axis=1)

            diag = []
            for i in range(n_sub):
                sl = slice(i * sub, (i + 1) * sub)
                qs, ks, bs, vs = q2[sl], k2[sl], b2[sl], v2[sl]
                od = jnp.zeros((sub, dv2), F32)
                for jj in range(max(0, min(sub, n_valid - i * sub))):
                    w = qs * ks[jj:jj + 1] * jnp.exp(jnp.minimum(bs - bs[jj:jj + 1], 0.0))
                    keep = t_sub >= jj
                    a_e = jnp.where(keep, jnp.sum(jnp.where(low_sub, w, 0.0), axis=-1, keepdims=True), 0.0)
                    a_o = jnp.where(keep, jnp.sum(jnp.where(low_sub, 0.0, w), axis=-1, keepdims=True), 0.0)
                    od = od + jnp.concatenate([a_e * vs[jj:jj + 1, 0:GLA_DV], a_o * vs[jj:jj + 1, GLA_DV:]], axis=1)
                diag.append(od)
            o2 = o2 + (jnp.concatenate(diag, axis=0) if n_sub > 1 else diag[0])

            b_last = b2[chunk - 1:chunk, :]
            kdec = k2 * jnp.exp(b_last - b2)
            upd = _dot_tn(kdec.astype(BF16), v2.astype(BF16))
            decay_col = jnp.sum(jnp.where(eye, jnp.exp(b_last), 0.0), axis=1, keepdims=True)
            s_scr[p] = decay_col * s_pair + jnp.where(diag_blocks, upd, 0.0)
            for e in range(2):
                h = 2 * p + e
                on = _rms_rows(o2[:, e * GLA_DV:(e + 1) * GLA_DV], gnorm)
                r = gla_ref[rows, G_R + h * GLA_DV:G_R + (h + 1) * GLA_DV]
                o_ref[rows, h * GLA_DV:(h + 1) * GLA_DV] = (on * (r * jax.nn.sigmoid(r))).astype(o_ref.dtype)
        return carry
    lax.fori_loop(0, n_chunks, do_chunk, 0)

    @pl.when(i == pl.num_programs(1) - 1)
    def _():
        for h in range(GLA_HEADS):
            p, e = divmod(h, 2)
            sout_ref[h] = s_scr[p, e * GLA_DK:(e + 1) * GLA_DK, e * GLA_DV:(e + 1) * GLA_DV]


def _gla(gla, s0, gnorm, *, nb, rows_per_seq, chunk, n_valid, out_dtype):
    step_rows = min(rows_per_seq, 8 * chunk)
    steps = rows_per_seq // step_rows
    kern = functools.partial(_gla_kernel, chunk=chunk, sub=min(chunk, GLA_SUB), n_chunks=step_rows // chunk,
                             n_valid=n_valid)
    return pl.pallas_call(
        kern,
        grid=(nb, steps),
        in_specs=[pl.BlockSpec((step_rows, G_END), lambda b, i: (b * steps + i, 0)),
                  pl.BlockSpec((None, GLA_HEADS, GLA_DK, GLA_DV), lambda b, i: (b, 0, 0, 0)),
                  _const_spec(gnorm.shape)],
        out_specs=[pl.BlockSpec((step_rows, GLA_HEADS * GLA_DV), lambda b, i: (b * steps + i, 0)),
                   pl.BlockSpec((None, GLA_HEADS, GLA_DK, GLA_DV), lambda b, i: (b, 0, 0, 0))],
        out_shape=[jax.ShapeDtypeStruct((nb * rows_per_seq, GLA_HEADS * GLA_DV), out_dtype),
                   jax.ShapeDtypeStruct((nb, GLA_HEADS, GLA_DK, GLA_DV), F32)],
        scratch_shapes=[pltpu.VMEM((GLA_HEADS // 2, 2 * GLA_DK, 2 * GLA_DV), F32)],
        compiler_params=_cparams("parallel", "arbitrary"),
        name="gla",
    )(gla, s0, gnorm)


def _compress_kernel(*refs, n_parts, n_prefetch):
    refs = refs[n_prefetch:]
    x_refs = refs[:n_parts]
    wk_ref, wv_ref, pek_ref, pev_ref, kc_ref, vc_ref, sh_scr = refs[n_parts:]
    x = jnp.concatenate([r[...] for r in x_refs], axis=0) if n_parts > 1 else x_refs[0][...]
    m = x.shape[0]
    row_w = 2 * LANES

    def branch(off, w_ref, pe_ref, out_ref):
        lo = jnp.zeros((m, LANES), F32)
        hi = jnp.zeros((m, LANES), F32)
        for p in range(CMP_STRIDE):
            xp = x[:, p * row_w + off:p * row_w + off + LANES]
            lo = lo + _dot((xp + pe_ref[p:p + 1, :]).astype(BF16), w_ref[p])
            hi = hi + _dot((xp + pe_ref[CMP_STRIDE + p:CMP_STRIDE + p + 1, :]).astype(BF16), w_ref[CMP_STRIDE + p])
        sh_scr[pl.ds(0, m), :] = hi
        sh_scr[pl.ds(m, 8), :] = jnp.zeros((8, LANES), F32)
        out_ref[...] = (lo + sh_scr[pl.ds(1, m), :]).astype(out_ref.dtype)
    branch(0, wk_ref, pek_ref, kc_ref)
    branch(LANES, wv_ref, pev_ref, vc_ref)


def _compress_call(n_parts, m, grid, x_specs, out_map, out_rows, prefetch):
    kern = functools.partial(_compress_kernel, n_parts=n_parts, n_prefetch=prefetch)
    w_shape = (CMP_LEN, LANES, LANES)
    pe_shape = (CMP_LEN, LANES)
    in_specs = list(x_specs) + [_const_spec(w_shape), _const_spec(w_shape), _const_spec(pe_shape), _const_spec(pe_shape)]
    out_specs = [pl.BlockSpec((m, LANES), out_map), pl.BlockSpec((m, LANES), out_map)]
    gs = pltpu.PrefetchScalarGridSpec(num_scalar_prefetch=prefetch, grid=grid, in_specs=in_specs, out_specs=out_specs,
                                      scratch_shapes=[pltpu.VMEM((m + 8, LANES), F32)])
    return pl.pallas_call(
        kern, grid_spec=gs,
        out_shape=[jax.ShapeDtypeStruct((out_rows, LANES), BF16)] * 2,
        compiler_params=_cparams("parallel"),
        name="nsa_compress",
    )


def _compress_prompt(rows_c, nb, seq, wk, wv, pek, pev):
    m = seq // CMP_STRIDE
    x = rows_c.reshape(-1, CMP_STRIDE * 2 * LANES)
    call = _compress_call(1, m, (nb,), [pl.BlockSpec((m, CMP_STRIDE * 2 * LANES), lambda b: (b, 0))],
                          lambda b: (b, 0), nb * m, 0)
    return call(x, wk, wv, pek, pev)


def _compress_pages(pool, page_table, wk, wv, pek, pev):
    nb, n_pages = page_table.shape
    mp = PAGE_SIZE // CMP_STRIDE
    x = pool.reshape(pool.shape[0], mp, CMP_STRIDE * 2 * LANES)
    specs = [pl.BlockSpec((None, mp, CMP_STRIDE * 2 * LANES), functools.partial(lambda pg, b, pt: (pt[b, pg], 0, 0), pg))
             for pg in range(n_pages)]
    m = n_pages * mp
    call = _compress_call(n_pages, m, (nb,), specs, lambda b, pt: (b, 0), nb * m, 1)
    return call(page_table, *([x] * n_pages), wk, wv, pek, pev)


def _topk_mask(imp, qpos, n_slc):
    blk = lax.broadcasted_iota(jnp.int32, imp.shape, 1)
    cur = qpos // SLC_BLOCK
    val = jnp.where((blk == cur) | (blk == 0), BIG, jnp.where(blk <= cur, imp, NEG))
    val = jnp.where(blk < n_slc, val, REMOVED)
    sel = jnp.zeros(imp.shape, jnp.bool_)
    for _ in range(min(SLC_TOPK, n_slc)):
        mx = jnp.max(val, axis=-1, keepdims=True)
        idx = jnp.min(jnp.where(val == mx, blk, imp.shape[1]), axis=-1, keepdims=True)
        pick = blk == idx
        sel = sel | pick
        val = jnp.where(pick, REMOVED, val)
    return sel.astype(BF16)


def _block_expand(n_blocks_pad, k0, n_keys):
    blk = lax.broadcasted_iota(jnp.int32, (n_blocks_pad, n_keys), 0)
    col = lax.broadcasted_iota(jnp.int32, (n_blocks_pad, n_keys), 1)
    return (blk == (k0 + col) // SLC_BLOCK).astype(BF16)


def _flash_init(m_rows):
    return (jnp.full((m_rows, 1), NEG, F32), jnp.zeros((m_rows, 1), F32), jnp.zeros((m_rows, HEAD_DIM), F32))


def _nsa_prompt_kernel(qc_ref, qr_ref, gates_ref, kc_ref, vc_ref, ks_ref, kw_ref, mcs_ref, o_ref,
                       qa_scr, m_scr, l_scr, acc_scr, *, qb, seq, key_tile):
    t0 = pl.program_id(1) * qb
    qpos = t0 + lax.broadcasted_iota(jnp.int32, (qb, 1), 0)
    n_cmp_pad = kc_ref.shape[0]
    n_slc = seq // SLC_BLOCK
    hd = HEAD_DIM

    rows = NSA_HEADS * qb
    lane2 = lax.broadcasted_iota(jnp.int32, (qb, 2 * hd), 1)

    def stack_heads(q_ref):
        parts = []
        for h in range(NSA_HEADS):
            g = h // NSA_GROUP
            pair = q_ref[:, (h // 2) * 2 * hd:(h // 2 + 1) * 2 * hd].astype(F32)
            if h % 2 != g:
                pair = pltpu.roll(pair, hd, axis=1)
            parts.append(jnp.where((lane2 >= g * hd) & (lane2 < (g + 1) * hd), pair, 0.0).astype(BF16))
        return jnp.concatenate(parts, axis=0)

    def row_softmax(s, mask):
        s = jnp.where(mask, s, NEG)
        p = jnp.where(mask, jnp.exp(s - jnp.max(s, axis=-1, keepdims=True)), 0.0)
        return p, jnp.maximum(jnp.sum(p, axis=-1, keepdims=True), TINY)

    q_tok = t0 + lax.broadcasted_iota(jnp.int32, (rows, 1), 0) % qb
    qr_st = stack_heads(qr_ref)

    ncol = lax.broadcasted_iota(jnp.int32, (rows, n_cmp_pad), 1)
    p, l_cmp = row_softmax(_dot_nt(stack_heads(qc_ref), kc_ref[...]), (ncol * CMP_STRIDE + CMP_LEN - 1) <= q_tok)
    p = p * (1.0 / l_cmp)
    o_cmp = _dot(p.astype(BF16), vc_ref[...])
    psum = []
    for g in range(NSA_KV_HEADS):
        blocks = [p[(g * NSA_GROUP + j) * qb:(g * NSA_GROUP + j + 1) * qb] for j in range(NSA_GROUP)]
        psum.append(functools.reduce(lambda a, b: a + b, blocks))
    imp = jnp.dot(jnp.concatenate(psum, axis=0), mcs_ref[...], precision=HIGHEST, preferred_element_type=F32)
    sel = _topk_mask(imp, jnp.concatenate([qpos] * NSA_KV_HEADS, axis=0), n_slc)

    n_blk = sel.shape[1]
    qa_scr[:, 0:2 * hd] = qr_st
    for h in range(NSA_HEADS):
        g = h // NSA_GROUP
        qa_scr[h * qb:(h + 1) * qb, 2 * hd:] = 1.0 - sel[g * qb:(g + 1) * qb]
    m_scr[...] = jnp.full((rows, LANES), NEG, F32)
    l_scr[...] = jnp.zeros((rows, LANES), F32)
    acc_scr[...] = jnp.zeros((rows, 2 * hd), F32)

    def tile(k0, causal):
        kv = ks_ref[pl.ds(k0, key_tile), :]
        key_blk = (k0 + lax.broadcasted_iota(jnp.int32, (key_tile, n_blk), 0)) // SLC_BLOCK
        own_blk = key_blk == lax.broadcasted_iota(jnp.int32, (key_tile, n_blk), 1)
        k_aug = jnp.concatenate([kv[:, 0:2 * hd], jnp.where(own_blk, NEG, 0.0).astype(BF16)], axis=1)
        s = _dot_nt(qa_scr[...], k_aug)
        if causal:
            q_row = t0 + lax.broadcasted_iota(jnp.int32, (rows, key_tile), 0) % qb
            s = jnp.where(k0 + lax.broadcasted_iota(jnp.int32, (rows, key_tile), 1) <= q_row, s, NEG)
        m_old = m_scr[...]
        m_new = jnp.maximum(m_old, jnp.max(s, axis=-1, keepdims=True))
        alpha = jnp.exp(m_old - m_new)
        p = jnp.exp(s - jnp.concatenate([m_new] * (key_tile // LANES), axis=1))
        l_scr[...] = alpha * l_scr[...] + jnp.sum(p, axis=-1, keepdims=True)
        acc_scr[...] = alpha * acc_scr[...] + _dot(p.astype(BF16), kv[:, 2 * hd:4 * hd])
        m_scr[...] = m_new

    n_full = t0 // key_tile

    def full_tile(kt, carry):
        tile(pl.multiple_of(kt * key_tile, key_tile), False)
        return carry
    lax.fori_loop(0, n_full, full_tile, 0)
    tile(pl.multiple_of(n_full * key_tile, key_tile), True)

    wlen = min(seq, NSA_WINDOW + qb)
    start = pl.multiple_of(jnp.clip(t0 - NSA_WINDOW, 0, seq - wlen), qb)
    band = kw_ref[pl.ds(start, wlen), :]
    kpos = start + lax.broadcasted_iota(jnp.int32, (rows, wlen), 1)
    p, l_win = row_softmax(_dot_nt(qr_st, band[:, 0:2 * hd]), (kpos <= q_tok) & (kpos > q_tok - NSA_WINDOW))
    o_win = _dot(p.astype(BF16), band[:, 2 * hd:4 * hd])
    for h in range(NSA_HEADS):
        g = h // NSA_GROUP
        rs, ls = slice(h * qb, (h + 1) * qb), slice(g * hd, (g + 1) * hd)
        o_w = o_win[rs, ls] / l_win[rs]
        o_slc = acc_scr[rs, ls] / jnp.maximum(l_scr[rs, 0:hd], TINY)
        gate = gates_ref[:, 3 * h:3 * h + 3]
        o = gate[:, 0:1] * o_cmp[rs, ls] + gate[:, 1:2] * o_slc + gate[:, 2:3] * o_w
        o_ref[:, h * hd:(h + 1) * hd] = o.astype(o_ref.dtype)


def _nsa_prompt(qc, qr, gates, kc, vc, ks, kw, mcs, *, nb, seq, qb):
    nq = seq // qb
    n_cmp_pad = seq // CMP_STRIDE
    key_tile = min(seq, NSA_KEY_TILE)
    n_blk = _round_up(seq // SLC_BLOCK, LANES)
    kern = functools.partial(_nsa_prompt_kernel, qb=qb, seq=seq, key_tile=key_tile)
    qspec = lambda wd: pl.BlockSpec((qb, wd), lambda b, i: (b * nq + i, 0))
    seqspec = lambda rows, wd: pl.BlockSpec((rows, wd), lambda b, i: (b, 0))
    return pl.pallas_call(
        kern,
        grid=(nb, nq),
        in_specs=[qspec(512), qspec(512), qspec(LANES), seqspec(n_cmp_pad, LANES), seqspec(n_cmp_pad, LANES),
                  seqspec(seq, 256), seqspec(seq, 256), _const_spec(mcs.shape)],
        out_specs=qspec(512),
        out_shape=jax.ShapeDtypeStruct((nb * seq, 512), BF16),
        scratch_shapes=[pltpu.VMEM((NSA_HEADS * qb, 2 * HEAD_DIM + n_blk), BF16),
                        pltpu.VMEM((NSA_HEADS * qb, LANES), F32), pltpu.VMEM((NSA_HEADS * qb, LANES), F32),
                        pltpu.VMEM((NSA_HEADS * qb, 2 * HEAD_DIM), F32)],
        compiler_params=_cparams("parallel", "arbitrary"),
        name="nsa_prompt",
    )(qc, qr, gates, kc, vc, ks, kw, mcs)


def _attend_transposed(q_bf, q_f32, kt, vt, mask, newk, newv, mask_new, n_tok):
    m_rows = q_bf.shape[0]
    s = jnp.where(mask, _dot(q_bf, kt), NEG)
    lane = lax.broadcasted_iota(jnp.int32, (m_rows, TOK_PAD), 1)
    sn = jnp.full((m_rows, TOK_PAD), NEG, F32)
    for t in range(n_tok):
        sn = jnp.where(lane == t, jnp.sum(q_f32 * newk[t:t + 1], axis=-1, keepdims=True), sn)
    sn = jnp.where(mask_new, sn, NEG)
    m = jnp.maximum(jnp.max(s, axis=-1, keepdims=True), jnp.max(sn, axis=-1, keepdims=True))
    p = jnp.where(mask, jnp.exp(s - m), 0.0)
    pn = jnp.where(mask_new, jnp.exp(sn - m), 0.0)
    l = jnp.sum(p, axis=-1, keepdims=True) + jnp.sum(pn, axis=-1, keepdims=True)
    o = _dot_nt(p.astype(BF16), vt)
    for t in range(n_tok):
        o = o + pn[:, t:t + 1] * newv[t:t + 1]
    return o / jnp.maximum(l, TINY)


def _nsa_sample_kernel(*refs, n_pages, n_tok):
    (qc_ref, qr_ref, gates_ref, kc_ref, vc_ref) = refs[1:6]
    page_refs = refs[6:6 + n_pages]
    news_ref, win_ref, neww_ref, newwt_ref, mcs_ref, amat_ref, o_ref, owin_ref = refs[6 + n_pages:]
    hd = HEAD_DIM
    past = n_pages * PAGE_SIZE
    win = win_ref.shape[-1]
    m_rows = qc_ref.shape[1]
    tok = lax.broadcasted_iota(jnp.int32, (m_rows, 1), 0) % n_tok
    qpos = past + tok
    n_cmp_pad = kc_ref.shape[0]
    n_slc = (past + n_tok + SLC_BLOCK - 1) // SLC_BLOCK

    ncol = lax.broadcasted_iota(jnp.int32, (m_rows, n_cmp_pad), 1)
    cmask = ((ncol * CMP_STRIDE + CMP_LEN - 1) <= qpos) & (ncol < n_cmp_pad - 1)
    new_pos = past + lax.broadcasted_iota(jnp.int32, (m_rows, TOK_PAD), 1)
    new_ok = (new_pos <= qpos) & (new_pos < past + n_tok)
    wpos = past - win + lax.broadcasted_iota(jnp.int32, (m_rows, win), 1)
    wmask = (wpos <= qpos) & (wpos > qpos - NSA_WINDOW) & (wpos >= 0)
    wmask_new = new_ok & (new_pos > qpos - NSA_WINDOW)

    o_cmp, imps = [], []
    for g in range(NSA_KV_HEADS):
        ksl = slice(g * hd, (g + 1) * hd)
        p, l = _softmax_full(_dot_nt(qc_ref[g], kc_ref[:, ksl]), cmask)
        p = p / jnp.maximum(l, TINY)
        o_cmp.append(_dot(p.astype(BF16), vc_ref[:, ksl]))
        psum = jnp.dot(amat_ref[...], p, precision=HIGHEST, preferred_element_type=F32)
        imps.append(jnp.dot(psum, mcs_ref[...], precision=HIGHEST, preferred_element_type=F32))
    sel_all = _topk_mask(jnp.concatenate(imps, axis=0), jnp.concatenate([qpos] * NSA_KV_HEADS, axis=0), n_slc)
    expand = _block_expand(LANES, 0, past)
    new_blk = lax.broadcasted_iota(jnp.int32, (m_rows, LANES), 1) == past // SLC_BLOCK

    for g in range(NSA_KV_HEADS):
        ksl = slice(g * hd, (g + 1) * hd)
        vsl = slice(LANES + g * hd, LANES + (g + 1) * hd)
        qr = qr_ref[g]
        qr32 = qr.astype(F32)
        sel = sel_all[g * m_rows:(g + 1) * m_rows]
        sel_new = jnp.sum(jnp.where(new_blk, sel.astype(F32), 0.0), axis=-1, keepdims=True) > 0.5
        kt = jnp.concatenate([r[0, g].astype(BF16) for r in page_refs], axis=1)
        vt = jnp.concatenate([r[1, g].astype(BF16) for r in page_refs], axis=1)
        o_slc = _attend_transposed(qr, qr32, kt, vt, _dot(sel, expand) > 0.5,
                                   news_ref[:, ksl], news_ref[:, vsl], new_ok & sel_new, n_tok)
        o_win = _attend_transposed(qr, qr32, win_ref[0, g].astype(BF16), win_ref[1, g].astype(BF16), wmask,
                                   neww_ref[:, ksl], neww_ref[:, vsl], wmask_new, n_tok)
        gate = gates_ref[g]
        o_ref[g] = gate[:, 0:1] * o_cmp[g] + gate[:, 1:2] * o_slc + gate[:, 2:3] * o_win

        lane = lax.broadcasted_iota(jnp.int32, (hd, LANES), 1)
        for kv in range(2):
            shifted = pltpu.roll(win_ref[kv, g], win - n_tok, axis=1)
            last = shifted[:, win - LANES:]
            for t in range(n_tok):
                last = jnp.where(lane == LANES - n_tok + t, newwt_ref[kv, g, :, t:t + 1], last)
            if win > LANES:
                owin_ref[kv, g, :, 0:win - LANES] = shifted[:, 0:win - LANES]
            owin_ref[kv, g, :, win - LANES:] = last


def _nsa_sample(page_table, qc, qr, gates, kc, vc, pool_s, news, win_t, neww, neww_t, mcs, amat, *, n_tok):
    nb, n_pages = page_table.shape
    m_rows = qc.shape[2]
    kern = functools.partial(_nsa_sample_kernel, n_pages=n_pages, n_tok=n_tok)
    per_b = lambda a: pl.BlockSpec((None,) + a.shape[1:], lambda b, pt: (b,) + (0,) * (a.ndim - 1))
    n_cmp_pad = kc.shape[0] // nb
    cmp_spec = pl.BlockSpec((n_cmp_pad, LANES), lambda b, pt: (b, 0))
    page_specs = [pl.BlockSpec((None,) + pool_s.shape[1:], functools.partial(lambda pg, b, pt: (pt[b, pg], 0, 0, 0, 0), pg))
                  for pg in range(n_pages)]
    gs = pltpu.PrefetchScalarGridSpec(
        num_scalar_prefetch=1, grid=(nb,),
        in_specs=[per_b(qc), per_b(qr), per_b(gates), cmp_spec, cmp_spec] + page_specs +
                 [per_b(news), per_b(win_t), per_b(neww), per_b(neww_t),
                  pl.BlockSpec(mcs.shape, lambda b, pt: (0, 0)), pl.BlockSpec(amat.shape, lambda b, pt: (0, 0))],
        out_specs=[per_b(qc), per_b(win_t)])
    return pl.pallas_call(
        kern, grid_spec=gs,
        out_shape=[jax.ShapeDtypeStruct((nb, NSA_KV_HEADS, m_rows, HEAD_DIM), F32),
                   jax.ShapeDtypeStruct(win_t.shape, F32)],
        compiler_params=_cparams("parallel"),
        name="nsa_sample",
    )(page_table, qc, qr, gates, kc, vc, *([pool_s] * n_pages), news, win_t, neww, neww_t, mcs, amat)


C_GROUP_COLS = 3 * DIL_HEADS * HEAD_DIM


def _proj_c_kernel(x_ref, gmix_ref, w_ref, gq_ref, gk_ref, cs_ref, sn_ref, *out_refs, tails):
    q_refs, r_refs, t_refs = out_refs[:N_DIL], out_refs[N_DIL:2 * N_DIL], out_refs[2 * N_DIL:]
    hn = _rms_rows(x_ref[...], gmix_ref[...]).astype(BF16)
    cs, sn = cs_ref[...], sn_ref[...]
    wd = DIL_HEADS * HEAD_DIM
    for g in range(N_DIL):
        z = _dot(hn, w_ref[:, g * C_GROUP_COLS:(g + 1) * C_GROUP_COLS])
        q = _rope(_rms_heads(z[:, 0:wd], gq_ref[g:g + 1, :]), cs, sn)
        q_refs[g][...] = (q * QSCALE).astype(BF16)
        r_refs[g][:, 0:wd] = _rope(_rms_heads(z[:, wd:2 * wd], gk_ref[g:g + 1, :]), cs, sn)
        r_refs[g][:, wd:] = z[:, 2 * wd:]
        if tails is not None:
            first, cols = tails[g]
            tm = x_ref.shape[0]

            in_window = pl.program_id(0) % tails[N_DIL] >= first

            @pl.when(in_window)
            def _():
                t_refs[g][...] = r_refs[g][tm - cols:tm, :].T

            @pl.when(jnp.logical_not(in_window))
            def _():
                t_refs[g][...] = jnp.zeros(t_refs[g].shape, F32)


def _proj_c(x, gmix, w, gq, gk, cs, sn, tm, with_tails):
    n = x.shape[0]
    seq = cs.shape[0]
    tiles_per_seq = seq // tm
    row = lambda width: pl.BlockSpec((tm, width), lambda i: (i, 0))
    tab = pl.BlockSpec((tm, LANES), lambda i: (i % tiles_per_seq, 0))
    wd = DIL_HEADS * HEAD_DIM
    out_specs = [row(wd)] * N_DIL + [row(2 * wd)] * N_DIL
    out_shape = [jax.ShapeDtypeStruct((n, wd), BF16)] * N_DIL + [jax.ShapeDtypeStruct((n, 2 * wd), F32)] * N_DIL
    tails = None
    if with_tails:
        tails = []
        for win, _ in DIL_PAIRS:
            rows = min(win, seq)
            cols = min(rows, tm)
            assert rows % cols == 0 and seq % tm == 0
            first = tiles_per_seq - max(rows // tm, 1)
            tails.append((first, cols))
            out_specs.append(pl.BlockSpec(
                (None, 2 * wd, cols),
                functools.partial(lambda first, i: (i // tiles_per_seq, 0, jnp.maximum(i % tiles_per_seq - first, 0)), first)))
            out_shape.append(jax.ShapeDtypeStruct((n // seq, 2 * wd, rows), F32))
        tails = tuple(tails) + (tiles_per_seq,)
    return pl.pallas_call(
        functools.partial(_proj_c_kernel, tails=tails),
        grid=(n // tm,),
        in_specs=[row(D_MODEL), _const_spec(gmix.shape), _const_spec(w.shape), _const_spec(gq.shape),
                  _const_spec(gk.shape), tab, tab],
        out_specs=out_specs,
        out_shape=out_shape,
        compiler_params=_cparams("arbitrary" if with_tails else "parallel"),
        name="proj_c",
    )(x, gmix, w, gq, gk, cs, sn)


def _dil_prompt_kernel(q_ref, kp_ref, vp_ref, kc_ref, vc_ref, o_ref, ml_ref, *, tq):
    i = pl.program_id(2)
    row = lax.broadcasted_iota(jnp.int32, (tq, 2 * tq), 0)
    col = lax.broadcasted_iota(jnp.int32, (tq, 2 * tq), 1)
    delta = tq + row - col
    mask = (delta >= 0) & (delta < DIL_KEYS) & ((col >= tq) | (i > 0))
    k = jnp.concatenate([kp_ref[...], kc_ref[...]], axis=0).astype(BF16)
    v = jnp.concatenate([vp_ref[...], vc_ref[...]], axis=0).astype(BF16)
    lane = lax.broadcasted_iota(jnp.int32, (tq, LANES), 1)
    ml = jnp.zeros((tq, LANES), F32)
    hd = HEAD_DIM
    for h in range(DIL_HEADS):
        hs = slice(h * hd, (h + 1) * hd)
        s = jnp.where(mask, _dot_nt(q_ref[:, hs], k[:, hs]), NEG)
        m = jnp.max(s, axis=-1, keepdims=True)
        p = jnp.where(mask, jnp.exp(s - m), 0.0)
        l = jnp.sum(p, axis=-1, keepdims=True)
        o_ref[:, hs] = _dot(p.astype(BF16), v[:, hs])
        ml = jnp.where(lane == h, m, jnp.where(lane == DIL_HEADS + h, l, ml))
    ml_ref[...] = ml


def _dil_prompt(q, rows, *, nb, seq, dil, tq):
    wd = DIL_HEADS * HEAD_DIM
    assert tq >= DIL_KEYS - 1 and seq % (dil * tq) == 0
    nu = seq // dil // tq
    qv = q.reshape(nb * seq // dil, dil * wd)
    rv = rows.reshape(nb * seq // dil, dil * 2 * wd)
    cur = lambda off: (lambda b, r, i: (b * nu + i, 2 * r + off))
    prev = lambda off: (lambda b, r, i: (b * nu + jnp.maximum(i - 1, 0), 2 * r + off))
    blk = lambda imap: pl.BlockSpec((tq, wd), imap)
    o, ml = pl.pallas_call(
        functools.partial(_dil_prompt_kernel, tq=tq),
        grid=(nb, dil, nu),
        in_specs=[blk(lambda b, r, i: (b * nu + i, r)), blk(prev(0)), blk(prev(1)), blk(cur(0)), blk(cur(1))],
        out_specs=[blk(lambda b, r, i: (b * nu + i, r)), pl.BlockSpec((tq, LANES), lambda b, r, i: (b * nu + i, r))],
        out_shape=[jax.ShapeDtypeStruct(qv.shape, F32), jax.ShapeDtypeStruct((qv.shape[0], dil * LANES), F32)],
        compiler_params=_cparams("parallel", "parallel", "arbitrary"),
        name="dil_prompt",
    )(qv, rv, rv, rv, rv)
    return o.reshape(nb * seq, wd), ml.reshape(nb * seq, LANES)


DIL_SAMPLE_HEADS = 4


def _dil_sample_kernel(q_ref, n0_ref, n1_ref, n2_ref, c0_ref, c1_ref, c2_ref, o_ref, oc0_ref, oc1_ref, oc2_ref,
                       *, n_tok):
    n_refs, c_refs, oc_refs = (n0_ref, n1_ref, n2_ref), (c0_ref, c1_ref, c2_ref), (oc0_ref, oc1_ref, oc2_ref)
    tp = lax.broadcasted_iota(jnp.int32, (1, TOK_PAD), 1)
    lane_t = lax.broadcasted_iota(jnp.int32, (HEAD_DIM, TOK_PAD), 1)
    for hl in range(q_ref.shape[1]):
        stats = [[None] * N_DIL for _ in range(n_tok)]
        for g, (w, dil) in enumerate(DIL_PAIRS):
            nk = n_refs[g][0, hl]
            nv = n_refs[g][1, hl]
            row = lax.broadcasted_iota(jnp.int32, (1, w), 1)
            n_ch = w // LANES
            q_cols = [q_ref[g, hl, :, t:t + 1] for t in range(n_tok)]
            q_wide = [jnp.broadcast_to(qc, (HEAD_DIM, LANES)) for qc in q_cols]
            shared = dil >= n_tok
            groups = [list(range(n_tok))] if shared else [[t] for t in range(n_tok)]
            lane_tok = lax.broadcasted_iota(jnp.int32, (HEAD_DIM, LANES), 1) & (dil - 1)
            row_tok = row & (dil - 1)
            for toks in groups:
                if shared:
                    q_sel = jnp.zeros((HEAD_DIM, LANES), F32)
                    for t in toks:
                        q_sel = jnp.where(lane_tok == t, q_wide[t], q_sel)
                else:
                    q_sel = q_wide[toks[0]]
                s_parts = [jnp.sum(c_refs[g][0, hl, :, c * LANES:(c + 1) * LANES] * q_sel, axis=0, keepdims=True)
                           for c in range(n_ch)]
                s_all = jnp.concatenate(s_parts, axis=1) if n_ch > 1 else s_parts[0]
                m_lanes = jnp.zeros((1, w), F32)
                any_valid = jnp.zeros((1, w), jnp.bool_)
                valids, pnews = [], []
                for t in toks:
                    valid = (row_tok == (t & (dil - 1))) & (w + t - row <= (DIL_KEYS - 1) * dil)
                    dn = t - tp
                    valid_n = (dn >= 0) & ((dn & (dil - 1)) == 0)
                    sn = jnp.where(valid_n, jnp.sum(nk * q_cols[t], axis=0, keepdims=True), NEG)
                    m = jnp.maximum(jnp.max(jnp.where(valid, s_all, NEG), axis=-1, keepdims=True),
                                    jnp.max(sn, axis=-1, keepdims=True))
                    m_lanes = jnp.where(valid, m, m_lanes)
                    any_valid = any_valid | valid
                    valids.append(valid)
                    pnews.append(jnp.where(valid_n, jnp.exp(sn - m), 0.0))
                    stats[t][g] = (None, m, None)
                p_all = jnp.where(any_valid, jnp.exp(s_all - m_lanes), 0.0)
                acc = jnp.zeros((HEAD_DIM, LANES), F32)
                for c in range(n_ch):
                    acc = acc + c_refs[g][1, hl, :, c * LANES:(c + 1) * LANES] * p_all[:, c * LANES:(c + 1) * LANES]
                for t, valid, pn in zip(toks, valids, pnews):
                    l = jnp.sum(jnp.where(valid, p_all, 0.0), axis=-1, keepdims=True) + jnp.sum(pn, axis=-1, keepdims=True)
                    own = acc if not shared else jnp.where(lane_tok == t, acc, 0.0)
                    o = jnp.sum(own, axis=-1, keepdims=True) + jnp.sum(nv * pn, axis=-1, keepdims=True)
                    stats[t][g] = (o, stats[t][g][1], l)
            lane = lax.broadcasted_iota(jnp.int32, (HEAD_DIM, LANES), 1)
            for kv, new in ((0, nk), (1, nv)):
                shifted = pltpu.roll(c_refs[g][kv, hl], w - n_tok, axis=1)
                last = shifted[:, w - LANES:]
                for t in range(n_tok):
                    last = jnp.where(lane == LANES - n_tok + t, new[:, t:t + 1], last)
                if w > LANES:
                    oc_refs[g][kv, hl, :, 0:w - LANES] = shifted[:, 0:w - LANES]
                oc_refs[g][kv, hl, :, w - LANES:] = last
        out = jnp.zeros((HEAD_DIM, TOK_PAD), F32)
        for t in range(n_tok):
            parts = stats[t]
            mx = functools.reduce(jnp.maximum, [m for _, m, _ in parts])
            num = sum(jnp.exp(m - mx) * o for o, m, _ in parts)
            den = sum(jnp.exp(m - mx) * l for _, m, l in parts)
            out = jnp.where(lane_t == t, num / den, out)
        o_ref[hl] = out


def _dil_sample(q, news, caches, *, n_tok):
    nb = q.shape[0]
    hs = DIL_SAMPLE_HEADS
    for g, (w, d) in enumerate(DIL_PAIRS):
        assert caches[g].shape[-1] == w and w // d == DIL_KEYS - 1
        assert d & (d - 1) == 0
    blk = lambda a: pl.BlockSpec((None, a.shape[1], hs) + a.shape[3:], lambda b, hh: (b, 0, hh, 0, 0))
    o_spec = pl.BlockSpec((None, hs, HEAD_DIM, TOK_PAD), lambda b, hh: (b, hh, 0, 0))
    return pl.pallas_call(
        functools.partial(_dil_sample_kernel, n_tok=n_tok),
        grid=(nb, DIL_HEADS // hs),
        in_specs=[blk(q)] + [blk(a) for a in news] + [blk(a) for a in caches],
        out_specs=[o_spec] + [blk(a) for a in caches],
        out_shape=[jax.ShapeDtypeStruct((nb, DIL_HEADS, HEAD_DIM, TOK_PAD), F32)]
                  + [jax.ShapeDtypeStruct(a.shape, F32) for a in caches],
        compiler_params=_cparams("parallel", "parallel"),
        name="dil_sample",
    )(q, *news, *caches)


FF_CHUNK = 1024


def _post_kernel(*refs, n_cat, combine):
    h_ref = refs[0]
    mix_refs = refs[1:1 + (2 * N_DIL if combine else n_cat)]
    (wo_ref, gmlp_ref, w1_ref, w2_ref, gple_ref, wg_ref, p_ref, wp_ref, out_ref) = refs[1 + len(mix_refs):1 + len(mix_refs) + 9]
    hd = HEAD_DIM
    if combine:
        comb_scr = refs[-1]
        o_refs, ml_refs = mix_refs[:N_DIL], mix_refs[N_DIL:]
        for h in range(DIL_HEADS):
            ms = [r[:, h:h + 1] for r in ml_refs]
            ls = [r[:, DIL_HEADS + h:DIL_HEADS + h + 1] for r in ml_refs]
            mx = functools.reduce(jnp.maximum, ms)
            ws = [jnp.exp(m - mx) for m in ms]
            num = sum(w * r[:, h * hd:(h + 1) * hd] for w, r in zip(ws, o_refs))
            den = sum(w * l for w, l in zip(ws, ls))
            comb_scr[:, h * hd:(h + 1) * hd] = (num / den).astype(BF16)
        y = _dot(comb_scr[...], wo_ref[...])
    else:
        y = None
        for k, r in enumerate(mix_refs):
            part = _dot(r[...].astype(BF16), wo_ref[k * r.shape[1]:(k + 1) * r.shape[1], :])
            y = part if y is None else y + part
    h1 = h_ref[...] + y
    hn = _rms_rows(h1, gmlp_ref[...]).astype(BF16)
    acc = jnp.zeros(h1.shape, F32)
    for c in range(D_FF // FF_CHUNK):
        u = _dot(hn, w1_ref[:, c * FF_CHUNK:(c + 1) * FF_CHUNK])
        acc = acc + _dot(jnp.square(jnp.maximum(u, 0.0)).astype(BF16), w2_ref[c * FF_CHUNK:(c + 1) * FF_CHUNK, :])
    h2 = h1 + acc
    gate = jax.nn.sigmoid(_dot(_rms_rows(h2, gple_ref[...]).astype(BF16), wg_ref[...]))
    out_ref[...] = h2 + gate * _dot(p_ref[...].astype(BF16), wp_ref[...])


def _post(h, mix, wo, gmlp, w1, w2, gple, wg, p, wp, *, combine, tm):
    n = h.shape[0]
    row = lambda a: pl.BlockSpec((tm, a.shape[1]), lambda i: (i, 0))
    weights = (wo, gmlp, w1, w2, gple, wg)
    wspec = lambda a: pl.BlockSpec(a.shape, lambda i: (0, 0), pipeline_mode=pl.Buffered(1))
    kern = functools.partial(_post_kernel, n_cat=len(mix), combine=combine)
    return pl.pallas_call(
        kern,
        grid=(n // tm,),
        in_specs=[row(h)] + [row(a) for a in mix] + [wspec(a) for a in weights] + [row(p), wspec(wp)],
        out_specs=row(h),
        out_shape=jax.ShapeDtypeStruct(h.shape, F32),
        scratch_shapes=[pltpu.VMEM((tm, DIL_HEADS * HEAD_DIM), BF16)] if combine else [],
        compiler_params=_cparams("parallel"),
        name="post",
    )(h, *mix, *weights, p, wp)


def _rope_tables(pos):
    half = HEAD_DIM // 2
    freq = ROPE_THETA ** (-jnp.arange(half, dtype=F32) / half)
    ang = pos.astype(F32)[:, None] * freq[None, :]
    cos, sin = jnp.cos(ang), jnp.sin(ang)
    return jnp.tile(jnp.concatenate([cos, cos], axis=1), (1, 2)), jnp.tile(jnp.concatenate([-sin, sin], axis=1), (1, 2))


def _cmp_to_slc(n_cmp, n_slc, rows_pad, cols_pad):
    cs = jnp.arange(n_cmp)[:, None] * CMP_STRIDE
    ss = jnp.arange(n_slc)[None, :] * SLC_BLOCK
    ov = jnp.maximum(jnp.minimum(cs + CMP_LEN, ss + SLC_BLOCK) - jnp.maximum(cs, ss), 0)
    return jnp.pad(ov.astype(F32) / CMP_LEN, ((0, rows_pad - n_cmp), (0, cols_pad - n_slc)))


def _round_up(x, m):
    return (x + m - 1) // m * m


def _block_diag2(w):
    z = jnp.zeros_like(w)
    return jnp.concatenate([jnp.concatenate([w, z], axis=2), jnp.concatenate([z, w], axis=2)], axis=1)


def kernel(x_prompt, x_sample, state_gla, cache_nsa_cmp, cache_nsa_slc, cache_nsa_win, cache_dil_0, cache_dil_1,
           cache_dil_2, page_table, p_prompt, p_sample, norm_mix, norm_mlp, norm_ple, a_w_in, a_w_out, gla_w_a2,
           gla_b_a, gla_g_norm, nsa_g_qk, nsa_w_phi, nsa_pe, c_w_in, c_g_qk, c_w_out, mlp_w1, mlp_w2, ple_w_proj,
           ple_w_gate):
    nb_p, seq, _ = x_prompt.shape
    nb_s, n_tok, _ = x_sample.shape
    depth = norm_mix.shape[0]
    past = page_table.shape[1] * PAGE_SIZE
    assert n_tok < CMP_STRIDE and n_tok <= 8 and seq % 512 == 0
    tm_p = 256
    tm_s = min(256, nb_s * n_tok)
    cs_p, sn_p = _rope_tables(jnp.arange(seq))
    cs_s, sn_s = _rope_tables(past + jnp.arange(nb_s * n_tok) % n_tok)
    hp = x_prompt.reshape(nb_p * seq, D_MODEL)
    hs = x_sample.reshape(nb_s * n_tok, D_MODEL)
    dil_caches = (cache_dil_0, cache_dil_1, cache_dil_2)
    outs = {k: [] for k in ("gla_p", "gla_s", "cmp_p", "cmp_s", "slc_p", "slc_s", "win_p", "win_s")}
    dil_p = [[] for _ in DIL_PAIRS]
    dil_s = [[] for _ in DIL_PAIRS]
    kvrow = (2, NSA_KV_HEADS, HEAD_DIM)
    wd = DIL_HEADS * HEAD_DIM

    for i in range(depth):
        j = i // 2
        gmix = norm_mix[i][None, :]
        if i % 2 == 0:
            w_in = a_w_in[j]
            pad_cols = lambda a, width: jnp.pad(a, ((0, 0), (0, width - a.shape[1])))
            w_pad = jnp.concatenate([w_in[:, :1536], pad_cols(w_in[:, 1536:1552], LANES), w_in[:, 1552:2832],
                                     pad_cols(w_in[:, 2832:], LANES)], axis=1).astype(BF16)
            wa2 = jnp.pad(gla_w_a2[j], ((0, LANES - GLA_LOWRANK), (0, 0))).astype(BF16)
            ba = gla_b_a[j][None, :]
            gqk = jnp.tile(nsa_g_qk[j], (1, NSA_HEADS))
            gnorm = gla_g_norm[j][None, :]
            wk, wv = (_block_diag2(nsa_w_phi[j, c]).astype(BF16) for c in range(2))
            pek, pev = (jnp.tile(nsa_pe[j, c], (1, 2)) for c in range(2))
            wo = a_w_out[j].astype(BF16)

            gla, qc, qr, gates, rc, _, _, sbf, wbf, rc_t, rs_t, rw_t = _proj_a(hp, gmix, w_pad, wa2, ba, gqk, cs_p, sn_p,
                                                                               tm_p)
            s0 = jnp.zeros((nb_p, GLA_HEADS, GLA_DK, GLA_DV), F32)
            chunk = min(GLA_CHUNK, seq)
            o_gla, s_p = _gla(gla, s0, gnorm, nb=nb_p, rows_per_seq=seq, chunk=chunk, n_valid=chunk, out_dtype=BF16)
            kc, vc = _compress_prompt(rc, nb_p, seq, wk, wv, pek, pev)
            n_cmp_pad = seq // CMP_STRIDE
            mcs = _cmp_to_slc(n_cmp_pad - 1, seq // SLC_BLOCK, n_cmp_pad, _round_up(seq // SLC_BLOCK, LANES))
            o_nsa = _nsa_prompt(qc, qr, gates, kc, vc, sbf, wbf, mcs, nb=nb_p, seq=seq, qb=NSA_Q_BLOCK)
            mix_p = (o_gla, o_nsa)
            outs["gla_p"].append(s_p)
            rows_out = lambda a: a.reshape((nb_p,) + kvrow + (a.shape[-1],)).transpose(0, 4, 1, 2, 3)
            outs["cmp_p"].append(rows_out(rc_t))
            outs["slc_p"].append(rows_out(rs_t))
            outs["win_p"].append(rows_out(rw_t[:, :, seq - min(NSA_WINDOW, seq):]))

            gla, qc, qr, gates, rc, rs, rw = _proj_a(hs, gmix, w_pad, wa2, ba, gqk, cs_s, sn_s, tm_s)[:7]
            gla8 = jnp.pad(gla.reshape(nb_s, n_tok, G_END), ((0, 0), (0, 8 - n_tok), (0, 0))).reshape(nb_s * 8, G_END)
            o_gla8, s_s = _gla(gla8, state_gla[j], gnorm, nb=nb_s, rows_per_seq=8, chunk=8, n_valid=n_tok, out_dtype=F32)
            o_gla = o_gla8.reshape(nb_s, 8, -1)[:, :n_tok].reshape(nb_s * n_tok, -1)
            kc, vc = _compress_pages(cache_nsa_cmp[j].reshape(-1, PAGE_SIZE, 256), page_table, wk, wv, pek, pev)
            n_cmp_pad = past // CMP_STRIDE
            n_slc = (past + n_tok + SLC_BLOCK - 1) // SLC_BLOCK
            mcs = _cmp_to_slc(n_cmp_pad - 1, n_slc, n_cmp_pad, LANES)
            m_rows = NSA_GROUP * n_tok
            regroup = lambda a: a.reshape(nb_s, n_tok, NSA_KV_HEADS, NSA_GROUP, -1).transpose(0, 2, 3, 1, 4).reshape(
                nb_s, NSA_KV_HEADS, m_rows, -1)
            g16 = jnp.pad(regroup(gates[:, :3 * NSA_HEADS]), ((0, 0), (0, 0), (0, 0), (0, LANES - 3)))
            tok_id = jnp.arange(m_rows) % n_tok
            amat = (tok_id[:, None] == tok_id[None, :]).astype(F32)
            pad_rows = lambda a: jnp.pad(a.reshape(nb_s, n_tok, 256), ((0, 0), (0, TOK_PAD - n_tok), (0, 0)))
            keyt = lambda a: a.transpose(0, 2, 3, 4, 1)
            neww_t = jnp.pad(keyt(rw.reshape((nb_s, n_tok) + kvrow)), [(0, 0)] * 4 + [(0, TOK_PAD - n_tok)])
            o16, win_new = _nsa_sample(page_table, regroup(qc), regroup(qr), g16, kc, vc, keyt(cache_nsa_slc[j]),
                                       pad_rows(rs), keyt(cache_nsa_win[j]), pad_rows(rw), neww_t, mcs, amat,
                                       n_tok=n_tok)
            o_nsa = o16.reshape(nb_s, NSA_KV_HEADS, NSA_GROUP, n_tok, HEAD_DIM).transpose(0, 3, 1, 2, 4).reshape(
                nb_s * n_tok, NSA_HEADS * HEAD_DIM)
            mix_s = (o_gla, o_nsa)
            outs["gla_s"].append(s_s)
            outs["cmp_s"].append(rc.reshape((nb_s, n_tok) + kvrow))
            outs["slc_s"].append(rs.reshape((nb_s, n_tok) + kvrow))
            outs["win_s"].append(win_new.transpose(0, 4, 1, 2, 3))
            combine = False
        else:
            w_c = c_w_in[j].astype(BF16)
            gq = jnp.tile(c_g_qk[j][:, 0], (1, DIL_HEADS))
            gk = jnp.tile(c_g_qk[j][:, 1], (1, DIL_HEADS))
            wo = c_w_out[j].astype(BF16)
            dilrow = (2, DIL_HEADS, HEAD_DIM)

            res = _proj_c(hp, gmix, w_c, gq, gk, cs_p, sn_p, tm_p, True)
            qs, rows, tails = res[:N_DIL], res[N_DIL:2 * N_DIL], res[2 * N_DIL:]
            o_parts, ml_parts = [], []
            for g, (w, d) in enumerate(DIL_PAIRS):
                o_g, ml_g = _dil_prompt(qs[g], rows[g], nb=nb_p, seq=seq, dil=d, tq=128)
                o_parts.append(o_g)
                ml_parts.append(ml_g)
                dil_p[g].append(tails[g].reshape((nb_p,) + dilrow + (tails[g].shape[-1],)).transpose(0, 4, 1, 2, 3))
            mix_p = tuple(o_parts) + tuple(ml_parts)

            res = _proj_c(hs, gmix, w_c, gq, gk, cs_s, sn_s, tm_s, False)
            qs, rows = res[:N_DIL], res[N_DIL:]
            pad_tok = lambda a: jnp.pad(a, [(0, 0)] * (a.ndim - 1) + [(0, TOK_PAD - n_tok)])
            q_t = pad_tok(jnp.stack([q.astype(F32).reshape(nb_s, n_tok, DIL_HEADS, HEAD_DIM).transpose(0, 2, 3, 1)
                                     for q in qs], axis=1))
            news_t = [pad_tok(r.reshape((nb_s, n_tok) + dilrow).transpose(0, 2, 3, 4, 1)) for r in rows]
            caches_t = [c[j].transpose(0, 2, 3, 4, 1) for c in dil_caches]
            o_t, *new_caches = _dil_sample(q_t, news_t, caches_t, n_tok=n_tok)
            mix_s = (o_t[..., :n_tok].transpose(0, 3, 1, 2).reshape(nb_s * n_tok, wd),)
            for g in range(N_DIL):
                dil_s[g].append(new_caches[g].transpose(0, 4, 1, 2, 3))
            combine = True

        lw = (norm_mlp[i][None, :], mlp_w1[i].astype(BF16), mlp_w2[i].astype(BF16), norm_ple[i][None, :],
              ple_w_gate[i].astype(BF16))
        wp = ple_w_proj[i].astype(BF16)
        hp = _post(hp, mix_p, wo, *lw, p_prompt[i].reshape(nb_p * seq, PLE_DIM), wp, combine=combine, tm=tm_p)
        hs = _post(hs, mix_s, wo, *lw, p_sample[i].reshape(nb_s * n_tok, PLE_DIM), wp, combine=False, tm=tm_s)

    st = jnp.stack
    return (hp.reshape(x_prompt.shape), hs.reshape(x_sample.shape),
            st(outs["gla_p"]), st(outs["gla_s"]), st(outs["cmp_p"]), st(outs["cmp_s"]),
            st(outs["slc_p"]), st(outs["slc_s"]), st(outs["win_p"]), st(outs["win_s"]),
            st(dil_p[0]), st(dil_s[0]), st(dil_p[1]), st(dil_s[1]), st(dil_p[2]), st(dil_s[2]))
```

```python
import functools

import jax
import jax.numpy as jnp
from jax import lax
from jax.experimental import pallas as pl
from jax.experimental.pallas import tpu as pltpu

F32 = jnp.float32
BF16 = jnp.bfloat16
HIGHEST = lax.Precision.HIGHEST

D_MODEL = 1024
PAGE_SIZE = 128
HEAD_DIM = 64
GLA_HEADS = 4
GLA_DK = 64
GLA_DV = 128
GLA_LOWRANK = 16
GLA_TAU = 16.0
GLA_CHUNK = 64
GLA_SUB = 16
NSA_HEADS = 8
NSA_KV_HEADS = 2
NSA_GROUP = NSA_HEADS // NSA_KV_HEADS
CMP_LEN = 32
CMP_STRIDE = 16
SLC_BLOCK = 64
SLC_TOPK = 16
NSA_WINDOW = 512
DIL_PAIRS = ((128, 1), (512, 4), (2048, 16))
N_DIL = 3
DIL_HEADS = 8
DIL_KEYS = 129
D_FF = 4 * D_MODEL
PLE_DIM = 256
ROPE_THETA = 10000.0
EPS = 1e-6
NEG = -1e30
BIG = 1e30
TINY = 1e-20
REMOVED = -3e38

LANES = 128
TOK_PAD = 8
NSA_KEY_TILE = 512
NSA_Q_BLOCK = 256
VMEM_LIMIT = 56 * 1024 * 1024
QSCALE = HEAD_DIM ** -0.5

A_GQ, A_GK, A_GV, A_GR, A_GA, A_NQ, A_NKV, A_NG, A_END = 0, 256, 512, 1024, 1536, 1664, 2176, 2944, 3072
G_Q, G_K, G_V, G_R, G_LG, G_END = 0, 256, 512, 1024, 1536, 1792


def _cparams(*sem):
    return pltpu.CompilerParams(dimension_semantics=sem, vmem_limit_bytes=VMEM_LIMIT)


def _rms_rows(x, g):
    return x * lax.rsqrt(jnp.mean(x * x, axis=-1, keepdims=True) + EPS) * g


def _cols(x):
    return [x[:, c * LANES:(c + 1) * LANES] for c in range(x.shape[1] // LANES)]


def _rms_heads(x, g):
    out = []
    for c, xc in enumerate(_cols(x)):
        low = lax.broadcasted_iota(jnp.int32, xc.shape, 1) < HEAD_DIM
        sq = xc * xc
        s_lo = jnp.sum(jnp.where(low, sq, 0.0), axis=-1, keepdims=True)
        s_hi = jnp.sum(jnp.where(low, 0.0, sq), axis=-1, keepdims=True)
        ms = jnp.where(low, s_lo, s_hi) * (1.0 / HEAD_DIM)
        out.append(xc * lax.rsqrt(ms + EPS) * g[:, c * LANES:(c + 1) * LANES])
    return jnp.concatenate(out, axis=1) if len(out) > 1 else out[0]


def _rope(x, cs, sn):
    out = []
    for xc in _cols(x):
        lane = lax.broadcasted_iota(jnp.int32, xc.shape, 1)
        swapped = jnp.where((lane & 32) == 0, pltpu.roll(xc, LANES - 32, axis=1), pltpu.roll(xc, 32, axis=1))
        out.append(xc * cs + swapped * sn)
    return jnp.concatenate(out, axis=1) if len(out) > 1 else out[0]


def _dot(a, b):
    return jnp.dot(a, b, preferred_element_type=F32)


def _dot_nt(a, b):
    return lax.dot_general(a, b, (((1,), (1,)), ((), ())), preferred_element_type=F32)


def _dot_tn(a, b):
    return lax.dot_general(a, b, (((0,), (0,)), ((), ())), preferred_element_type=F32)


def _softmax_full(s, mask):
    s = jnp.where(mask, s, NEG)
    m = jnp.max(s, axis=-1, keepdims=True)
    p = jnp.where(mask, jnp.exp(s - m), 0.0)
    l = jnp.sum(p, axis=-1, keepdims=True)
    return p, l


def _flash_step(state, s, mask, v):
    m, l, acc = state
    s = jnp.where(mask, s, NEG)
    m_new = jnp.maximum(m, jnp.max(s, axis=-1, keepdims=True))
    alpha = jnp.exp(m - m_new)
    p = jnp.where(mask, jnp.exp(s - m_new), 0.0)
    l = alpha * l + jnp.sum(p, axis=-1, keepdims=True)
    acc = alpha * acc + _dot(p.astype(BF16), v)
    return m_new, l, acc


def _flash_step_bias(state, s, v):
    m, l, acc = state
    m_new = jnp.maximum(m, jnp.max(s, axis=-1, keepdims=True))
    alpha = jnp.exp(m - m_new)
    p = jnp.exp(s - m_new)
    l = alpha * l + jnp.sum(p, axis=-1, keepdims=True)
    acc = alpha * acc + _dot(p.astype(BF16), v)
    return m_new, l, acc


def _proj_a_kernel(x_ref, gmix_ref, w_ref, wa2_ref, ba_ref, gqk_ref, cs_ref, sn_ref,
                   gla_ref, qc_ref, qr_ref, gates_ref, rc_ref, rs_ref, rw_ref, sbf_ref, wbf_ref,
                   rct_ref, rst_ref, rwt_ref):
    hn = _rms_rows(x_ref[...], gmix_ref[...])
    z = _dot(hn.astype(BF16), w_ref[...])
    cs, sn = cs_ref[...], sn_ref[...]
    gla_ref[:, G_Q:G_K] = z[:, A_GQ:A_GK] * (GLA_DK ** -0.5)
    gla_ref[:, G_K:G_LG] = z[:, A_GK:A_GA]
    pre = _dot(z[:, A_GA:A_NQ].astype(BF16), wa2_ref[...]) + ba_ref[...]
    gla_ref[:, G_LG:G_END] = jax.nn.log_sigmoid(pre) / GLA_TAU
    qn = _rms_heads(z[:, A_NQ:A_NKV], gqk_ref[0:1, :])
    qc_ref[...] = (qn * QSCALE).astype(BF16)
    qr_ref[...] = (_rope(qn, cs, sn) * QSCALE).astype(BF16)
    gates_ref[...] = jax.nn.sigmoid(z[:, A_NG:A_END])
    kv = [z[:, A_NKV + i * LANES:A_NKV + (i + 1) * LANES] for i in range(6)]
    rc_ref[:, 0:LANES] = _rms_heads(kv[0], gqk_ref[1:2, 0:LANES])
    rc_ref[:, LANES:] = kv[1]
    ks = _rope(_rms_heads(kv[2], gqk_ref[2:3, 0:LANES]), cs, sn)
    rs_ref[:, 0:LANES] = ks
    rs_ref[:, LANES:] = kv[3]
    kw = _rope(_rms_heads(kv[4], gqk_ref[3:4, 0:LANES]), cs, sn)
    rw_ref[:, 0:LANES] = kw
    rw_ref[:, LANES:] = kv[5]
    sbf_ref[:, 0:LANES] = ks.astype(BF16)
    sbf_ref[:, LANES:] = kv[3].astype(BF16)
    wbf_ref[:, 0:LANES] = kw.astype(BF16)
    wbf_ref[:, LANES:] = kv[5].astype(BF16)
    for src, dst in ((rc_ref, rct_ref), (rs_ref, rst_ref), (rw_ref, rwt_ref)):
        dst[...] = src[...].T


def _const_spec(shape):
    nd = len(shape)
    return pl.BlockSpec(shape, lambda *_: (0,) * nd)


def _proj_a(x, gmix, w, wa2, ba, gqk, cs, sn, tm):
    n = x.shape[0]
    tiles_per_seq = cs.shape[0] // tm
    row = lambda width: pl.BlockSpec((tm, width), lambda i: (i, 0))
    tab = pl.BlockSpec((tm, LANES), lambda i: (i % tiles_per_seq, 0))
    widths = (G_END, 512, 512, LANES, 256, 256, 256, 256, 256)
    dtypes = (F32, BF16, BF16, F32, F32, F32, F32, BF16, BF16)
    return pl.pallas_call(
        _proj_a_kernel,
        grid=(n // tm,),
        in_specs=[row(D_MODEL), _const_spec(gmix.shape), _const_spec(w.shape), _const_spec(wa2.shape),
                  _const_spec(ba.shape), _const_spec(gqk.shape), tab, tab],
        out_specs=[row(wd) for wd in widths]
                  + [pl.BlockSpec((None, 256, tm), lambda i: (i // tiles_per_seq, 0, i % tiles_per_seq))] * 3,
        out_shape=[jax.ShapeDtypeStruct((n, wd), dt) for wd, dt in zip(widths, dtypes)]
                  + [jax.ShapeDtypeStruct((n // cs.shape[0], 256, cs.shape[0]), F32)] * 3,
        compiler_params=_cparams("parallel"),
        name="proj_a",
    )(x, gmix, w, wa2, ba, gqk, cs, sn)


def _gla_kernel(gla_ref, s0_ref, gnorm_ref, o_ref, sout_ref, s_scr, *, chunk, sub, n_chunks, n_valid):
    i = pl.program_id(1)
    n_pairs = GLA_HEADS // 2
    n_sub = chunk // sub
    dk2, dv2 = 2 * GLA_DK, 2 * GLA_DV

    @pl.when(i == 0)
    def _():
        s_scr[...] = jnp.zeros(s_scr.shape, F32)
        for h in range(GLA_HEADS):
            p, e = divmod(h, 2)
            s_scr[p, e * GLA_DK:(e + 1) * GLA_DK, e * GLA_DV:(e + 1) * GLA_DV] = s0_ref[h]

    tri = (lax.broadcasted_iota(jnp.int32, (chunk, chunk), 0)
           >= lax.broadcasted_iota(jnp.int32, (chunk, chunk), 1)).astype(F32)
    low = lax.broadcasted_iota(jnp.int32, (chunk, dk2), 1) < GLA_DK
    low_sub = lax.broadcasted_iota(jnp.int32, (sub, dk2), 1) < GLA_DK
    t_sub = lax.broadcasted_iota(jnp.int32, (sub, 1), 0)
    row_blk = lax.broadcasted_iota(jnp.int32, (chunk, chunk), 0) // sub
    col_idx = lax.broadcasted_iota(jnp.int32, (chunk, chunk), 1)
    diag_blocks = ((lax.broadcasted_iota(jnp.int32, (dk2, dv2), 0) < GLA_DK)
                   == (lax.broadcasted_iota(jnp.int32, (dk2, dv2), 1) < GLA_DV))
    eye = (lax.broadcasted_iota(jnp.int32, (dk2, dk2), 0) == lax.broadcasted_iota(jnp.int32, (dk2, dk2), 1))
    gnorm = gnorm_ref[...]

    def do_chunk(c, carry):
        rows = pl.ds(pl.multiple_of(c * chunk, chunk), chunk)
        b_all = jnp.dot(tri, gla_ref[rows, G_LG:G_END], precision=HIGHEST, preferred_element_type=F32)
        for p in range(n_pairs):
            q2 = gla_ref[rows, G_Q + p * dk2:G_Q + (p + 1) * dk2]
            k2 = gla_ref[rows, G_K + p * dk2:G_K + (p + 1) * dk2]
            v2 = gla_ref[rows, G_V + p * dv2:G_V + (p + 1) * dv2]
            b2 = b_all[:, p * dk2:(p + 1) * dk2]
            s_pair = s_scr[p]
            o2 = _dot((q2 * jnp.exp(b2)).astype(BF16), s_pair.astype(BF16))

            if n_sub > 1:
                bref_rows = jnp.concatenate(
                    [b2[0:sub]] + [jnp.broadcast_to(b2[i * sub - 1:i * sub], (sub, dk2)) for i in range(1, n_sub)], axis=0)
                qt = q2 * jnp.exp(jnp.minimum(b2 - bref_rows, 0.0))
                qt_even = jnp.where(low, qt, 0.0).astype(BF16)
                qt_odd = jnp.where(low, 0.0, qt).astype(BF16)
                a_even = jnp.zeros((chunk, chunk), F32)
                a_odd = jnp.zeros((chunk, chunk), F32)
                for i in range(1, n_sub):
                    kt = (k2 * jnp.exp(jnp.minimum(b2[i * sub - 1:i * sub] - b2, 0.0))).astype(BF16)
                    take = (row_blk == i) & (col_idx < i * sub)
                    a_even = jnp.where(take, _dot_nt(qt_even, kt), a_even)
                    a_odd = jnp.where(take, _dot_nt(qt_odd, kt), a_odd)
                o2 = o2 + jnp.concatenate([_dot(a_even.astype(BF16), v2[:, 0:GLA_DV].astype(BF16)),
                                           _dot(a_odd.astype(BF16), v2[:, GLA_DV:].astype(BF16))], axis=1)

            diag = []
            for i in range(n_sub):
                sl = slice(i * sub, (i + 1) * sub)
                qs, ks, bs, vs = q2[sl], k2[sl], b2[sl], v2[sl]
                od = jnp.zeros((sub, dv2), F32)
                for jj in range(max(0, min(sub, n_valid - i * sub))):
                    w = qs * ks[jj:jj + 1] * jnp.exp(jnp.minimum(bs - bs[jj:jj + 1], 0.0))
                    keep = t_sub >= jj
                    a_e = jnp.where(keep, jnp.sum(jnp.where(low_sub, w, 0.0), axis=-1, keepdims=True), 0.0)
                    a_o = jnp.where(keep, jnp.sum(jnp.where(low_sub, 0.0, w), axis=-1, keepdims=True), 0.0)
                    od = od + jnp.concatenate([a_e * vs[jj:jj + 1, 0:GLA_DV], a_o * vs[jj:jj + 1, GLA_DV:]], axis=1)
                diag.append(od)
            o2 = o2 + (jnp.concatenate(diag, axis=0) if n_sub > 1 else diag[0])

            b_last = b2[chunk - 1:chunk, :]
            kdec = k2 * jnp.exp(b_last - b2)
            upd = _dot_tn(kdec.astype(BF16), v2.astype(BF16))
            decay_col = jnp.sum(jnp.where(eye, jnp.exp(b_last), 0.0), axis=1, keepdims=True)
            s_scr[p] = decay_col * s_pair + jnp.where(diag_blocks, upd, 0.0)
            for e in range(2):
                h = 2 * p + e
                on = _rms_rows(o2[:, e * GLA_DV:(e + 1) * GLA_DV], gnorm)
                r = gla_ref[rows, G_R + h * GLA_DV:G_R + (h + 1) * GLA_DV]
                o_ref[rows, h * GLA_DV:(h + 1) * GLA_DV] = (on * (r * jax.nn.sigmoid(r))).astype(o_ref.dtype)
        return carry
    lax.fori_loop(0, n_chunks, do_chunk, 0)

    @pl.when(i == pl.num_programs(1) - 1)
    def _():
        for h in range(GLA_HEADS):
            p, e = divmod(h, 2)
            sout_ref[h] = s_scr[p, e * GLA_DK:(e + 1) * GLA_DK, e * GLA_DV:(e + 1) * GLA_DV]


def _gla(gla, s0, gnorm, *, nb, rows_per_seq, chunk, n_valid, out_dtype):
    step_rows = min(rows_per_seq, 8 * chunk)
    steps = rows_per_seq // step_rows
    kern = functools.partial(_gla_kernel, chunk=chunk, sub=min(chunk, GLA_SUB), n_chunks=step_rows // chunk,
                             n_valid=n_valid)
    return pl.pallas_call(
        kern,
        grid=(nb, steps),
        in_specs=[pl.BlockSpec((step_rows, G_END), lambda b, i: (b * steps + i, 0)),
                  pl.BlockSpec((None, GLA_HEADS, GLA_DK, GLA_DV), lambda b, i: (b, 0, 0, 0)),
                  _const_spec(gnorm.shape)],
        out_specs=[pl.BlockSpec((step_rows, GLA_HEADS * GLA_DV), lambda b, i: (b * steps + i, 0)),
                   pl.BlockSpec((None, GLA_HEADS, GLA_DK, GLA_DV), lambda b, i: (b, 0, 0, 0))],
        out_shape=[jax.ShapeDtypeStruct((nb * rows_per_seq, GLA_HEADS * GLA_DV), out_dtype),
                   jax.ShapeDtypeStruct((nb, GLA_HEADS, GLA_DK, GLA_DV), F32)],
        scratch_shapes=[pltpu.VMEM((GLA_HEADS // 2, 2 * GLA_DK, 2 * GLA_DV), F32)],
        compiler_params=_cparams("parallel", "arbitrary"),
        name="gla",
    )(gla, s0, gnorm)


def _compress_kernel(*refs, n_parts, n_prefetch, transposed_pages):
    refs = refs[n_prefetch:]
    x_refs = refs[:n_parts]
    row_w = 2 * LANES
    if transposed_pages:
        wk_ref, wv_ref, pek_ref, pev_ref, kc_ref, vc_ref, sh_scr, page_scr, x_scr = refs[n_parts:]
        mp = PAGE_SIZE // CMP_STRIDE
        m = n_parts * mp
        for pg, r in enumerate(x_refs):
            page = r[...].reshape(row_w, PAGE_SIZE).T
            for half in range(2):
                page_scr[half] = page[:, half * LANES:(half + 1) * LANES]
            for p in range(CMP_STRIDE):
                for half in range(2):
                    x_scr[p, pg * mp:(pg + 1) * mp, half * LANES:(half + 1) * LANES] = (
                        page_scr[half, pl.ds(p, mp, stride=CMP_STRIDE), :])
        chunk_rows = lambda p, off: x_scr[p, :, off:off + LANES]
    else:
        wk_ref, wv_ref, pek_ref, pev_ref, kc_ref, vc_ref, sh_scr = refs[n_parts:]
        x = x_refs[0][...]
        m = x.shape[0]
        chunk_rows = lambda p, off: x[:, p * row_w + off:p * row_w + off + LANES]

    def branch(off, w_ref, pe_ref, out_ref):
        lo = jnp.zeros((m, LANES), F32)
        hi = jnp.zeros((m, LANES), F32)
        for p in range(CMP_STRIDE):
            xp = chunk_rows(p, off)
            lo = lo + _dot((xp + pe_ref[p:p + 1, :]).astype(BF16), w_ref[p])
            hi = hi + _dot((xp + pe_ref[CMP_STRIDE + p:CMP_STRIDE + p + 1, :]).astype(BF16), w_ref[CMP_STRIDE + p])
        sh_scr[pl.ds(0, m), :] = hi
        sh_scr[pl.ds(m, 8), :] = jnp.zeros((8, LANES), F32)
        out_ref[...] = (lo + sh_scr[pl.ds(1, m), :]).astype(out_ref.dtype)
    branch(0, wk_ref, pek_ref, kc_ref)
    branch(LANES, wv_ref, pev_ref, vc_ref)


def _compress_call(n_parts, m, grid, x_specs, out_map, out_rows, prefetch, transposed_pages):
    kern = functools.partial(_compress_kernel, n_parts=n_parts, n_prefetch=prefetch, transposed_pages=transposed_pages)
    w_shape = (CMP_LEN, LANES, LANES)
    pe_shape = (CMP_LEN, LANES)
    in_specs = list(x_specs) + [_const_spec(w_shape), _const_spec(w_shape), _const_spec(pe_shape), _const_spec(pe_shape)]
    out_specs = [pl.BlockSpec((m, LANES), out_map), pl.BlockSpec((m, LANES), out_map)]
    scratch = [pltpu.VMEM((m + 8, LANES), F32)]
    if transposed_pages:
        scratch += [pltpu.VMEM((2, PAGE_SIZE, LANES), F32), pltpu.VMEM((CMP_STRIDE, m, 2 * LANES), F32)]
    gs = pltpu.PrefetchScalarGridSpec(num_scalar_prefetch=prefetch, grid=grid, in_specs=in_specs, out_specs=out_specs,
                                      scratch_shapes=scratch)
    return pl.pallas_call(
        kern, grid_spec=gs,
        out_shape=[jax.ShapeDtypeStruct((out_rows, LANES), BF16)] * 2,
        compiler_params=_cparams("parallel"),
        name="nsa_compress",
    )


def _compress_prompt(rows_c, nb, seq, wk, wv, pek, pev):
    m = seq // CMP_STRIDE
    x = rows_c.reshape(-1, CMP_STRIDE * 2 * LANES)
    call = _compress_call(1, m, (nb,), [pl.BlockSpec((m, CMP_STRIDE * 2 * LANES), lambda b: (b, 0))],
                          lambda b: (b, 0), nb * m, 0, False)
    return call(x, wk, wv, pek, pev)


def _compress_pages(pool_t, page_table, wk, wv, pek, pev):
    nb, n_pages = page_table.shape
    specs = [pl.BlockSpec((None,) + pool_t.shape[1:], functools.partial(lambda pg, b, pt: (pt[b, pg], 0, 0, 0, 0), pg))
             for pg in range(n_pages)]
    m = n_pages * (PAGE_SIZE // CMP_STRIDE)
    call = _compress_call(n_pages, m, (nb,), specs, lambda b, pt: (b, 0), nb * m, 1, True)
    return call(page_table, *([pool_t] * n_pages), wk, wv, pek, pev)


def _topk_mask(imp, qpos, n_slc):
    blk = lax.broadcasted_iota(jnp.int32, imp.shape, 1)
    cur = qpos // SLC_BLOCK
    val = jnp.where((blk == cur) | (blk == 0), BIG, jnp.where(blk <= cur, imp, NEG))
    val = jnp.where(blk < n_slc, val, REMOVED)
    sel = jnp.zeros(imp.shape, jnp.bool_)
    for _ in range(min(SLC_TOPK, n_slc)):
        mx = jnp.max(val, axis=-1, keepdims=True)
        idx = jnp.min(jnp.where(val == mx, blk, imp.shape[1]), axis=-1, keepdims=True)
        pick = blk == idx
        sel = sel | pick
        val = jnp.where(pick, REMOVED, val)
    return sel.astype(BF16)


def _topk_mask_by_rank(imp, qpos, n_slc):
    blk = lax.broadcasted_iota(jnp.int32, imp.shape, 1)
    cur = qpos // SLC_BLOCK
    val = jnp.where((blk == cur) | (blk == 0), BIG, jnp.where(blk <= cur, imp, NEG))
    rank = jnp.zeros(imp.shape, jnp.int32)
    for i in range(n_slc):
        col = val[:, i:i + 1]
        rank = rank + ((col > val) | ((col == val) & (blk > i))).astype(jnp.int32)
    return ((rank < min(SLC_TOPK, n_slc)) & (blk < n_slc)).astype(BF16)


def _block_expand(n_blocks_pad, k0, n_keys):
    blk = lax.broadcasted_iota(jnp.int32, (n_blocks_pad, n_keys), 0)
    col = lax.broadcasted_iota(jnp.int32, (n_blocks_pad, n_keys), 1)
    return (blk == (k0 + col) // SLC_BLOCK).astype(BF16)


def _flash_init(m_rows):
    return (jnp.full((m_rows, 1), NEG, F32), jnp.zeros((m_rows, 1), F32), jnp.zeros((m_rows, HEAD_DIM), F32))


def _nsa_prompt_kernel(qc_ref, qr_ref, gates_ref, kc_ref, vc_ref, ks_ref, kw_ref, mcs_ref, o_ref,
                       qa_scr, m_scr, l_scr, acc_scr, *, qb, seq, key_tile):
    t0 = pl.program_id(1) * qb
    qpos = t0 + lax.broadcasted_iota(jnp.int32, (qb, 1), 0)
    n_cmp_pad = kc_ref.shape[0]
    n_slc = seq // SLC_BLOCK
    hd = HEAD_DIM

    rows = NSA_HEADS * qb
    lane2 = lax.broadcasted_iota(jnp.int32, (qb, 2 * hd), 1)

    def stack_heads(q_ref):
        parts = []
        for h in range(NSA_HEADS):
            g = h // NSA_GROUP
            pair = q_ref[:, (h // 2) * 2 * hd:(h // 2 + 1) * 2 * hd].astype(F32)
            if h % 2 != g:
                pair = pltpu.roll(pair, hd, axis=1)
            parts.append(jnp.where((lane2 >= g * hd) & (lane2 < (g + 1) * hd), pair, 0.0).astype(BF16))
        return jnp.concatenate(parts, axis=0)

    def row_softmax(s, mask):
        s = jnp.where(mask, s, NEG)
        p = jnp.where(mask, jnp.exp(s - jnp.max(s, axis=-1, keepdims=True)), 0.0)
        return p, jnp.maximum(jnp.sum(p, axis=-1, keepdims=True), TINY)

    q_tok = t0 + lax.broadcasted_iota(jnp.int32, (rows, 1), 0) % qb
    qr_st = stack_heads(qr_ref)

    ncol = lax.broadcasted_iota(jnp.int32, (rows, n_cmp_pad), 1)
    p, l_cmp = row_softmax(_dot_nt(stack_heads(qc_ref), kc_ref[...]), (ncol * CMP_STRIDE + CMP_LEN - 1) <= q_tok)
    p = p * (1.0 / l_cmp)
    o_cmp = _dot(p.astype(BF16), vc_ref[...])
    psum = []
    for g in range(NSA_KV_HEADS):
        blocks = [p[(g * NSA_GROUP + j) * qb:(g * NSA_GROUP + j + 1) * qb] for j in range(NSA_GROUP)]
        psum.append(functools.reduce(lambda a, b: a + b, blocks))
    imp = jnp.dot(jnp.concatenate(psum, axis=0), mcs_ref[...], precision=HIGHEST, preferred_element_type=F32)
    sel = _topk_mask(imp, jnp.concatenate([qpos] * NSA_KV_HEADS, axis=0), n_slc)

    n_blk = sel.shape[1]
    qa_scr[:, 0:2 * hd] = qr_st
    for h in range(NSA_HEADS):
        g = h // NSA_GROUP
        qa_scr[h * qb:(h + 1) * qb, 2 * hd:] = 1.0 - sel[g * qb:(g + 1) * qb]
    m_scr[...] = jnp.full((rows, LANES), NEG, F32)
    l_scr[...] = jnp.zeros((rows, LANES), F32)
    acc_scr[...] = jnp.zeros((rows, 2 * hd), F32)

    def tile(k0, causal):
        kv = ks_ref[pl.ds(k0, key_tile), :]
        key_blk = (k0 + lax.broadcasted_iota(jnp.int32, (key_tile, n_blk), 0)) // SLC_BLOCK
        own_blk = key_blk == lax.broadcasted_iota(jnp.int32, (key_tile, n_blk), 1)
        k_aug = jnp.concatenate([kv[:, 0:2 * hd], jnp.where(own_blk, NEG, 0.0).astype(BF16)], axis=1)
        s = _dot_nt(qa_scr[...], k_aug)
        if causal:
            q_row = t0 + lax.broadcasted_iota(jnp.int32, (rows, key_tile), 0) % qb
            s = jnp.where(k0 + lax.broadcasted_iota(jnp.int32, (rows, key_tile), 1) <= q_row, s, NEG)
        m_old = m_scr[...]
        m_new = jnp.maximum(m_old, jnp.max(s, axis=-1, keepdims=True))
        alpha = jnp.exp(m_old - m_new)
        p = jnp.exp(s - jnp.concatenate([m_new] * (key_tile // LANES), axis=1))
        l_scr[...] = alpha * l_scr[...] + jnp.sum(p, axis=-1, keepdims=True)
        acc_scr[...] = alpha * acc_scr[...] + _dot(p.astype(BF16), kv[:, 2 * hd:4 * hd])
        m_scr[...] = m_new

    n_full = t0 // key_tile

    def full_tile(kt, carry):
        tile(pl.multiple_of(kt * key_tile, key_tile), False)
        return carry
    lax.fori_loop(0, n_full, full_tile, 0)
    tile(pl.multiple_of(n_full * key_tile, key_tile), True)

    wlen = min(seq, NSA_WINDOW + qb)
    start = pl.multiple_of(jnp.clip(t0 - NSA_WINDOW, 0, seq - wlen), qb)
    band = kw_ref[pl.ds(start, wlen), :]
    kpos = start + lax.broadcasted_iota(jnp.int32, (rows, wlen), 1)
    p, l_win = row_softmax(_dot_nt(qr_st, band[:, 0:2 * hd]), (kpos <= q_tok) & (kpos > q_tok - NSA_WINDOW))
    o_win = _dot(p.astype(BF16), band[:, 2 * hd:4 * hd])
    for h in range(NSA_HEADS):
        g = h // NSA_GROUP
        rs, ls = slice(h * qb, (h + 1) * qb), slice(g * hd, (g + 1) * hd)
        o_w = o_win[rs, ls] / l_win[rs]
        o_slc = acc_scr[rs, ls] / jnp.maximum(l_scr[rs, 0:hd], TINY)
        gate = gates_ref[:, 3 * h:3 * h + 3]
        o = gate[:, 0:1] * o_cmp[rs, ls] + gate[:, 1:2] * o_slc + gate[:, 2:3] * o_w
        o_ref[:, h * hd:(h + 1) * hd] = o.astype(o_ref.dtype)


def _nsa_prompt(qc, qr, gates, kc, vc, ks, kw, mcs, *, nb, seq, qb):
    nq = seq // qb
    n_cmp_pad = seq // CMP_STRIDE
    key_tile = min(seq, NSA_KEY_TILE)
    n_blk = _round_up(seq // SLC_BLOCK, LANES)
    kern = functools.partial(_nsa_prompt_kernel, qb=qb, seq=seq, key_tile=key_tile)
    qspec = lambda wd: pl.BlockSpec((qb, wd), lambda b, i: (b * nq + i, 0))
    seqspec = lambda rows, wd: pl.BlockSpec((rows, wd), lambda b, i: (b, 0))
    return pl.pallas_call(
        kern,
        grid=(nb, nq),
        in_specs=[qspec(512), qspec(512), qspec(LANES), seqspec(n_cmp_pad, LANES), seqspec(n_cmp_pad, LANES),
                  seqspec(seq, 256), seqspec(seq, 256), _const_spec(mcs.shape)],
        out_specs=qspec(512),
        out_shape=jax.ShapeDtypeStruct((nb * seq, 512), BF16),
        scratch_shapes=[pltpu.VMEM((NSA_HEADS * qb, 2 * HEAD_DIM + n_blk), BF16),
                        pltpu.VMEM((NSA_HEADS * qb, LANES), F32), pltpu.VMEM((NSA_HEADS * qb, LANES), F32),
                        pltpu.VMEM((NSA_HEADS * qb, 2 * HEAD_DIM), F32)],
        compiler_params=_cparams("parallel", "arbitrary"),
        name="nsa_prompt",
    )(qc, qr, gates, kc, vc, ks, kw, mcs)


def _attend_transposed(q_bf, q_f32, kt, vt, mask, newk, newv, mask_new, n_tok):
    m_rows = q_bf.shape[0]
    s = jnp.where(mask, _dot(q_bf, kt), NEG)
    lane = lax.broadcasted_iota(jnp.int32, (m_rows, TOK_PAD), 1)
    sn = jnp.full((m_rows, TOK_PAD), NEG, F32)
    for t in range(n_tok):
        sn = jnp.where(lane == t, jnp.sum(q_f32 * newk[t:t + 1], axis=-1, keepdims=True), sn)
    sn = jnp.where(mask_new, sn, NEG)
    m = jnp.maximum(jnp.max(s, axis=-1, keepdims=True), jnp.max(sn, axis=-1, keepdims=True))
    p = jnp.where(mask, jnp.exp(s - m), 0.0)
    pn = jnp.where(mask_new, jnp.exp(sn - m), 0.0)
    l = jnp.sum(p, axis=-1, keepdims=True) + jnp.sum(pn, axis=-1, keepdims=True)
    o = _dot_nt(p.astype(BF16), vt)
    for t in range(n_tok):
        o = o + pn[:, t:t + 1] * newv[t:t + 1]
    return o / jnp.maximum(l, TINY)


def _nsa_sample_kernel(*refs, n_pages, n_tok):
    (qc_ref, qr_ref, gates_ref, kc_ref, vc_ref) = refs[1:6]
    page_refs = refs[6:6 + n_pages]
    news_ref, win_ref, neww_ref, newwt_ref, mcs_ref, amat_ref, o_ref, owin_ref = refs[6 + n_pages:]
    hd = HEAD_DIM
    past = n_pages * PAGE_SIZE
    win = win_ref.shape[-1]
    m_rows = qc_ref.shape[1]
    tok = lax.broadcasted_iota(jnp.int32, (m_rows, 1), 0) % n_tok
    qpos = past + tok
    n_cmp_pad = kc_ref.shape[0]
    n_slc = (past + n_tok + SLC_BLOCK - 1) // SLC_BLOCK

    ncol = lax.broadcasted_iota(jnp.int32, (m_rows, n_cmp_pad), 1)
    cmask = ((ncol * CMP_STRIDE + CMP_LEN - 1) <= qpos) & (ncol < n_cmp_pad - 1)
    new_pos = past + lax.broadcasted_iota(jnp.int32, (m_rows, TOK_PAD), 1)
    new_ok = (new_pos <= qpos) & (new_pos < past + n_tok)
    wpos = past - win + lax.broadcasted_iota(jnp.int32, (m_rows, win), 1)
    wmask = (wpos <= qpos) & (wpos > qpos - NSA_WINDOW) & (wpos >= 0)
    wmask_new = new_ok & (new_pos > qpos - NSA_WINDOW)

    o_cmp, imps = [], []
    for g in range(NSA_KV_HEADS):
        ksl = slice(g * hd, (g + 1) * hd)
        p, l = _softmax_full(_dot_nt(qc_ref[g], kc_ref[:, ksl]), cmask)
        p = p / jnp.maximum(l, TINY)
        o_cmp.append(_dot(p.astype(BF16), vc_ref[:, ksl]))
        psum = jnp.dot(amat_ref[...], p, precision=HIGHEST, preferred_element_type=F32)
        imps.append(jnp.dot(psum, mcs_ref[...], precision=HIGHEST, preferred_element_type=F32))
    sel_all = _topk_mask_by_rank(jnp.concatenate(imps, axis=0), jnp.concatenate([qpos] * NSA_KV_HEADS, axis=0), n_slc)
    expand = _block_expand(LANES, 0, past)
    new_blk = lax.broadcasted_iota(jnp.int32, (m_rows, LANES), 1) == past // SLC_BLOCK

    for g in range(NSA_KV_HEADS):
        ksl = slice(g * hd, (g + 1) * hd)
        vsl = slice(LANES + g * hd, LANES + (g + 1) * hd)
        qr = qr_ref[g]
        qr32 = qr.astype(F32)
        sel = sel_all[g * m_rows:(g + 1) * m_rows]
        sel_new = jnp.sum(jnp.where(new_blk, sel.astype(F32), 0.0), axis=-1, keepdims=True) > 0.5
        kt = jnp.concatenate([r[0, g].astype(BF16) for r in page_refs], axis=1)
        vt = jnp.concatenate([r[1, g].astype(BF16) for r in page_refs], axis=1)
        o_slc = _attend_transposed(qr, qr32, kt, vt, _dot(sel, expand) > 0.5,
                                   news_ref[:, ksl], news_ref[:, vsl], new_ok & sel_new, n_tok)
        o_win = _attend_transposed(qr, qr32, win_ref[0, g].astype(BF16), win_ref[1, g].astype(BF16), wmask,
                                   neww_ref[:, ksl], neww_ref[:, vsl], wmask_new, n_tok)
        gate = gates_ref[g]
        o_ref[g] = gate[:, 0:1] * o_cmp[g] + gate[:, 1:2] * o_slc + gate[:, 2:3] * o_win

        lane = lax.broadcasted_iota(jnp.int32, (hd, LANES), 1)
        for kv in range(2):
            shifted = pltpu.roll(win_ref[kv, g], win - n_tok, axis=1)
            last = shifted[:, win - LANES:]
            for t in range(n_tok):
                last = jnp.where(lane == LANES - n_tok + t, newwt_ref[kv, g, :, t:t + 1], last)
            if win > LANES:
                owin_ref[kv, g, :, 0:win - LANES] = shifted[:, 0:win - LANES]
            owin_ref[kv, g, :, win - LANES:] = last


def _nsa_sample(page_table, qc, qr, gates, kc, vc, pool_s, news, win_t, neww, neww_t, mcs, amat, *, n_tok):
    nb, n_pages = page_table.shape
    m_rows = qc.shape[2]
    kern = functools.partial(_nsa_sample_kernel, n_pages=n_pages, n_tok=n_tok)
    per_b = lambda a: pl.BlockSpec((None,) + a.shape[1:], lambda b, pt: (b,) + (0,) * (a.ndim - 1))
    n_cmp_pad = kc.shape[0] // nb
    cmp_spec = pl.BlockSpec((n_cmp_pad, LANES), lambda b, pt: (b, 0))
    page_specs = [pl.BlockSpec((None,) + pool_s.shape[1:], functools.partial(lambda pg, b, pt: (pt[b, pg], 0, 0, 0, 0), pg))
                  for pg in range(n_pages)]
    gs = pltpu.PrefetchScalarGridSpec(
        num_scalar_prefetch=1, grid=(nb,),
        in_specs=[per_b(qc), per_b(qr), per_b(gates), cmp_spec, cmp_spec] + page_specs +
                 [per_b(news), per_b(win_t), per_b(neww), per_b(neww_t),
                  pl.BlockSpec(mcs.shape, lambda b, pt: (0, 0)), pl.BlockSpec(amat.shape, lambda b, pt: (0, 0))],
        out_specs=[per_b(qc), per_b(win_t)])
    return pl.pallas_call(
        kern, grid_spec=gs,
        out_shape=[jax.ShapeDtypeStruct((nb, NSA_KV_HEADS, m_rows, HEAD_DIM), F32),
                   jax.ShapeDtypeStruct(win_t.shape, F32)],
        compiler_params=_cparams("parallel"),
        name="nsa_sample",
    )(page_table, qc, qr, gates, kc, vc, *([pool_s] * n_pages), news, win_t, neww, neww_t, mcs, amat)


C_GROUP_COLS = 3 * DIL_HEADS * HEAD_DIM


def _proj_c_kernel(x_ref, gmix_ref, w_ref, gq_ref, gk_ref, cs_ref, sn_ref, *out_refs, tails):
    q_refs, r_refs, t_refs = out_refs[:N_DIL], out_refs[N_DIL:2 * N_DIL], out_refs[2 * N_DIL:]
    hn = _rms_rows(x_ref[...], gmix_ref[...]).astype(BF16)
    cs, sn = cs_ref[...], sn_ref[...]
    wd = DIL_HEADS * HEAD_DIM
    for g in range(N_DIL):
        z = _dot(hn, w_ref[:, g * C_GROUP_COLS:(g + 1) * C_GROUP_COLS])
        q = _rope(_rms_heads(z[:, 0:wd], gq_ref[g:g + 1, :]), cs, sn)
        q_refs[g][...] = (q * QSCALE).astype(BF16)
        r_refs[g][:, 0:wd] = _rope(_rms_heads(z[:, wd:2 * wd], gk_ref[g:g + 1, :]), cs, sn)
        r_refs[g][:, wd:] = z[:, 2 * wd:]
        if tails is not None:
            first, cols = tails[g]
            tm = x_ref.shape[0]

            in_window = pl.program_id(0) % tails[N_DIL] >= first

            @pl.when(in_window)
            def _():
                t_refs[g][...] = r_refs[g][tm - cols:tm, :].T

            @pl.when(jnp.logical_not(in_window))
            def _():
                t_refs[g][...] = jnp.zeros(t_refs[g].shape, F32)


def _proj_c(x, gmix, w, gq, gk, cs, sn, tm, with_tails):
    n = x.shape[0]
    seq = cs.shape[0]
    tiles_per_seq = seq // tm
    row = lambda width: pl.BlockSpec((tm, width), lambda i: (i, 0))
    tab = pl.BlockSpec((tm, LANES), lambda i: (i % tiles_per_seq, 0))
    wd = DIL_HEADS * HEAD_DIM
    out_specs = [row(wd)] * N_DIL + [row(2 * wd)] * N_DIL
    out_shape = [jax.ShapeDtypeStruct((n, wd), BF16)] * N_DIL + [jax.ShapeDtypeStruct((n, 2 * wd), F32)] * N_DIL
    tails = None
    if with_tails:
        tails = []
        for win, _ in DIL_PAIRS:
            rows = min(win, seq)
            cols = min(rows, tm)
            assert rows % cols == 0 and seq % tm == 0
            first = tiles_per_seq - max(rows // tm, 1)
            tails.append((first, cols))
            out_specs.append(pl.BlockSpec(
                (None, 2 * wd, cols),
                functools.partial(lambda first, i: (i // tiles_per_seq, 0, jnp.maximum(i % tiles_per_seq - first, 0)), first)))
            out_shape.append(jax.ShapeDtypeStruct((n // seq, 2 * wd, rows), F32))
        tails = tuple(tails) + (tiles_per_seq,)
    return pl.pallas_call(
        functools.partial(_proj_c_kernel, tails=tails),
        grid=(n // tm,),
        in_specs=[row(D_MODEL), _const_spec(gmix.shape), _const_spec(w.shape), _const_spec(gq.shape),
                  _const_spec(gk.shape), tab, tab],
        out_specs=out_specs,
        out_shape=out_shape,
        compiler_params=_cparams("arbitrary" if with_tails else "parallel"),
        name="proj_c",
    )(x, gmix, w, gq, gk, cs, sn)


def _dil_prompt_kernel(q_ref, kp_ref, vp_ref, kc_ref, vc_ref, o_ref, ml_ref, *, tq):
    i = pl.program_id(2)
    row = lax.broadcasted_iota(jnp.int32, (tq, 2 * tq), 0)
    col = lax.broadcasted_iota(jnp.int32, (tq, 2 * tq), 1)
    delta = tq + row - col
    mask = (delta >= 0) & (delta < DIL_KEYS) & ((col >= tq) | (i > 0))
    k = jnp.concatenate([kp_ref[...], kc_ref[...]], axis=0).astype(BF16)
    v = jnp.concatenate([vp_ref[...], vc_ref[...]], axis=0).astype(BF16)
    lane = lax.broadcasted_iota(jnp.int32, (tq, LANES), 1)
    ml = jnp.zeros((tq, LANES), F32)
    hd = HEAD_DIM
    for h in range(DIL_HEADS):
        hs = slice(h * hd, (h + 1) * hd)
        s = jnp.where(mask, _dot_nt(q_ref[:, hs], k[:, hs]), NEG)
        m = jnp.max(s, axis=-1, keepdims=True)
        p = jnp.where(mask, jnp.exp(s - m), 0.0)
        l = jnp.sum(p, axis=-1, keepdims=True)
        o_ref[:, hs] = _dot(p.astype(BF16), v[:, hs])
        ml = jnp.where(lane == h, m, jnp.where(lane == DIL_HEADS + h, l, ml))
    ml_ref[...] = ml


def _dil_prompt(q, rows, *, nb, seq, dil, tq):
    wd = DIL_HEADS * HEAD_DIM
    assert tq >= DIL_KEYS - 1 and seq % (dil * tq) == 0
    nu = seq // dil // tq
    qv = q.reshape(nb * seq // dil, dil * wd)
    rv = rows.reshape(nb * seq // dil, dil * 2 * wd)
    cur = lambda off: (lambda b, r, i: (b * nu + i, 2 * r + off))
    prev = lambda off: (lambda b, r, i: (b * nu + jnp.maximum(i - 1, 0), 2 * r + off))
    blk = lambda imap: pl.BlockSpec((tq, wd), imap)
    o, ml = pl.pallas_call(
        functools.partial(_dil_prompt_kernel, tq=tq),
        grid=(nb, dil, nu),
        in_specs=[blk(lambda b, r, i: (b * nu + i, r)), blk(prev(0)), blk(prev(1)), blk(cur(0)), blk(cur(1))],
        out_specs=[blk(lambda b, r, i: (b * nu + i, r)), pl.BlockSpec((tq, LANES), lambda b, r, i: (b * nu + i, r))],
        out_shape=[jax.ShapeDtypeStruct(qv.shape, F32), jax.ShapeDtypeStruct((qv.shape[0], dil * LANES), F32)],
        compiler_params=_cparams("parallel", "parallel", "arbitrary"),
        name="dil_prompt",
    )(qv, rv, rv, rv, rv)
    return o.reshape(nb * seq, wd), ml.reshape(nb * seq, LANES)


DIL_SAMPLE_HEADS = 4


def _dil_sample_kernel(q_ref, n0_ref, n1_ref, n2_ref, c0_ref, c1_ref, c2_ref, o_ref, oc0_ref, oc1_ref, oc2_ref,
                       *, n_tok):
    n_refs, c_refs, oc_refs = (n0_ref, n1_ref, n2_ref), (c0_ref, c1_ref, c2_ref), (oc0_ref, oc1_ref, oc2_ref)
    tp = lax.broadcasted_iota(jnp.int32, (1, TOK_PAD), 1)
    lane_t = lax.broadcasted_iota(jnp.int32, (HEAD_DIM, TOK_PAD), 1)
    for hl in range(q_ref.shape[1]):
        stats = [[None] * N_DIL for _ in range(n_tok)]
        for g, (w, dil) in enumerate(DIL_PAIRS):
            nk = n_refs[g][0, hl]
            nv = n_refs[g][1, hl]
            row = lax.broadcasted_iota(jnp.int32, (1, w), 1)
            n_ch = w // LANES
            q_cols = [q_ref[g, hl, :, t:t + 1] for t in range(n_tok)]
            q_wide = [jnp.broadcast_to(qc, (HEAD_DIM, LANES)) for qc in q_cols]
            shared = dil >= n_tok
            groups = [list(range(n_tok))] if shared else [[t] for t in range(n_tok)]
            lane_tok = lax.broadcasted_iota(jnp.int32, (HEAD_DIM, LANES), 1) & (dil - 1)
            row_tok = row & (dil - 1)
            for toks in groups:
                if shared:
                    q_sel = jnp.zeros((HEAD_DIM, LANES), F32)
                    for t in toks:
                        q_sel = jnp.where(lane_tok == t, q_wide[t], q_sel)
                else:
                    q_sel = q_wide[toks[0]]
                s_parts = [jnp.sum(c_refs[g][0, hl, :, c * LANES:(c + 1) * LANES] * q_sel, axis=0, keepdims=True)
                           for c in range(n_ch)]
                s_all = jnp.concatenate(s_parts, axis=1) if n_ch > 1 else s_parts[0]
                m_lanes = jnp.zeros((1, w), F32)
                any_valid = jnp.zeros((1, w), jnp.bool_)
                valids, pnews = [], []
                for t in toks:
                    valid = (row_tok == (t & (dil - 1))) & (w + t - row <= (DIL_KEYS - 1) * dil)
                    dn = t - tp
                    valid_n = (dn >= 0) & ((dn & (dil - 1)) == 0)
                    sn = jnp.where(valid_n, jnp.sum(nk * q_cols[t], axis=0, keepdims=True), NEG)
                    m = jnp.maximum(jnp.max(jnp.where(valid, s_all, NEG), axis=-1, keepdims=True),
                                    jnp.max(sn, axis=-1, keepdims=True))
                    m_lanes = jnp.where(valid, m, m_lanes)
                    any_valid = any_valid | valid
                    valids.append(valid)
                    pnews.append(jnp.where(valid_n, jnp.exp(sn - m), 0.0))
                    stats[t][g] = (None, m, None)
                p_all = jnp.where(any_valid, jnp.exp(s_all - m_lanes), 0.0)
                acc = jnp.zeros((HEAD_DIM, LANES), F32)
                for c in range(n_ch):
                    acc = acc + c_refs[g][1, hl, :, c * LANES:(c + 1) * LANES] * p_all[:, c * LANES:(c + 1) * LANES]
                for t, valid, pn in zip(toks, valids, pnews):
                    l = jnp.sum(jnp.where(valid, p_all, 0.0), axis=-1, keepdims=True) + jnp.sum(pn, axis=-1, keepdims=True)
                    own = acc if not shared else jnp.where(lane_tok == t, acc, 0.0)
                    o = jnp.sum(own, axis=-1, keepdims=True) + jnp.sum(nv * pn, axis=-1, keepdims=True)
                    stats[t][g] = (o, stats[t][g][1], l)
            lane = lax.broadcasted_iota(jnp.int32, (HEAD_DIM, LANES), 1)
            for kv, new in ((0, nk), (1, nv)):
                shifted = pltpu.roll(c_refs[g][kv, hl], w - n_tok, axis=1)
                last = shifted[:, w - LANES:]
                for t in range(n_tok):
                    last = jnp.where(lane == LANES - n_tok + t, new[:, t:t + 1], last)
                if w > LANES:
                    oc_refs[g][kv, hl, :, 0:w - LANES] = shifted[:, 0:w - LANES]
                oc_refs[g][kv, hl, :, w - LANES:] = last
        out = jnp.zeros((HEAD_DIM, TOK_PAD), F32)
        for t in range(n_tok):
            parts = stats[t]
            mx = functools.reduce(jnp.maximum, [m for _, m, _ in parts])
            num = sum(jnp.exp(m - mx) * o for o, m, _ in parts)
            den = sum(jnp.exp(m - mx) * l for _, m, l in parts)
            out = jnp.where(lane_t == t, num / den, out)
        o_ref[hl] = out


def _dil_sample(q, news, caches, *, n_tok):
    nb = q.shape[0]
    hs = DIL_SAMPLE_HEADS
    for g, (w, d) in enumerate(DIL_PAIRS):
        assert caches[g].shape[-1] == w and w // d == DIL_KEYS - 1
        assert d & (d - 1) == 0
    blk = lambda a: pl.BlockSpec((None, a.shape[1], hs) + a.shape[3:], lambda b, hh: (b, 0, hh, 0, 0))
    o_spec = pl.BlockSpec((None, hs, HEAD_DIM, TOK_PAD), lambda b, hh: (b, hh, 0, 0))
    return pl.pallas_call(
        functools.partial(_dil_sample_kernel, n_tok=n_tok),
        grid=(nb, DIL_HEADS // hs),
        in_specs=[blk(q)] + [blk(a) for a in news] + [blk(a) for a in caches],
        out_specs=[o_spec] + [blk(a) for a in caches],
        out_shape=[jax.ShapeDtypeStruct((nb, DIL_HEADS, HEAD_DIM, TOK_PAD), F32)]
                  + [jax.ShapeDtypeStruct(a.shape, F32) for a in caches],
        compiler_params=_cparams("parallel", "parallel"),
        name="dil_sample",
    )(q, *news, *caches)


FF_CHUNK = 1024


def _post_kernel(*refs, n_cat, combine):
    h_ref = refs[0]
    mix_refs = refs[1:1 + (2 * N_DIL if combine else n_cat)]
    (wo_ref, gmlp_ref, w1_ref, w2_ref, gple_ref, wg_ref, p_ref, wp_ref, out_ref) = refs[1 + len(mix_refs):1 + len(mix_refs) + 9]
    hd = HEAD_DIM
    if combine:
        comb_scr = refs[-1]
        o_refs, ml_refs = mix_refs[:N_DIL], mix_refs[N_DIL:]
        for h in range(DIL_HEADS):
            ms = [r[:, h:h + 1] for r in ml_refs]
            ls = [r[:, DIL_HEADS + h:DIL_HEADS + h + 1] for r in ml_refs]
            mx = functools.reduce(jnp.maximum, ms)
            ws = [jnp.exp(m - mx) for m in ms]
            num = sum(w * r[:, h * hd:(h + 1) * hd] for w, r in zip(ws, o_refs))
            den = sum(w * l for w, l in zip(ws, ls))
            comb_scr[:, h * hd:(h + 1) * hd] = (num / den).astype(BF16)
        y = _dot(comb_scr[...], wo_ref[...])
    else:
        y = None
        for k, r in enumerate(mix_refs):
            part = _dot(r[...].astype(BF16), wo_ref[k * r.shape[1]:(k + 1) * r.shape[1], :])
            y = part if y is None else y + part
    h1 = h_ref[...] + y
    hn = _rms_rows(h1, gmlp_ref[...]).astype(BF16)
    acc = jnp.zeros(h1.shape, F32)
    for c in range(D_FF // FF_CHUNK):
        u = _dot(hn, w1_ref[:, c * FF_CHUNK:(c + 1) * FF_CHUNK])
        acc = acc + _dot(jnp.square(jnp.maximum(u, 0.0)).astype(BF16), w2_ref[c * FF_CHUNK:(c + 1) * FF_CHUNK, :])
    h2 = h1 + acc
    gate = jax.nn.sigmoid(_dot(_rms_rows(h2, gple_ref[...]).astype(BF16), wg_ref[...]))
    out_ref[...] = h2 + gate * _dot(p_ref[...].astype(BF16), wp_ref[...])


def _post(h, mix, wo, gmlp, w1, w2, gple, wg, p, wp, *, combine, tm):
    n = h.shape[0]
    row = lambda a: pl.BlockSpec((tm, a.shape[1]), lambda i: (i, 0))
    weights = (wo, gmlp, w1, w2, gple, wg)
    wspec = lambda a: pl.BlockSpec(a.shape, lambda i: (0, 0), pipeline_mode=pl.Buffered(1))
    kern = functools.partial(_post_kernel, n_cat=len(mix), combine=combine)
    return pl.pallas_call(
        kern,
        grid=(n // tm,),
        in_specs=[row(h)] + [row(a) for a in mix] + [wspec(a) for a in weights] + [row(p), wspec(wp)],
        out_specs=row(h),
        out_shape=jax.ShapeDtypeStruct(h.shape, F32),
        scratch_shapes=[pltpu.VMEM((tm, DIL_HEADS * HEAD_DIM), BF16)] if combine else [],
        compiler_params=_cparams("parallel"),
        name="post",
    )(h, *mix, *weights, p, wp)


def _rope_tables(pos):
    half = HEAD_DIM // 2
    freq = ROPE_THETA ** (-jnp.arange(half, dtype=F32) / half)
    ang = pos.astype(F32)[:, None] * freq[None, :]
    cos, sin = jnp.cos(ang), jnp.sin(ang)
    return jnp.tile(jnp.concatenate([cos, cos], axis=1), (1, 2)), jnp.tile(jnp.concatenate([-sin, sin], axis=1), (1, 2))


def _cmp_to_slc(n_cmp, n_slc, rows_pad, cols_pad):
    cs = jnp.arange(n_cmp)[:, None] * CMP_STRIDE
    ss = jnp.arange(n_slc)[None, :] * SLC_BLOCK
    ov = jnp.maximum(jnp.minimum(cs + CMP_LEN, ss + SLC_BLOCK) - jnp.maximum(cs, ss), 0)
    return jnp.pad(ov.astype(F32) / CMP_LEN, ((0, rows_pad - n_cmp), (0, cols_pad - n_slc)))


def _round_up(x, m):
    return (x + m - 1) // m * m


def _block_diag2(w):
    z = jnp.zeros_like(w)
    return jnp.concatenate([jnp.concatenate([w, z], axis=2), jnp.concatenate([z, w], axis=2)], axis=1)


def kernel(x_prompt, x_sample, state_gla, cache_nsa_cmp, cache_nsa_slc, cache_nsa_win, cache_dil_0, cache_dil_1,
           cache_dil_2, page_table, p_prompt, p_sample, norm_mix, norm_mlp, norm_ple, a_w_in, a_w_out, gla_w_a2,
           gla_b_a, gla_g_norm, nsa_g_qk, nsa_w_phi, nsa_pe, c_w_in, c_g_qk, c_w_out, mlp_w1, mlp_w2, ple_w_proj,
           ple_w_gate):
    nb_p, seq, _ = x_prompt.shape
    nb_s, n_tok, _ = x_sample.shape
    depth = norm_mix.shape[0]
    past = page_table.shape[1] * PAGE_SIZE
    assert n_tok < CMP_STRIDE and n_tok <= 8 and seq % 512 == 0
    tm_p = 256
    tm_s = min(256, nb_s * n_tok)
    cs_p, sn_p = _rope_tables(jnp.arange(seq))
    cs_s, sn_s = _rope_tables(past + jnp.arange(nb_s * n_tok) % n_tok)
    hp = x_prompt.reshape(nb_p * seq, D_MODEL)
    hs = x_sample.reshape(nb_s * n_tok, D_MODEL)
    dil_caches = (cache_dil_0, cache_dil_1, cache_dil_2)
    outs = {k: [] for k in ("gla_p", "gla_s", "cmp_p", "cmp_s", "slc_p", "slc_s", "win_p", "win_s")}
    dil_p = [[] for _ in DIL_PAIRS]
    dil_s = [[] for _ in DIL_PAIRS]
    kvrow = (2, NSA_KV_HEADS, HEAD_DIM)
    wd = DIL_HEADS * HEAD_DIM

    for i in range(depth):
        j = i // 2
        gmix = norm_mix[i][None, :]
        if i % 2 == 0:
            w_in = a_w_in[j]
            pad_cols = lambda a, width: jnp.pad(a, ((0, 0), (0, width - a.shape[1])))
            w_pad = jnp.concatenate([w_in[:, :1536], pad_cols(w_in[:, 1536:1552], LANES), w_in[:, 1552:2832],
                                     pad_cols(w_in[:, 2832:], LANES)], axis=1).astype(BF16)
            wa2 = jnp.pad(gla_w_a2[j], ((0, LANES - GLA_LOWRANK), (0, 0))).astype(BF16)
            ba = gla_b_a[j][None, :]
            gqk = jnp.tile(nsa_g_qk[j], (1, NSA_HEADS))
            gnorm = gla_g_norm[j][None, :]
            wk, wv = (_block_diag2(nsa_w_phi[j, c]).astype(BF16) for c in range(2))
            pek, pev = (jnp.tile(nsa_pe[j, c], (1, 2)) for c in range(2))
            wo = a_w_out[j].astype(BF16)

            gla, qc, qr, gates, rc, _, _, sbf, wbf, rc_t, rs_t, rw_t = _proj_a(hp, gmix, w_pad, wa2, ba, gqk, cs_p, sn_p,
                                                                               tm_p)
            s0 = jnp.zeros((nb_p, GLA_HEADS, GLA_DK, GLA_DV), F32)
            chunk = min(GLA_CHUNK, seq)
            o_gla, s_p = _gla(gla, s0, gnorm, nb=nb_p, rows_per_seq=seq, chunk=chunk, n_valid=chunk, out_dtype=BF16)
            kc, vc = _compress_prompt(rc, nb_p, seq, wk, wv, pek, pev)
            n_cmp_pad = seq // CMP_STRIDE
            mcs = _cmp_to_slc(n_cmp_pad - 1, seq // SLC_BLOCK, n_cmp_pad, _round_up(seq // SLC_BLOCK, LANES))
            o_nsa = _nsa_prompt(qc, qr, gates, kc, vc, sbf, wbf, mcs, nb=nb_p, seq=seq, qb=NSA_Q_BLOCK)
            mix_p = (o_gla, o_nsa)
            outs["gla_p"].append(s_p)
            rows_out = lambda a: a.reshape((nb_p,) + kvrow + (a.shape[-1],)).transpose(0, 4, 1, 2, 3)
            outs["cmp_p"].append(rows_out(rc_t))
            outs["slc_p"].append(rows_out(rs_t))
            outs["win_p"].append(rows_out(rw_t[:, :, seq - min(NSA_WINDOW, seq):]))

            gla, qc, qr, gates, rc, rs, rw = _proj_a(hs, gmix, w_pad, wa2, ba, gqk, cs_s, sn_s, tm_s)[:7]
            gla8 = jnp.pad(gla.reshape(nb_s, n_tok, G_END), ((0, 0), (0, 8 - n_tok), (0, 0))).reshape(nb_s * 8, G_END)
            o_gla8, s_s = _gla(gla8, state_gla[j], gnorm, nb=nb_s, rows_per_seq=8, chunk=8, n_valid=n_tok, out_dtype=F32)
            o_gla = o_gla8.reshape(nb_s, 8, -1)[:, :n_tok].reshape(nb_s * n_tok, -1)
            kc, vc = _compress_pages(cache_nsa_cmp[j].transpose(0, 2, 3, 4, 1), page_table, wk, wv, pek, pev)
            n_cmp_pad = past // CMP_STRIDE
            n_slc = (past + n_tok + SLC_BLOCK - 1) // SLC_BLOCK
            mcs = _cmp_to_slc(n_cmp_pad - 1, n_slc, n_cmp_pad, LANES)
            m_rows = NSA_GROUP * n_tok
            regroup = lambda a: a.reshape(nb_s, n_tok, NSA_KV_HEADS, NSA_GROUP, -1).transpose(0, 2, 3, 1, 4).reshape(
                nb_s, NSA_KV_HEADS, m_rows, -1)
            g16 = jnp.pad(regroup(gates[:, :3 * NSA_HEADS]), ((0, 0), (0, 0), (0, 0), (0, LANES - 3)))
            tok_id = jnp.arange(m_rows) % n_tok
            amat = (tok_id[:, None] == tok_id[None, :]).astype(F32)
            pad_rows = lambda a: jnp.pad(a.reshape(nb_s, n_tok, 256), ((0, 0), (0, TOK_PAD - n_tok), (0, 0)))
            keyt = lambda a: a.transpose(0, 2, 3, 4, 1)
            neww_t = jnp.pad(keyt(rw.reshape((nb_s, n_tok) + kvrow)), [(0, 0)] * 4 + [(0, TOK_PAD - n_tok)])
            o16, win_new = _nsa_sample(page_table, regroup(qc), regroup(qr), g16, kc, vc, keyt(cache_nsa_slc[j]),
                                       pad_rows(rs), keyt(cache_nsa_win[j]), pad_rows(rw), neww_t, mcs, amat,
                                       n_tok=n_tok)
            o_nsa = o16.reshape(nb_s, NSA_KV_HEADS, NSA_GROUP, n_tok, HEAD_DIM).transpose(0, 3, 1, 2, 4).reshape(
                nb_s * n_tok, NSA_HEADS * HEAD_DIM)
            mix_s = (o_gla, o_nsa)
            outs["gla_s"].append(s_s)
            outs["cmp_s"].append(rc.reshape((nb_s, n_tok) + kvrow))
            outs["slc_s"].append(rs.reshape((nb_s, n_tok) + kvrow))
            outs["win_s"].append(win_new.transpose(0, 4, 1, 2, 3))
            combine = False
        else:
            w_c = c_w_in[j].astype(BF16)
            gq = jnp.tile(c_g_qk[j][:, 0], (1, DIL_HEADS))
            gk = jnp.tile(c_g_qk[j][:, 1], (1, DIL_HEADS))
            wo = c_w_out[j].astype(BF16)
            dilrow = (2, DIL_HEADS, HEAD_DIM)

            res = _proj_c(hp, gmix, w_c, gq, gk, cs_p, sn_p, tm_p, True)
            qs, rows, tails = res[:N_DIL], res[N_DIL:2 * N_DIL], res[2 * N_DIL:]
            o_parts, ml_parts = [], []
            for g, (w, d) in enumerate(DIL_PAIRS):
                o_g, ml_g = _dil_prompt(qs[g], rows[g], nb=nb_p, seq=seq, dil=d, tq=128)
                o_parts.append(o_g)
                ml_parts.append(ml_g)
                dil_p[g].append(tails[g].reshape((nb_p,) + dilrow + (tails[g].shape[-1],)).transpose(0, 4, 1, 2, 3))
            mix_p = tuple(o_parts) + tuple(ml_parts)

            res = _proj_c(hs, gmix, w_c, gq, gk, cs_s, sn_s, tm_s, False)
            qs, rows = res[:N_DIL], res[N_DIL:]
            pad_tok = lambda a: jnp.pad(a, [(0, 0)] * (a.ndim - 1) + [(0, TOK_PAD - n_tok)])
            q_t = pad_tok(jnp.stack([q.astype(F32).reshape(nb_s, n_tok, DIL_HEADS, HEAD_DIM).transpose(0, 2, 3, 1)
                                     for q in qs], axis=1))
            news_t = [pad_tok(r.reshape((nb_s, n_tok) + dilrow).transpose(0, 2, 3, 4, 1)) for r in rows]
            caches_t = [c[j].transpose(0, 2, 3, 4, 1) for c in dil_caches]
            o_t, *new_caches = _dil_sample(q_t, news_t, caches_t, n_tok=n_tok)
            mix_s = (o_t[..., :n_tok].transpose(0, 3, 1, 2).reshape(nb_s * n_tok, wd),)
            for g in range(N_DIL):
                dil_s[g].append(new_caches[g].transpose(0, 4, 1, 2, 3))
            combine = True

        lw = (norm_mlp[i][None, :], mlp_w1[i].astype(BF16), mlp_w2[i].astype(BF16), norm_ple[i][None, :],
              ple_w_gate[i].astype(BF16))
        wp = ple_w_proj[i].astype(BF16)
        hp = _post(hp, mix_p, wo, *lw, p_prompt[i].reshape(nb_p * seq, PLE_DIM), wp, combine=combine, tm=tm_p)
        hs = _post(hs, mix_s, wo, *lw, p_sample[i].reshape(nb_s * n_tok, PLE_DIM), wp, combine=False, tm=tm_s)

    st = jnp.stack
    return (hp.reshape(x_prompt.shape), hs.reshape(x_sample.shape),
            st(outs["gla_p"]), st(outs["gla_s"]), st(outs["cmp_p"]), st(outs["cmp_s"]),
            st(outs["slc_p"]), st(outs["slc_s"]), st(outs["win_p"]), st(outs["win_s"]),
            st(dil_p[0]), st(dil_s[0]), st(dil_p[1]), st(dil_s[1]), st(dil_p[2]), st(dil_s[2]))
```

```python
import functools

import jax
import jax.numpy as jnp
from jax import lax
from jax.experimental import pallas as pl
from jax.experimental.pallas import tpu as pltpu

F32 = jnp.float32
BF16 = jnp.bfloat16
HIGHEST = lax.Precision.HIGHEST

D_MODEL = 1024
PAGE_SIZE = 128
HEAD_DIM = 64
GLA_HEADS = 4
GLA_DK = 64
GLA_DV = 128
GLA_LOWRANK = 16
GLA_TAU = 16.0
GLA_CHUNK = 64
GLA_SUB = 16
NSA_HEADS = 8
NSA_KV_HEADS = 2
NSA_GROUP = NSA_HEADS // NSA_KV_HEADS
CMP_LEN = 32
CMP_STRIDE = 16
SLC_BLOCK = 64
SLC_TOPK = 16
NSA_WINDOW = 512
DIL_PAIRS = ((128, 1), (512, 4), (2048, 16))
N_DIL = 3
DIL_HEADS = 8
DIL_KEYS = 129
D_FF = 4 * D_MODEL
PLE_DIM = 256
ROPE_THETA = 10000.0
EPS = 1e-6
NEG = -1e30
BIG = 1e30
TINY = 1e-20
REMOVED = -3e38

LANES = 128
TOK_PAD = 8
NSA_KEY_TILE = 512
NSA_Q_BLOCK = 256
VMEM_LIMIT = 56 * 1024 * 1024
QSCALE = HEAD_DIM ** -0.5

A_GQ, A_GK, A_GV, A_GR, A_GA, A_NQ, A_NKV, A_NG, A_END = 0, 256, 512, 1024, 1536, 1664, 2176, 2944, 3072
G_Q, G_K, G_V, G_R, G_LG, G_END = 0, 256, 512, 1024, 1536, 1792


def _cparams(*sem):
    return pltpu.CompilerParams(dimension_semantics=sem, vmem_limit_bytes=VMEM_LIMIT)


def _rms_rows(x, g):
    return x * lax.rsqrt(jnp.mean(x * x, axis=-1, keepdims=True) + EPS) * g


def _cols(x):
    return [x[:, c * LANES:(c + 1) * LANES] for c in range(x.shape[1] // LANES)]


def _rms_heads(x, g):
    out = []
    for c, xc in enumerate(_cols(x)):
        low = lax.broadcasted_iota(jnp.int32, xc.shape, 1) < HEAD_DIM
        sq = xc * xc
        s_lo = jnp.sum(jnp.where(low, sq, 0.0), axis=-1, keepdims=True)
        s_hi = jnp.sum(jnp.where(low, 0.0, sq), axis=-1, keepdims=True)
        ms = jnp.where(low, s_lo, s_hi) * (1.0 / HEAD_DIM)
        out.append(xc * lax.rsqrt(ms + EPS) * g[:, c * LANES:(c + 1) * LANES])
    return jnp.concatenate(out, axis=1) if len(out) > 1 else out[0]


def _rope(x, cs, sn):
    out = []
    for xc in _cols(x):
        lane = lax.broadcasted_iota(jnp.int32, xc.shape, 1)
        swapped = jnp.where((lane & 32) == 0, pltpu.roll(xc, LANES - 32, axis=1), pltpu.roll(xc, 32, axis=1))
        out.append(xc * cs + swapped * sn)
    return jnp.concatenate(out, axis=1) if len(out) > 1 else out[0]


def _dot(a, b):
    return jnp.dot(a, b, preferred_element_type=F32)


def _dot_nt(a, b):
    return lax.dot_general(a, b, (((1,), (1,)), ((), ())), preferred_element_type=F32)


def _dot_tn(a, b):
    return lax.dot_general(a, b, (((0,), (0,)), ((), ())), preferred_element_type=F32)


def _softmax_full(s, mask):
    s = jnp.where(mask, s, NEG)
    m = jnp.max(s, axis=-1, keepdims=True)
    p = jnp.where(mask, jnp.exp(s - m), 0.0)
    l = jnp.sum(p, axis=-1, keepdims=True)
    return p, l


def _proj_a_kernel(x_ref, gmix_ref, w_ref, wa2_ref, ba_ref, gqk_ref, cs_ref, sn_ref,
                   gla_ref, qc_ref, qr_ref, gates_ref, rc_ref, rs_ref, rw_ref, sbf_ref, wbf_ref,
                   rct_ref, rst_ref, rwt_ref):
    hn = _rms_rows(x_ref[...], gmix_ref[...])
    z = _dot(hn.astype(BF16), w_ref[...])
    cs, sn = cs_ref[...], sn_ref[...]
    gla_ref[:, G_Q:G_K] = z[:, A_GQ:A_GK] * (GLA_DK ** -0.5)
    gla_ref[:, G_K:G_LG] = z[:, A_GK:A_GA]
    pre = _dot(z[:, A_GA:A_NQ].astype(BF16), wa2_ref[...]) + ba_ref[...]
    gla_ref[:, G_LG:G_END] = jax.nn.log_sigmoid(pre) / GLA_TAU
    qn = _rms_heads(z[:, A_NQ:A_NKV], gqk_ref[0:1, :])
    qc_ref[...] = (qn * QSCALE).astype(BF16)
    qr_ref[...] = (_rope(qn, cs, sn) * QSCALE).astype(BF16)
    gates_ref[...] = jax.nn.sigmoid(z[:, A_NG:A_END])
    kv = [z[:, A_NKV + i * LANES:A_NKV + (i + 1) * LANES] for i in range(6)]
    rc_ref[:, 0:LANES] = _rms_heads(kv[0], gqk_ref[1:2, 0:LANES])
    rc_ref[:, LANES:] = kv[1]
    ks = _rope(_rms_heads(kv[2], gqk_ref[2:3, 0:LANES]), cs, sn)
    rs_ref[:, 0:LANES] = ks
    rs_ref[:, LANES:] = kv[3]
    kw = _rope(_rms_heads(kv[4], gqk_ref[3:4, 0:LANES]), cs, sn)
    rw_ref[:, 0:LANES] = kw
    rw_ref[:, LANES:] = kv[5]
    sbf_ref[:, 0:LANES] = ks.astype(BF16)
    sbf_ref[:, LANES:] = kv[3].astype(BF16)
    wbf_ref[:, 0:LANES] = kw.astype(BF16)
    wbf_ref[:, LANES:] = kv[5].astype(BF16)
    for src, dst in ((rc_ref, rct_ref), (rs_ref, rst_ref), (rw_ref, rwt_ref)):
        dst[...] = src[...].T


def _const_spec(shape):
    nd = len(shape)
    return pl.BlockSpec(shape, lambda *_: (0,) * nd)


def _proj_a(x, gmix, w, wa2, ba, gqk, cs, sn, tm):
    n = x.shape[0]
    tiles_per_seq = cs.shape[0] // tm
    row = lambda width: pl.BlockSpec((tm, width), lambda i: (i, 0))
    tab = pl.BlockSpec((tm, LANES), lambda i: (i % tiles_per_seq, 0))
    widths = (G_END, 512, 512, LANES, 256, 256, 256, 256, 256)
    dtypes = (F32, BF16, BF16, F32, F32, F32, F32, BF16, BF16)
    return pl.pallas_call(
        _proj_a_kernel,
        grid=(n // tm,),
        in_specs=[row(D_MODEL), _const_spec(gmix.shape), _const_spec(w.shape), _const_spec(wa2.shape),
                  _const_spec(ba.shape), _const_spec(gqk.shape), tab, tab],
        out_specs=[row(wd) for wd in widths]
                  + [pl.BlockSpec((None, 256, tm), lambda i: (i // tiles_per_seq, 0, i % tiles_per_seq))] * 3,
        out_shape=[jax.ShapeDtypeStruct((n, wd), dt) for wd, dt in zip(widths, dtypes)]
                  + [jax.ShapeDtypeStruct((n // cs.shape[0], 256, cs.shape[0]), F32)] * 3,
        compiler_params=_cparams("parallel"),
        name="proj_a",
    )(x, gmix, w, wa2, ba, gqk, cs, sn)


def _gla_kernel(gla_ref, s0_ref, gnorm_ref, o_ref, sout_ref, s_scr, *, chunk, sub, n_chunks, n_valid):
    i = pl.program_id(1)
    n_pairs = GLA_HEADS // 2
    n_sub = chunk // sub
    dk2, dv2 = 2 * GLA_DK, 2 * GLA_DV

    @pl.when(i == 0)
    def _():
        s_scr[...] = jnp.zeros(s_scr.shape, F32)
        for h in range(GLA_HEADS):
            p, e = divmod(h, 2)
            s_scr[p, e * GLA_DK:(e + 1) * GLA_DK, e * GLA_DV:(e + 1) * GLA_DV] = s0_ref[h]

    tri = (lax.broadcasted_iota(jnp.int32, (chunk, chunk), 0)
           >= lax.broadcasted_iota(jnp.int32, (chunk, chunk), 1)).astype(F32)
    low = lax.broadcasted_iota(jnp.int32, (chunk, dk2), 1) < GLA_DK
    low_sub = lax.broadcasted_iota(jnp.int32, (sub, dk2), 1) < GLA_DK
    t_sub = lax.broadcasted_iota(jnp.int32, (sub, 1), 0)
    row_blk = lax.broadcasted_iota(jnp.int32, (chunk, chunk), 0) // sub
    col_idx = lax.broadcasted_iota(jnp.int32, (chunk, chunk), 1)
    diag_blocks = ((lax.broadcasted_iota(jnp.int32, (dk2, dv2), 0) < GLA_DK)
                   == (lax.broadcasted_iota(jnp.int32, (dk2, dv2), 1) < GLA_DV))
    eye = (lax.broadcasted_iota(jnp.int32, (dk2, dk2), 0) == lax.broadcasted_iota(jnp.int32, (dk2, dk2), 1))
    gnorm = gnorm_ref[...]

    def do_chunk(c, carry):
        rows = pl.ds(pl.multiple_of(c * chunk, chunk), chunk)
        b_all = jnp.dot(tri, gla_ref[rows, G_LG:G_END], precision=HIGHEST, preferred_element_type=F32)
        for p in range(n_pairs):
            q2 = gla_ref[rows, G_Q + p * dk2:G_Q + (p + 1) * dk2]
            k2 = gla_ref[rows, G_K + p * dk2:G_K + (p + 1) * dk2]
            v2 = gla_ref[rows, G_V + p * dv2:G_V + (p + 1) * dv2]
            b2 = b_all[:, p * dk2:(p + 1) * dk2]
            s_pair = s_scr[p]
            o2 = _dot((q2 * jnp.exp(b2)).astype(BF16), s_pair.astype(BF16))

            if n_sub > 1:
                bref_rows = jnp.concatenate(
                    [b2[0:sub]] + [jnp.broadcast_to(b2[i * sub - 1:i * sub], (sub, dk2)) for i in range(1, n_sub)], axis=0)
                qt = q2 * jnp.exp(jnp.minimum(b2 - bref_rows, 0.0))
                qt_even = jnp.where(low, qt, 0.0).astype(BF16)
                qt_odd = jnp.where(low, 0.0, qt).astype(BF16)
                a_even = jnp.zeros((chunk, chunk), F32)
                a_odd = jnp.zeros((chunk, chunk), F32)
                for i in range(1, n_sub):
                    kt = (k2 * jnp.exp(jnp.minimum(b2[i * sub - 1:i * sub] - b2, 0.0))).astype(BF16)
                    take = (row_blk == i) & (col_idx < i * sub)
                    a_even = jnp.where(take, _dot_nt(qt_even, kt), a_even)
                    a_odd = jnp.where(take, _dot_nt(qt_odd, kt), a_odd)
                o2 = o2 + jnp.concatenate([_dot(a_even.astype(BF16), v2[:, 0:GLA_DV].astype(BF16)),
                                           _dot(a_odd.astype(BF16), v2[:, GLA_DV:].astype(BF16))], axis=1)

            diag = []
            for i in range(n_sub):
                sl = slice(i * sub, (i + 1) * sub)
                qs, ks, bs, vs = q2[sl], k2[sl], b2[sl], v2[sl]
                od = jnp.zeros((sub, dv2), F32)
                for jj in range(max(0, min(sub, n_valid - i * sub))):
                    w = qs * ks[jj:jj + 1] * jnp.exp(jnp.minimum(bs - bs[jj:jj + 1], 0.0))
                    keep = t_sub >= jj
                    a_e = jnp.where(keep, jnp.sum(jnp.where(low_sub, w, 0.0), axis=-1, keepdims=True), 0.0)
                    a_o = jnp.where(keep, jnp.sum(jnp.where(low_sub, 0.0, w), axis=-1, keepdims=True), 0.0)
                    od = od + jnp.concatenate([a_e * vs[jj:jj + 1, 0:GLA_DV], a_o * vs[jj:jj + 1, GLA_DV:]], axis=1)
                diag.append(od)
            o2 = o2 + (jnp.concatenate(diag, axis=0) if n_sub > 1 else diag[0])

            b_last = b2[chunk - 1:chunk, :]
            kdec = k2 * jnp.exp(b_last - b2)
            upd = _dot_tn(kdec.astype(BF16), v2.astype(BF16))
            decay_col = jnp.sum(jnp.where(eye, jnp.exp(b_last), 0.0), axis=1, keepdims=True)
            s_scr[p] = decay_col * s_pair + jnp.where(diag_blocks, upd, 0.0)
            for e in range(2):
                h = 2 * p + e
                on = _rms_rows(o2[:, e * GLA_DV:(e + 1) * GLA_DV], gnorm)
                r = gla_ref[rows, G_R + h * GLA_DV:G_R + (h + 1) * GLA_DV]
                o_ref[rows, h * GLA_DV:(h + 1) * GLA_DV] = (on * (r * jax.nn.sigmoid(r))).astype(o_ref.dtype)
        return carry
    lax.fori_loop(0, n_chunks, do_chunk, 0)

    @pl.when(i == pl.num_programs(1) - 1)
    def _():
        for h in range(GLA_HEADS):
            p, e = divmod(h, 2)
            sout_ref[h] = s_scr[p, e * GLA_DK:(e + 1) * GLA_DK, e * GLA_DV:(e + 1) * GLA_DV]


def _gla(gla, s0, gnorm, *, nb, rows_per_seq, chunk, n_valid, out_dtype):
    step_rows = min(rows_per_seq, 8 * chunk)
    steps = rows_per_seq // step_rows
    kern = functools.partial(_gla_kernel, chunk=chunk, sub=min(chunk, GLA_SUB), n_chunks=step_rows // chunk,
                             n_valid=n_valid)
    return pl.pallas_call(
        kern,
        grid=(nb, steps),
        in_specs=[pl.BlockSpec((step_rows, G_END), lambda b, i: (b * steps + i, 0)),
                  pl.BlockSpec((None, GLA_HEADS, GLA_DK, GLA_DV), lambda b, i: (b, 0, 0, 0)),
                  _const_spec(gnorm.shape)],
        out_specs=[pl.BlockSpec((step_rows, GLA_HEADS * GLA_DV), lambda b, i: (b * steps + i, 0)),
                   pl.BlockSpec((None, GLA_HEADS, GLA_DK, GLA_DV), lambda b, i: (b, 0, 0, 0))],
        out_shape=[jax.ShapeDtypeStruct((nb * rows_per_seq, GLA_HEADS * GLA_DV), out_dtype),
                   jax.ShapeDtypeStruct((nb, GLA_HEADS, GLA_DK, GLA_DV), F32)],
        scratch_shapes=[pltpu.VMEM((GLA_HEADS // 2, 2 * GLA_DK, 2 * GLA_DV), F32)],
        compiler_params=_cparams("parallel", "arbitrary"),
        name="gla",
    )(gla, s0, gnorm)


def _compress_kernel(*refs, n_parts, n_prefetch, transposed_pages):
    refs = refs[n_prefetch:]
    x_refs = refs[:n_parts]
    row_w = 2 * LANES
    if transposed_pages:
        wk_ref, wv_ref, pek_ref, pev_ref, kc_ref, vc_ref, sh_scr, page_scr, x_scr = refs[n_parts:]
        mp = PAGE_SIZE // CMP_STRIDE
        m = n_parts * mp
        for pg, r in enumerate(x_refs):
            page = r[...].reshape(row_w, PAGE_SIZE).T
            for half in range(2):
                page_scr[half] = page[:, half * LANES:(half + 1) * LANES]
            for p in range(CMP_STRIDE):
                for half in range(2):
                    x_scr[p, pg * mp:(pg + 1) * mp, half * LANES:(half + 1) * LANES] = (
                        page_scr[half, pl.ds(p, mp, stride=CMP_STRIDE), :])
        chunk_rows = lambda p, off: x_scr[p, :, off:off + LANES]
    else:
        wk_ref, wv_ref, pek_ref, pev_ref, kc_ref, vc_ref, sh_scr = refs[n_parts:]
        x = x_refs[0][...]
        m = x.shape[0]
        chunk_rows = lambda p, off: x[:, p * row_w + off:p * row_w + off + LANES]

    def branch(off, w_ref, pe_ref, out_ref):
        lo = jnp.zeros((m, LANES), F32)
        hi = jnp.zeros((m, LANES), F32)
        for p in range(CMP_STRIDE):
            xp = chunk_rows(p, off)
            lo = lo + _dot((xp + pe_ref[p:p + 1, :]).astype(BF16), w_ref[p])
            hi = hi + _dot((xp + pe_ref[CMP_STRIDE + p:CMP_STRIDE + p + 1, :]).astype(BF16), w_ref[CMP_STRIDE + p])
        sh_scr[pl.ds(0, m), :] = hi
        sh_scr[pl.ds(m, 8), :] = jnp.zeros((8, LANES), F32)
        out_ref[...] = (lo + sh_scr[pl.ds(1, m), :]).astype(out_ref.dtype)
    branch(0, wk_ref, pek_ref, kc_ref)
    branch(LANES, wv_ref, pev_ref, vc_ref)


def _compress_call(n_parts, m, grid, x_specs, out_map, out_rows, prefetch, transposed_pages):
    kern = functools.partial(_compress_kernel, n_parts=n_parts, n_prefetch=prefetch, transposed_pages=transposed_pages)
    w_shape = (CMP_LEN, LANES, LANES)
    pe_shape = (CMP_LEN, LANES)
    in_specs = list(x_specs) + [_const_spec(w_shape), _const_spec(w_shape), _const_spec(pe_shape), _const_spec(pe_shape)]
    out_specs = [pl.BlockSpec((m, LANES), out_map), pl.BlockSpec((m, LANES), out_map)]
    scratch = [pltpu.VMEM((m + 8, LANES), F32)]
    if transposed_pages:
        scratch += [pltpu.VMEM((2, PAGE_SIZE, LANES), F32), pltpu.VMEM((CMP_STRIDE, m, 2 * LANES), F32)]
    gs = pltpu.PrefetchScalarGridSpec(num_scalar_prefetch=prefetch, grid=grid, in_specs=in_specs, out_specs=out_specs,
                                      scratch_shapes=scratch)
    return pl.pallas_call(
        kern, grid_spec=gs,
        out_shape=[jax.ShapeDtypeStruct((out_rows, LANES), BF16)] * 2,
        compiler_params=_cparams("parallel"),
        name="nsa_compress",
    )


def _compress_prompt(rows_c, nb, seq, wk, wv, pek, pev):
    m = seq // CMP_STRIDE
    x = rows_c.reshape(-1, CMP_STRIDE * 2 * LANES)
    call = _compress_call(1, m, (nb,), [pl.BlockSpec((m, CMP_STRIDE * 2 * LANES), lambda b: (b, 0))],
                          lambda b: (b, 0), nb * m, 0, False)
    return call(x, wk, wv, pek, pev)


def _compress_pages(pool_t, page_table, wk, wv, pek, pev):
    nb, n_pages = page_table.shape
    specs = [pl.BlockSpec((None,) + pool_t.shape[1:], functools.partial(lambda pg, b, pt: (pt[b, pg], 0, 0, 0, 0), pg))
             for pg in range(n_pages)]
    m = n_pages * (PAGE_SIZE // CMP_STRIDE)
    call = _compress_call(n_pages, m, (nb,), specs, lambda b, pt: (b, 0), nb * m, 1, True)
    return call(page_table, *([pool_t] * n_pages), wk, wv, pek, pev)


def _topk_mask(imp, qpos, n_slc):
    blk = lax.broadcasted_iota(jnp.int32, imp.shape, 1)
    cur = qpos // SLC_BLOCK
    val = jnp.where((blk == cur) | (blk == 0), BIG, jnp.where(blk <= cur, imp, NEG))
    val = jnp.where(blk < n_slc, val, REMOVED)
    sel = jnp.zeros(imp.shape, jnp.bool_)
    for _ in range(min(SLC_TOPK, n_slc)):
        mx = jnp.max(val, axis=-1, keepdims=True)
        idx = jnp.min(jnp.where(val == mx, blk, imp.shape[1]), axis=-1, keepdims=True)
        pick = blk == idx
        sel = sel | pick
        val = jnp.where(pick, REMOVED, val)
    return sel.astype(BF16)


def _topk_mask_by_rank(imp, qpos, n_slc):
    blk = lax.broadcasted_iota(jnp.int32, imp.shape, 1)
    cur = qpos // SLC_BLOCK
    val = jnp.where((blk == cur) | (blk == 0), BIG, jnp.where(blk <= cur, imp, NEG))
    rank = jnp.zeros(imp.shape, jnp.int32)
    for i in range(n_slc):
        col = val[:, i:i + 1]
        rank = rank + ((col > val) | ((col == val) & (blk > i))).astype(jnp.int32)
    return ((rank < min(SLC_TOPK, n_slc)) & (blk < n_slc)).astype(BF16)


def _block_expand(n_blocks_pad, k0, n_keys):
    blk = lax.broadcasted_iota(jnp.int32, (n_blocks_pad, n_keys), 0)
    col = lax.broadcasted_iota(jnp.int32, (n_blocks_pad, n_keys), 1)
    return (blk == (k0 + col) // SLC_BLOCK).astype(BF16)


def _nsa_prompt_kernel(qc_ref, qr_ref, gates_ref, kc_ref, vc_ref, ks_ref, kw_ref, mcs_ref, o_ref,
                       qa_scr, m_scr, l_scr, acc_scr, *, qb, seq, key_tile):
    t0 = pl.program_id(1) * qb
    qpos = t0 + lax.broadcasted_iota(jnp.int32, (qb, 1), 0)
    n_cmp_pad = kc_ref.shape[0]
    n_slc = seq // SLC_BLOCK
    hd = HEAD_DIM

    rows = NSA_HEADS * qb
    lane2 = lax.broadcasted_iota(jnp.int32, (qb, 2 * hd), 1)

    def stack_heads(q_ref):
        parts = []
        for h in range(NSA_HEADS):
            g = h // NSA_GROUP
            pair = q_ref[:, (h // 2) * 2 * hd:(h // 2 + 1) * 2 * hd].astype(F32)
            if h % 2 != g:
                pair = pltpu.roll(pair, hd, axis=1)
            parts.append(jnp.where((lane2 >= g * hd) & (lane2 < (g + 1) * hd), pair, 0.0).astype(BF16))
        return jnp.concatenate(parts, axis=0)

    def row_softmax(s, mask):
        s = jnp.where(mask, s, NEG)
        p = jnp.where(mask, jnp.exp(s - jnp.max(s, axis=-1, keepdims=True)), 0.0)
        return p, jnp.maximum(jnp.sum(p, axis=-1, keepdims=True), TINY)

    q_tok = t0 + lax.broadcasted_iota(jnp.int32, (rows, 1), 0) % qb
    qr_st = stack_heads(qr_ref)

    ncol = lax.broadcasted_iota(jnp.int32, (rows, n_cmp_pad), 1)
    p, l_cmp = row_softmax(_dot_nt(stack_heads(qc_ref), kc_ref[...]), (ncol * CMP_STRIDE + CMP_LEN - 1) <= q_tok)
    p = p * (1.0 / l_cmp)
    o_cmp = _dot(p.astype(BF16), vc_ref[...])
    psum = []
    for g in range(NSA_KV_HEADS):
        blocks = [p[(g * NSA_GROUP + j) * qb:(g * NSA_GROUP + j + 1) * qb] for j in range(NSA_GROUP)]
        psum.append(functools.reduce(lambda a, b: a + b, blocks))
    imp = jnp.dot(jnp.concatenate(psum, axis=0), mcs_ref[...], precision=HIGHEST, preferred_element_type=F32)
    sel = _topk_mask(imp, jnp.concatenate([qpos] * NSA_KV_HEADS, axis=0), n_slc)

    n_blk = sel.shape[1]
    qa_scr[:, 0:2 * hd] = qr_st
    for h in range(NSA_HEADS):
        g = h // NSA_GROUP
        qa_scr[h * qb:(h + 1) * qb, 2 * hd:] = 1.0 - sel[g * qb:(g + 1) * qb]
    m_scr[...] = jnp.full((rows, LANES), NEG, F32)
    l_scr[...] = jnp.zeros((rows, LANES), F32)
    acc_scr[...] = jnp.zeros((rows, 2 * hd), F32)

    def tile(k0, causal):
        kv = ks_ref[pl.ds(k0, key_tile), :]
        key_blk = (k0 + lax.broadcasted_iota(jnp.int32, (key_tile, n_blk), 0)) // SLC_BLOCK
        own_blk = key_blk == lax.broadcasted_iota(jnp.int32, (key_tile, n_blk), 1)
        k_aug = jnp.concatenate([kv[:, 0:2 * hd], jnp.where(own_blk, NEG, 0.0).astype(BF16)], axis=1)
        s = _dot_nt(qa_scr[...], k_aug)
        if causal:
            q_row = t0 + lax.broadcasted_iota(jnp.int32, (rows, key_tile), 0) % qb
            s = jnp.where(k0 + lax.broadcasted_iota(jnp.int32, (rows, key_tile), 1) <= q_row, s, NEG)
        m_old = m_scr[...]
        m_new = jnp.maximum(m_old, jnp.max(s, axis=-1, keepdims=True))
        alpha = jnp.exp(m_old - m_new)
        p = jnp.exp(s - jnp.concatenate([m_new] * (key_tile // LANES), axis=1))
        l_scr[...] = alpha * l_scr[...] + jnp.sum(p, axis=-1, keepdims=True)
        acc_scr[...] = alpha * acc_scr[...] + _dot(p.astype(BF16), kv[:, 2 * hd:4 * hd])
        m_scr[...] = m_new

    n_full = t0 // key_tile

    def full_tile(kt, carry):
        tile(pl.multiple_of(kt * key_tile, key_tile), False)
        return carry
    lax.fori_loop(0, n_full, full_tile, 0)
    tile(pl.multiple_of(n_full * key_tile, key_tile), True)

    wlen = min(seq, NSA_WINDOW + qb)
    start = pl.multiple_of(jnp.clip(t0 - NSA_WINDOW, 0, seq - wlen), qb)
    band = kw_ref[pl.ds(start, wlen), :]
    kpos = start + lax.broadcasted_iota(jnp.int32, (rows, wlen), 1)
    p, l_win = row_softmax(_dot_nt(qr_st, band[:, 0:2 * hd]), (kpos <= q_tok) & (kpos > q_tok - NSA_WINDOW))
    o_win = _dot(p.astype(BF16), band[:, 2 * hd:4 * hd])
    for h in range(NSA_HEADS):
        g = h // NSA_GROUP
        rs, ls = slice(h * qb, (h + 1) * qb), slice(g * hd, (g + 1) * hd)
        o_w = o_win[rs, ls] / l_win[rs]
        o_slc = acc_scr[rs, ls] / jnp.maximum(l_scr[rs, 0:hd], TINY)
        gate = gates_ref[:, 3 * h:3 * h + 3]
        o = gate[:, 0:1] * o_cmp[rs, ls] + gate[:, 1:2] * o_slc + gate[:, 2:3] * o_w
        o_ref[:, h * hd:(h + 1) * hd] = o.astype(o_ref.dtype)


def _nsa_prompt(qc, qr, gates, kc, vc, ks, kw, mcs, *, nb, seq, qb):
    nq = seq // qb
    n_cmp_pad = seq // CMP_STRIDE
    key_tile = min(seq, NSA_KEY_TILE)
    n_blk = _round_up(seq // SLC_BLOCK, LANES)
    kern = functools.partial(_nsa_prompt_kernel, qb=qb, seq=seq, key_tile=key_tile)
    qspec = lambda wd: pl.BlockSpec((qb, wd), lambda b, i: (b * nq + i, 0))
    seqspec = lambda rows, wd: pl.BlockSpec((rows, wd), lambda b, i: (b, 0))
    return pl.pallas_call(
        kern,
        grid=(nb, nq),
        in_specs=[qspec(512), qspec(512), qspec(LANES), seqspec(n_cmp_pad, LANES), seqspec(n_cmp_pad, LANES),
                  seqspec(seq, 256), seqspec(seq, 256), _const_spec(mcs.shape)],
        out_specs=qspec(512),
        out_shape=jax.ShapeDtypeStruct((nb * seq, 512), BF16),
        scratch_shapes=[pltpu.VMEM((NSA_HEADS * qb, 2 * HEAD_DIM + n_blk), BF16),
                        pltpu.VMEM((NSA_HEADS * qb, LANES), F32), pltpu.VMEM((NSA_HEADS * qb, LANES), F32),
                        pltpu.VMEM((NSA_HEADS * qb, 2 * HEAD_DIM), F32)],
        compiler_params=_cparams("parallel", "arbitrary"),
        name="nsa_prompt",
    )(qc, qr, gates, kc, vc, ks, kw, mcs)


def _attend_transposed(q_bf, q_f32, kt, vt, mask, newk, newv, mask_new, n_tok):
    m_rows = q_bf.shape[0]
    s = jnp.where(mask, _dot(q_bf, kt), NEG)
    lane = lax.broadcasted_iota(jnp.int32, (m_rows, TOK_PAD), 1)
    sn = jnp.full((m_rows, TOK_PAD), NEG, F32)
    for t in range(n_tok):
        sn = jnp.where(lane == t, jnp.sum(q_f32 * newk[t:t + 1], axis=-1, keepdims=True), sn)
    sn = jnp.where(mask_new, sn, NEG)
    m = jnp.maximum(jnp.max(s, axis=-1, keepdims=True), jnp.max(sn, axis=-1, keepdims=True))
    p = jnp.where(mask, jnp.exp(s - m), 0.0)
    pn = jnp.where(mask_new, jnp.exp(sn - m), 0.0)
    l = jnp.sum(p, axis=-1, keepdims=True) + jnp.sum(pn, axis=-1, keepdims=True)
    o = _dot_nt(p.astype(BF16), vt)
    for t in range(n_tok):
        o = o + pn[:, t:t + 1] * newv[t:t + 1]
    return o / jnp.maximum(l, TINY)


def _nsa_sample_kernel(*refs, n_pages, n_tok):
    (qc_ref, qr_ref, gates_ref, kc_ref, vc_ref) = refs[1:6]
    page_refs = refs[6:6 + n_pages]
    news_ref, win_ref, neww_ref, newwt_ref, mcs_ref, amat_ref, o_ref, owin_ref = refs[6 + n_pages:]
    hd = HEAD_DIM
    past = n_pages * PAGE_SIZE
    win = win_ref.shape[-1]
    m_rows = qc_ref.shape[1]
    tok = lax.broadcasted_iota(jnp.int32, (m_rows, 1), 0) % n_tok
    qpos = past + tok
    n_cmp_pad = kc_ref.shape[0]
    n_slc = (past + n_tok + SLC_BLOCK - 1) // SLC_BLOCK

    ncol = lax.broadcasted_iota(jnp.int32, (m_rows, n_cmp_pad), 1)
    cmask = ((ncol * CMP_STRIDE + CMP_LEN - 1) <= qpos) & (ncol < n_cmp_pad - 1)
    new_pos = past + lax.broadcasted_iota(jnp.int32, (m_rows, TOK_PAD), 1)
    new_ok = (new_pos <= qpos) & (new_pos < past + n_tok)
    wpos = past - win + lax.broadcasted_iota(jnp.int32, (m_rows, win), 1)
    wmask = (wpos <= qpos) & (wpos > qpos - NSA_WINDOW) & (wpos >= 0)
    wmask_new = new_ok & (new_pos > qpos - NSA_WINDOW)

    o_cmp, imps = [], []
    for g in range(NSA_KV_HEADS):
        ksl = slice(g * hd, (g + 1) * hd)
        p, l = _softmax_full(_dot_nt(qc_ref[g], kc_ref[:, ksl]), cmask)
        p = p / jnp.maximum(l, TINY)
        o_cmp.append(_dot(p.astype(BF16), vc_ref[:, ksl]))
        psum = jnp.dot(amat_ref[...], p, precision=HIGHEST, preferred_element_type=F32)
        imps.append(jnp.dot(psum, mcs_ref[...], precision=HIGHEST, preferred_element_type=F32))
    sel_all = _topk_mask_by_rank(jnp.concatenate(imps, axis=0), jnp.concatenate([qpos] * NSA_KV_HEADS, axis=0), n_slc)
    expand = _block_expand(LANES, 0, past)
    new_blk = lax.broadcasted_iota(jnp.int32, (m_rows, LANES), 1) == past // SLC_BLOCK

    for g in range(NSA_KV_HEADS):
        ksl = slice(g * hd, (g + 1) * hd)
        vsl = slice(LANES + g * hd, LANES + (g + 1) * hd)
        qr = qr_ref[g]
        qr32 = qr.astype(F32)
        sel = sel_all[g * m_rows:(g + 1) * m_rows]
        sel_new = jnp.sum(jnp.where(new_blk, sel.astype(F32), 0.0), axis=-1, keepdims=True) > 0.5
        kt = jnp.concatenate([r[0, g].astype(BF16) for r in page_refs], axis=1)
        vt = jnp.concatenate([r[1, g].astype(BF16) for r in page_refs], axis=1)
        o_slc = _attend_transposed(qr, qr32, kt, vt, _dot(sel, expand) > 0.5,
                                   news_ref[:, ksl], news_ref[:, vsl], new_ok & sel_new, n_tok)
        o_win = _attend_transposed(qr, qr32, win_ref[0, g].astype(BF16), win_ref[1, g].astype(BF16), wmask,
                                   neww_ref[:, ksl], neww_ref[:, vsl], wmask_new, n_tok)
        gate = gates_ref[g]
        o_ref[g] = gate[:, 0:1] * o_cmp[g] + gate[:, 1:2] * o_slc + gate[:, 2:3] * o_win

        lane = lax.broadcasted_iota(jnp.int32, (hd, LANES), 1)
        for kv in range(2):
            shifted = pltpu.roll(win_ref[kv, g], win - n_tok, axis=1)
            last = shifted[:, win - LANES:]
            for t in range(n_tok):
                last = jnp.where(lane == LANES - n_tok + t, newwt_ref[kv, g, :, t:t + 1], last)
            if win > LANES:
                owin_ref[kv, g, :, 0:win - LANES] = shifted[:, 0:win - LANES]
            owin_ref[kv, g, :, win - LANES:] = last


def _nsa_sample(page_table, qc, qr, gates, kc, vc, pool_s, news, win_t, neww, neww_t, mcs, amat, *, n_tok):
    nb, n_pages = page_table.shape
    m_rows = qc.shape[2]
    kern = functools.partial(_nsa_sample_kernel, n_pages=n_pages, n_tok=n_tok)
    per_b = lambda a: pl.BlockSpec((None,) + a.shape[1:], lambda b, pt: (b,) + (0,) * (a.ndim - 1))
    n_cmp_pad = kc.shape[0] // nb
    cmp_spec = pl.BlockSpec((n_cmp_pad, LANES), lambda b, pt: (b, 0))
    page_specs = [pl.BlockSpec((None,) + pool_s.shape[1:], functools.partial(lambda pg, b, pt: (pt[b, pg], 0, 0, 0, 0), pg))
                  for pg in range(n_pages)]
    gs = pltpu.PrefetchScalarGridSpec(
        num_scalar_prefetch=1, grid=(nb,),
        in_specs=[per_b(qc), per_b(qr), per_b(gates), cmp_spec, cmp_spec] + page_specs +
                 [per_b(news), per_b(win_t), per_b(neww), per_b(neww_t),
                  pl.BlockSpec(mcs.shape, lambda b, pt: (0, 0)), pl.BlockSpec(amat.shape, lambda b, pt: (0, 0))],
        out_specs=[per_b(qc), per_b(win_t)])
    return pl.pallas_call(
        kern, grid_spec=gs,
        out_shape=[jax.ShapeDtypeStruct((nb, NSA_KV_HEADS, m_rows, HEAD_DIM), F32),
                   jax.ShapeDtypeStruct(win_t.shape, F32)],
        compiler_params=_cparams("parallel"),
        name="nsa_sample",
    )(page_table, qc, qr, gates, kc, vc, *([pool_s] * n_pages), news, win_t, neww, neww_t, mcs, amat)


C_GROUP_COLS = 3 * DIL_HEADS * HEAD_DIM


def _proj_c_kernel(x_ref, gmix_ref, w_ref, gq_ref, gk_ref, cs_ref, sn_ref, *out_refs, tails):
    q_refs, r_refs, t_refs = out_refs[:N_DIL], out_refs[N_DIL:2 * N_DIL], out_refs[2 * N_DIL:]
    hn = _rms_rows(x_ref[...], gmix_ref[...]).astype(BF16)
    cs, sn = cs_ref[...], sn_ref[...]
    wd = DIL_HEADS * HEAD_DIM
    for g in range(N_DIL):
        z = _dot(hn, w_ref[:, g * C_GROUP_COLS:(g + 1) * C_GROUP_COLS])
        q = _rope(_rms_heads(z[:, 0:wd], gq_ref[g:g + 1, :]), cs, sn)
        q_refs[g][...] = (q * QSCALE).astype(BF16)
        r_refs[g][:, 0:wd] = _rope(_rms_heads(z[:, wd:2 * wd], gk_ref[g:g + 1, :]), cs, sn)
        r_refs[g][:, wd:] = z[:, 2 * wd:]
        if tails is not None:
            first, cols = tails[g]
            tm = x_ref.shape[0]

            i_seq = pl.program_id(0) % tails[N_DIL]

            @pl.when(i_seq >= first)
            def _():
                t_refs[g][...] = r_refs[g][tm - cols:tm, :].T

            if first > 0:
                @pl.when(i_seq == 0)
                def _():
                    t_refs[g][...] = jnp.zeros(t_refs[g].shape, F32)


def _proj_c(x, gmix, w, gq, gk, cs, sn, tm, with_tails):
    n = x.shape[0]
    seq = cs.shape[0]
    tiles_per_seq = seq // tm
    row = lambda width: pl.BlockSpec((tm, width), lambda i: (i, 0))
    tab = pl.BlockSpec((tm, LANES), lambda i: (i % tiles_per_seq, 0))
    wd = DIL_HEADS * HEAD_DIM
    out_specs = [row(wd)] * N_DIL + [row(2 * wd)] * N_DIL
    out_shape = [jax.ShapeDtypeStruct((n, wd), BF16)] * N_DIL + [jax.ShapeDtypeStruct((n, 2 * wd), F32)] * N_DIL
    tails = None
    if with_tails:
        tails = []
        for win, _ in DIL_PAIRS:
            rows = min(win, seq)
            cols = min(rows, tm)
            assert rows % cols == 0 and seq % tm == 0
            first = tiles_per_seq - max(rows // tm, 1)
            tails.append((first, cols))
            out_specs.append(pl.BlockSpec(
                (None, 2 * wd, cols),
                functools.partial(lambda first, i: (i // tiles_per_seq, 0, jnp.maximum(i % tiles_per_seq - first, 0)), first)))
            out_shape.append(jax.ShapeDtypeStruct((n // seq, 2 * wd, rows), F32))
        tails = tuple(tails) + (tiles_per_seq,)
    return pl.pallas_call(
        functools.partial(_proj_c_kernel, tails=tails),
        grid=(n // tm,),
        in_specs=[row(D_MODEL), _const_spec(gmix.shape), _const_spec(w.shape), _const_spec(gq.shape),
                  _const_spec(gk.shape), tab, tab],
        out_specs=out_specs,
        out_shape=out_shape,
        compiler_params=_cparams("arbitrary" if with_tails else "parallel"),
        name="proj_c",
    )(x, gmix, w, gq, gk, cs, sn)


def _dil_prompt_kernel(q_ref, kp_ref, vp_ref, kc_ref, vc_ref, o_ref, ml_ref, *, tq):
    i = pl.program_id(2)
    row = lax.broadcasted_iota(jnp.int32, (tq, 2 * tq), 0)
    col = lax.broadcasted_iota(jnp.int32, (tq, 2 * tq), 1)
    delta = tq + row - col
    mask = (delta >= 0) & (delta < DIL_KEYS) & ((col >= tq) | (i > 0))
    k = jnp.concatenate([kp_ref[...], kc_ref[...]], axis=0).astype(BF16)
    v = jnp.concatenate([vp_ref[...], vc_ref[...]], axis=0).astype(BF16)
    lane = lax.broadcasted_iota(jnp.int32, (tq, LANES), 1)
    ml = jnp.zeros((tq, LANES), F32)
    hd = HEAD_DIM
    for h in range(DIL_HEADS):
        hs = slice(h * hd, (h + 1) * hd)
        s = jnp.where(mask, _dot_nt(q_ref[:, hs], k[:, hs]), NEG)
        m = jnp.max(s, axis=-1, keepdims=True)
        p = jnp.where(mask, jnp.exp(s - m), 0.0)
        l = jnp.sum(p, axis=-1, keepdims=True)
        o_ref[:, hs] = _dot(p.astype(BF16), v[:, hs])
        ml = jnp.where(lane == h, m, jnp.where(lane == DIL_HEADS + h, l, ml))
    ml_ref[...] = ml


def _dil_prompt(q, rows, *, nb, seq, dil, tq):
    wd = DIL_HEADS * HEAD_DIM
    assert tq >= DIL_KEYS - 1 and seq % (dil * tq) == 0
    nu = seq // dil // tq
    qv = q.reshape(nb * seq // dil, dil * wd)
    rv = rows.reshape(nb * seq // dil, dil * 2 * wd)
    cur = lambda off: (lambda b, r, i: (b * nu + i, 2 * r + off))
    prev = lambda off: (lambda b, r, i: (b * nu + jnp.maximum(i - 1, 0), 2 * r + off))
    blk = lambda imap: pl.BlockSpec((tq, wd), imap)
    o, ml = pl.pallas_call(
        functools.partial(_dil_prompt_kernel, tq=tq),
        grid=(nb, dil, nu),
        in_specs=[blk(lambda b, r, i: (b * nu + i, r)), blk(prev(0)), blk(prev(1)), blk(cur(0)), blk(cur(1))],
        out_specs=[blk(lambda b, r, i: (b * nu + i, r)), pl.BlockSpec((tq, LANES), lambda b, r, i: (b * nu + i, r))],
        out_shape=[jax.ShapeDtypeStruct(qv.shape, F32), jax.ShapeDtypeStruct((qv.shape[0], dil * LANES), F32)],
        compiler_params=_cparams("parallel", "parallel", "arbitrary"),
        name="dil_prompt",
    )(qv, rv, rv, rv, rv)
    return o.reshape(nb * seq, wd), ml.reshape(nb * seq, LANES)


DIL_SAMPLE_HEADS = 4


def _dil_sample_kernel(q_ref, n0_ref, n1_ref, n2_ref, c0_ref, c1_ref, c2_ref, o_ref, oc0_ref, oc1_ref, oc2_ref,
                       *, n_tok):
    n_refs, c_refs, oc_refs = (n0_ref, n1_ref, n2_ref), (c0_ref, c1_ref, c2_ref), (oc0_ref, oc1_ref, oc2_ref)
    tp = lax.broadcasted_iota(jnp.int32, (1, TOK_PAD), 1)
    lane_t = lax.broadcasted_iota(jnp.int32, (HEAD_DIM, TOK_PAD), 1)
    for hl in range(q_ref.shape[1]):
        stats = [[None] * N_DIL for _ in range(n_tok)]
        for g, (w, dil) in enumerate(DIL_PAIRS):
            nk = n_refs[g][0, hl]
            nv = n_refs[g][1, hl]
            row = lax.broadcasted_iota(jnp.int32, (1, w), 1)
            n_ch = w // LANES
            q_cols = [q_ref[g, hl, :, t:t + 1] for t in range(n_tok)]
            q_wide = [jnp.broadcast_to(qc, (HEAD_DIM, LANES)) for qc in q_cols]
            shared = dil >= n_tok
            groups = [list(range(n_tok))] if shared else [[t] for t in range(n_tok)]
            lane_tok = lax.broadcasted_iota(jnp.int32, (HEAD_DIM, LANES), 1) & (dil - 1)
            row_tok = row & (dil - 1)
            for toks in groups:
                if shared:
                    q_sel = jnp.zeros((HEAD_DIM, LANES), F32)
                    for t in toks:
                        q_sel = jnp.where(lane_tok == t, q_wide[t], q_sel)
                else:
                    q_sel = q_wide[toks[0]]
                s_parts = [jnp.sum(c_refs[g][0, hl, :, c * LANES:(c + 1) * LANES] * q_sel, axis=0, keepdims=True)
                           for c in range(n_ch)]
                s_all = jnp.concatenate(s_parts, axis=1) if n_ch > 1 else s_parts[0]
                m_lanes = jnp.zeros((1, w), F32)
                any_valid = jnp.zeros((1, w), jnp.bool_)
                valids, pnews = [], []
                for t in toks:
                    valid = (row_tok == (t & (dil - 1))) & (w + t - row <= (DIL_KEYS - 1) * dil)
                    dn = t - tp
                    valid_n = (dn >= 0) & ((dn & (dil - 1)) == 0)
                    sn = jnp.where(valid_n, jnp.sum(nk * q_cols[t], axis=0, keepdims=True), NEG)
                    m = jnp.maximum(jnp.max(jnp.where(valid, s_all, NEG), axis=-1, keepdims=True),
                                    jnp.max(sn, axis=-1, keepdims=True))
                    m_lanes = jnp.where(valid, m, m_lanes)
                    any_valid = any_valid | valid
                    valids.append(valid)
                    pnews.append(jnp.where(valid_n, jnp.exp(sn - m), 0.0))
                    stats[t][g] = (None, m, None)
                p_all = jnp.where(any_valid, jnp.exp(s_all - m_lanes), 0.0)
                acc = jnp.zeros((HEAD_DIM, LANES), F32)
                for c in range(n_ch):
                    acc = acc + c_refs[g][1, hl, :, c * LANES:(c + 1) * LANES] * p_all[:, c * LANES:(c + 1) * LANES]
                for t, valid, pn in zip(toks, valids, pnews):
                    l = jnp.sum(jnp.where(valid, p_all, 0.0), axis=-1, keepdims=True) + jnp.sum(pn, axis=-1, keepdims=True)
                    own = acc if not shared else jnp.where(lane_tok == t, acc, 0.0)
                    o = jnp.sum(own, axis=-1, keepdims=True) + jnp.sum(nv * pn, axis=-1, keepdims=True)
                    stats[t][g] = (o, stats[t][g][1], l)
            lane = lax.broadcasted_iota(jnp.int32, (HEAD_DIM, LANES), 1)
            for kv, new in ((0, nk), (1, nv)):
                shifted = pltpu.roll(c_refs[g][kv, hl], w - n_tok, axis=1)
                last = shifted[:, w - LANES:]
                for t in range(n_tok):
                    last = jnp.where(lane == LANES - n_tok + t, new[:, t:t + 1], last)
                if w > LANES:
                    oc_refs[g][kv, hl, :, 0:w - LANES] = shifted[:, 0:w - LANES]
                oc_refs[g][kv, hl, :, w - LANES:] = last
        out = jnp.zeros((HEAD_DIM, TOK_PAD), F32)
        for t in range(n_tok):
            parts = stats[t]
            mx = functools.reduce(jnp.maximum, [m for _, m, _ in parts])
            num = sum(jnp.exp(m - mx) * o for o, m, _ in parts)
            den = sum(jnp.exp(m - mx) * l for _, m, l in parts)
            out = jnp.where(lane_t == t, num / den, out)
        o_ref[hl] = out


def _dil_sample(q, news, caches, *, n_tok):
    nb = q.shape[0]
    hs = DIL_SAMPLE_HEADS
    for g, (w, d) in enumerate(DIL_PAIRS):
        assert caches[g].shape[-1] == w and w // d == DIL_KEYS - 1
        assert d & (d - 1) == 0
    blk = lambda a: pl.BlockSpec((None, a.shape[1], hs) + a.shape[3:], lambda b, hh: (b, 0, hh, 0, 0))
    o_spec = pl.BlockSpec((None, hs, HEAD_DIM, TOK_PAD), lambda b, hh: (b, hh, 0, 0))
    return pl.pallas_call(
        functools.partial(_dil_sample_kernel, n_tok=n_tok),
        grid=(nb, DIL_HEADS // hs),
        in_specs=[blk(q)] + [blk(a) for a in news] + [blk(a) for a in caches],
        out_specs=[o_spec] + [blk(a) for a in caches],
        out_shape=[jax.ShapeDtypeStruct((nb, DIL_HEADS, HEAD_DIM, TOK_PAD), F32)]
                  + [jax.ShapeDtypeStruct(a.shape, F32) for a in caches],
        compiler_params=_cparams("parallel", "parallel"),
        name="dil_sample",
    )(q, *news, *caches)


FF_CHUNK = 1024


def _post_kernel(*refs, n_cat, combine):
    h_ref = refs[0]
    mix_refs = refs[1:1 + (2 * N_DIL if combine else n_cat)]
    (wo_ref, gmlp_ref, w1_ref, w2_ref, gple_ref, wg_ref, p_ref, wp_ref, out_ref) = refs[1 + len(mix_refs):1 + len(mix_refs) + 9]
    hd = HEAD_DIM
    if combine:
        comb_scr = refs[-1]
        o_refs, ml_refs = mix_refs[:N_DIL], mix_refs[N_DIL:]
        for h in range(DIL_HEADS):
            ms = [r[:, h:h + 1] for r in ml_refs]
            ls = [r[:, DIL_HEADS + h:DIL_HEADS + h + 1] for r in ml_refs]
            mx = functools.reduce(jnp.maximum, ms)
            ws = [jnp.exp(m - mx) for m in ms]
            num = sum(w * r[:, h * hd:(h + 1) * hd] for w, r in zip(ws, o_refs))
            den = sum(w * l for w, l in zip(ws, ls))
            comb_scr[:, h * hd:(h + 1) * hd] = (num / den).astype(BF16)
        y = _dot(comb_scr[...], wo_ref[...])
    else:
        y = None
        for k, r in enumerate(mix_refs):
            part = _dot(r[...].astype(BF16), wo_ref[k * r.shape[1]:(k + 1) * r.shape[1], :])
            y = part if y is None else y + part
    h1 = h_ref[...] + y
    hn = _rms_rows(h1, gmlp_ref[...]).astype(BF16)
    acc = jnp.zeros(h1.shape, F32)
    for c in range(D_FF // FF_CHUNK):
        u = _dot(hn, w1_ref[:, c * FF_CHUNK:(c + 1) * FF_CHUNK])
        acc = acc + _dot(jnp.square(jnp.maximum(u, 0.0)).astype(BF16), w2_ref[c * FF_CHUNK:(c + 1) * FF_CHUNK, :])
    h2 = h1 + acc
    gate = jax.nn.sigmoid(_dot(_rms_rows(h2, gple_ref[...]).astype(BF16), wg_ref[...]))
    out_ref[...] = h2 + gate * _dot(p_ref[...].astype(BF16), wp_ref[...])


def _post(h, mix, wo, gmlp, w1, w2, gple, wg, p, wp, *, combine, tm):
    n = h.shape[0]
    row = lambda a: pl.BlockSpec((tm, a.shape[1]), lambda i: (i, 0))
    weights = (wo, gmlp, w1, w2, gple, wg)
    wspec = lambda a: pl.BlockSpec(a.shape, lambda i: (0, 0), pipeline_mode=pl.Buffered(1))
    kern = functools.partial(_post_kernel, n_cat=len(mix), combine=combine)
    return pl.pallas_call(
        kern,
        grid=(n // tm,),
        in_specs=[row(h)] + [row(a) for a in mix] + [wspec(a) for a in weights] + [row(p), wspec(wp)],
        out_specs=row(h),
        out_shape=jax.ShapeDtypeStruct(h.shape, F32),
        scratch_shapes=[pltpu.VMEM((tm, DIL_HEADS * HEAD_DIM), BF16)] if combine else [],
        compiler_params=_cparams("parallel"),
        name="post",
    )(h, *mix, *weights, p, wp)


def _rope_tables(pos):
    half = HEAD_DIM // 2
    freq = ROPE_THETA ** (-jnp.arange(half, dtype=F32) / half)
    ang = pos.astype(F32)[:, None] * freq[None, :]
    cos, sin = jnp.cos(ang), jnp.sin(ang)
    return jnp.tile(jnp.concatenate([cos, cos], axis=1), (1, 2)), jnp.tile(jnp.concatenate([-sin, sin], axis=1), (1, 2))


def _cmp_to_slc(n_cmp, n_slc, rows_pad, cols_pad):
    cs = jnp.arange(n_cmp)[:, None] * CMP_STRIDE
    ss = jnp.arange(n_slc)[None, :] * SLC_BLOCK
    ov = jnp.maximum(jnp.minimum(cs + CMP_LEN, ss + SLC_BLOCK) - jnp.maximum(cs, ss), 0)
    return jnp.pad(ov.astype(F32) / CMP_LEN, ((0, rows_pad - n_cmp), (0, cols_pad - n_slc)))


def _round_up(x, m):
    return (x + m - 1) // m * m


def _block_diag2(w):
    z = jnp.zeros_like(w)
    return jnp.concatenate([jnp.concatenate([w, z], axis=2), jnp.concatenate([z, w], axis=2)], axis=1)


def kernel(x_prompt, x_sample, state_gla, cache_nsa_cmp, cache_nsa_slc, cache_nsa_win, cache_dil_0, cache_dil_1,
           cache_dil_2, page_table, p_prompt, p_sample, norm_mix, norm_mlp, norm_ple, a_w_in, a_w_out, gla_w_a2,
           gla_b_a, gla_g_norm, nsa_g_qk, nsa_w_phi, nsa_pe, c_w_in, c_g_qk, c_w_out, mlp_w1, mlp_w2, ple_w_proj,
           ple_w_gate):
    nb_p, seq, _ = x_prompt.shape
    nb_s, n_tok, _ = x_sample.shape
    depth = norm_mix.shape[0]
    past = page_table.shape[1] * PAGE_SIZE
    assert n_tok < CMP_STRIDE and n_tok <= 8 and seq % 512 == 0
    tm_p = 256
    tm_s = min(256, nb_s * n_tok)
    cs_p, sn_p = _rope_tables(jnp.arange(seq))
    cs_s, sn_s = _rope_tables(past + jnp.arange(nb_s * n_tok) % n_tok)
    hp = x_prompt.reshape(nb_p * seq, D_MODEL)
    hs = x_sample.reshape(nb_s * n_tok, D_MODEL)
    dil_caches = (cache_dil_0, cache_dil_1, cache_dil_2)
    outs = {k: [] for k in ("gla_p", "gla_s", "cmp_p", "cmp_s", "slc_p", "slc_s", "win_p", "win_s")}
    dil_p = [[] for _ in DIL_PAIRS]
    dil_s = [[] for _ in DIL_PAIRS]
    kvrow = (2, NSA_KV_HEADS, HEAD_DIM)
    wd = DIL_HEADS * HEAD_DIM

    for i in range(depth):
        j = i // 2
        gmix = norm_mix[i][None, :]
        if i % 2 == 0:
            w_in = a_w_in[j]
            pad_cols = lambda a, width: jnp.pad(a, ((0, 0), (0, width - a.shape[1])))
            w_pad = jnp.concatenate([w_in[:, :1536], pad_cols(w_in[:, 1536:1552], LANES), w_in[:, 1552:2832],
                                     pad_cols(w_in[:, 2832:], LANES)], axis=1).astype(BF16)
            wa2 = jnp.pad(gla_w_a2[j], ((0, LANES - GLA_LOWRANK), (0, 0))).astype(BF16)
            ba = gla_b_a[j][None, :]
            gqk = jnp.tile(nsa_g_qk[j], (1, NSA_HEADS))
            gnorm = gla_g_norm[j][None, :]
            wk, wv = (_block_diag2(nsa_w_phi[j, c]).astype(BF16) for c in range(2))
            pek, pev = (jnp.tile(nsa_pe[j, c], (1, 2)) for c in range(2))
            wo = a_w_out[j].astype(BF16)

            gla, qc, qr, gates, rc, _, _, sbf, wbf, rc_t, rs_t, rw_t = _proj_a(hp, gmix, w_pad, wa2, ba, gqk, cs_p, sn_p,
                                                                               tm_p)
            s0 = jnp.zeros((nb_p, GLA_HEADS, GLA_DK, GLA_DV), F32)
            chunk = min(GLA_CHUNK, seq)
            o_gla, s_p = _gla(gla, s0, gnorm, nb=nb_p, rows_per_seq=seq, chunk=chunk, n_valid=chunk, out_dtype=BF16)
            kc, vc = _compress_prompt(rc, nb_p, seq, wk, wv, pek, pev)
            n_cmp_pad = seq // CMP_STRIDE
            mcs = _cmp_to_slc(n_cmp_pad - 1, seq // SLC_BLOCK, n_cmp_pad, _round_up(seq // SLC_BLOCK, LANES))
            o_nsa = _nsa_prompt(qc, qr, gates, kc, vc, sbf, wbf, mcs, nb=nb_p, seq=seq, qb=NSA_Q_BLOCK)
            mix_p = (o_gla, o_nsa)
            outs["gla_p"].append(s_p)
            rows_out = lambda a: a.reshape((nb_p,) + kvrow + (a.shape[-1],)).transpose(0, 4, 1, 2, 3)
            outs["cmp_p"].append(rows_out(rc_t))
            outs["slc_p"].append(rows_out(rs_t))
            outs["win_p"].append(rows_out(rw_t[:, :, seq - min(NSA_WINDOW, seq):]))

            gla, qc, qr, gates, rc, rs, rw = _proj_a(hs, gmix, w_pad, wa2, ba, gqk, cs_s, sn_s, tm_s)[:7]
            gla8 = jnp.pad(gla.reshape(nb_s, n_tok, G_END), ((0, 0), (0, 8 - n_tok), (0, 0))).reshape(nb_s * 8, G_END)
            o_gla8, s_s = _gla(gla8, state_gla[j], gnorm, nb=nb_s, rows_per_seq=8, chunk=8, n_valid=n_tok, out_dtype=F32)
            o_gla = o_gla8.reshape(nb_s, 8, -1)[:, :n_tok].reshape(nb_s * n_tok, -1)
            kc, vc = _compress_pages(cache_nsa_cmp[j].transpose(0, 2, 3, 4, 1), page_table, wk, wv, pek, pev)
            n_cmp_pad = past // CMP_STRIDE
            n_slc = (past + n_tok + SLC_BLOCK - 1) // SLC_BLOCK
            mcs = _cmp_to_slc(n_cmp_pad - 1, n_slc, n_cmp_pad, LANES)
            m_rows = NSA_GROUP * n_tok
            regroup = lambda a: a.reshape(nb_s, n_tok, NSA_KV_HEADS, NSA_GROUP, -1).transpose(0, 2, 3, 1, 4).reshape(
                nb_s, NSA_KV_HEADS, m_rows, -1)
            g16 = jnp.pad(regroup(gates[:, :3 * NSA_HEADS]), ((0, 0), (0, 0), (0, 0), (0, LANES - 3)))
            tok_id = jnp.arange(m_rows) % n_tok
            amat = (tok_id[:, None] == tok_id[None, :]).astype(F32)
            pad_rows = lambda a: jnp.pad(a.reshape(nb_s, n_tok, 256), ((0, 0), (0, TOK_PAD - n_tok), (0, 0)))
            keyt = lambda a: a.transpose(0, 2, 3, 4, 1)
            neww_t = jnp.pad(keyt(rw.reshape((nb_s, n_tok) + kvrow)), [(0, 0)] * 4 + [(0, TOK_PAD - n_tok)])
            o16, win_new = _nsa_sample(page_table, regroup(qc), regroup(qr), g16, kc, vc, keyt(cache_nsa_slc[j]),
                                       pad_rows(rs), keyt(cache_nsa_win[j]), pad_rows(rw), neww_t, mcs, amat,
                                       n_tok=n_tok)
            o_nsa = o16.reshape(nb_s, NSA_KV_HEADS, NSA_GROUP, n_tok, HEAD_DIM).transpose(0, 3, 1, 2, 4).reshape(
                nb_s * n_tok, NSA_HEADS * HEAD_DIM)
            mix_s = (o_gla, o_nsa)
            outs["gla_s"].append(s_s)
            outs["cmp_s"].append(rc.reshape((nb_s, n_tok) + kvrow))
            outs["slc_s"].append(rs.reshape((nb_s, n_tok) + kvrow))
            outs["win_s"].append(win_new.transpose(0, 4, 1, 2, 3))
            combine = False
        else:
            w_c = c_w_in[j].astype(BF16)
            gq = jnp.tile(c_g_qk[j][:, 0], (1, DIL_HEADS))
            gk = jnp.tile(c_g_qk[j][:, 1], (1, DIL_HEADS))
            wo = c_w_out[j].astype(BF16)
            dilrow = (2, DIL_HEADS, HEAD_DIM)

            res = _proj_c(hp, gmix, w_c, gq, gk, cs_p, sn_p, tm_p, True)
            qs, rows, tails = res[:N_DIL], res[N_DIL:2 * N_DIL], res[2 * N_DIL:]
            o_parts, ml_parts = [], []
            for g, (w, d) in enumerate(DIL_PAIRS):
                o_g, ml_g = _dil_prompt(qs[g], rows[g], nb=nb_p, seq=seq, dil=d, tq=128)
                o_parts.append(o_g)
                ml_parts.append(ml_g)
                dil_p[g].append(tails[g].reshape((nb_p,) + dilrow + (tails[g].shape[-1],)).transpose(0, 4, 1, 2, 3))
            mix_p = tuple(o_parts) + tuple(ml_parts)

            res = _proj_c(hs, gmix, w_c, gq, gk, cs_s, sn_s, tm_s, False)
            qs, rows = res[:N_DIL], res[N_DIL:]
            pad_tok = lambda a: jnp.pad(a, [(0, 0)] * (a.ndim - 1) + [(0, TOK_PAD - n_tok)])
            q_t = pad_tok(jnp.stack([q.astype(F32).reshape(nb_s, n_tok, DIL_HEADS, HEAD_DIM).transpose(0, 2, 3, 1)
                                     for q in qs], axis=1))
            news_t = [pad_tok(r.reshape((nb_s, n_tok) + dilrow).transpose(0, 2, 3, 4, 1)) for r in rows]
            caches_t = [c[j].transpose(0, 2, 3, 4, 1) for c in dil_caches]
            o_t, *new_caches = _dil_sample(q_t, news_t, caches_t, n_tok=n_tok)
            mix_s = (o_t[..., :n_tok].transpose(0, 3, 1, 2).reshape(nb_s * n_tok, wd),)
            for g in range(N_DIL):
                dil_s[g].append(new_caches[g].transpose(0, 4, 1, 2, 3))
            combine = True

        lw = (norm_mlp[i][None, :], mlp_w1[i].astype(BF16), mlp_w2[i].astype(BF16), norm_ple[i][None, :],
              ple_w_gate[i].astype(BF16))
        wp = ple_w_proj[i].astype(BF16)
        hp = _post(hp, mix_p, wo, *lw, p_prompt[i].reshape(nb_p * seq, PLE_DIM), wp, combine=combine, tm=tm_p)
        hs = _post(hs, mix_s, wo, *lw, p_sample[i].reshape(nb_s * n_tok, PLE_DIM), wp, combine=False, tm=tm_s)

    st = jnp.stack
    return (hp.reshape(x_prompt.shape), hs.reshape(x_sample.shape),
            st(outs["gla_p"]), st(outs["gla_s"]), st(outs["cmp_p"]), st(outs["cmp_s"]),
            st(outs["slc_p"]), st(outs["slc_s"]), st(outs["win_p"]), st(outs["win_s"]),
            st(dil_p[0]), st(dil_s[0]), st(dil_p[1]), st(dil_s[1]), st(dil_p[2]), st(dil_s[2]))
```

```python
import functools

import jax
import jax.numpy as jnp
from jax import lax
from jax.experimental import pallas as pl
from jax.experimental.pallas import tpu as pltpu

F32 = jnp.float32
BF16 = jnp.bfloat16
HIGHEST = lax.Precision.HIGHEST

D_MODEL = 1024
PAGE_SIZE = 128
HEAD_DIM = 64
GLA_HEADS = 4
GLA_DK = 64
GLA_DV = 128
GLA_LOWRANK = 16
GLA_TAU = 16.0
GLA_CHUNK = 64
GLA_SUB = 16
NSA_HEADS = 8
NSA_KV_HEADS = 2
NSA_GROUP = NSA_HEADS // NSA_KV_HEADS
CMP_LEN = 32
CMP_STRIDE = 16
SLC_BLOCK = 64
SLC_TOPK = 16
NSA_WINDOW = 512
DIL_PAIRS = ((128, 1), (512, 4), (2048, 16))
N_DIL = 3
DIL_HEADS = 8
DIL_KEYS = 129
D_FF = 4 * D_MODEL
PLE_DIM = 256
ROPE_THETA = 10000.0
EPS = 1e-6
NEG = -1e30
BIG = 1e30
TINY = 1e-20
REMOVED = -3e38

LANES = 128
TOK_PAD = 8
NSA_KEY_TILE = 512
NSA_Q_BLOCK = 256
VMEM_LIMIT = 56 * 1024 * 1024
QSCALE = HEAD_DIM ** -0.5

A_GQ, A_GK, A_GV, A_GR, A_GA, A_NQ, A_NKV, A_NG, A_END = 0, 256, 512, 1024, 1536, 1664, 2176, 2944, 3072
G_Q, G_K, G_V, G_R, G_LG, G_END = 0, 256, 512, 1024, 1536, 1792


def _cparams(*sem):
    return pltpu.CompilerParams(dimension_semantics=sem, vmem_limit_bytes=VMEM_LIMIT)


def _rms_rows(x, g):
    return x * lax.rsqrt(jnp.mean(x * x, axis=-1, keepdims=True) + EPS) * g


def _cols(x):
    return [x[:, c * LANES:(c + 1) * LANES] for c in range(x.shape[1] // LANES)]


def _rms_heads(x, g):
    out = []
    for c, xc in enumerate(_cols(x)):
        low = lax.broadcasted_iota(jnp.int32, xc.shape, 1) < HEAD_DIM
        sq = xc * xc
        s_lo = jnp.sum(jnp.where(low, sq, 0.0), axis=-1, keepdims=True)
        s_hi = jnp.sum(jnp.where(low, 0.0, sq), axis=-1, keepdims=True)
        ms = jnp.where(low, s_lo, s_hi) * (1.0 / HEAD_DIM)
        out.append(xc * lax.rsqrt(ms + EPS) * g[:, c * LANES:(c + 1) * LANES])
    return jnp.concatenate(out, axis=1) if len(out) > 1 else out[0]


def _rope(x, cs, sn):
    out = []
    for xc in _cols(x):
        lane = lax.broadcasted_iota(jnp.int32, xc.shape, 1)
        swapped = jnp.where((lane & 32) == 0, pltpu.roll(xc, LANES - 32, axis=1), pltpu.roll(xc, 32, axis=1))
        out.append(xc * cs + swapped * sn)
    return jnp.concatenate(out, axis=1) if len(out) > 1 else out[0]


def _dot(a, b):
    return jnp.dot(a, b, preferred_element_type=F32)


def _dot_nt(a, b):
    return lax.dot_general(a, b, (((1,), (1,)), ((), ())), preferred_element_type=F32)


def _dot_tn(a, b):
    return lax.dot_general(a, b, (((0,), (0,)), ((), ())), preferred_element_type=F32)


def _softmax_full(s, mask):
    s = jnp.where(mask, s, NEG)
    m = jnp.max(s, axis=-1, keepdims=True)
    p = jnp.where(mask, jnp.exp(s - m), 0.0)
    l = jnp.sum(p, axis=-1, keepdims=True)
    return p, l


def _proj_a_kernel(x_ref, gmix_ref, w_ref, wa2_ref, ba_ref, gqk_ref, cs_ref, sn_ref,
                   gla_ref, qc_ref, qr_ref, gates_ref, rc_ref, rs_ref, rw_ref, sbf_ref, wbf_ref,
                   rct_ref, rst_ref, rwt_ref):
    hn = _rms_rows(x_ref[...], gmix_ref[...])
    z = _dot(hn.astype(BF16), w_ref[...])
    cs, sn = cs_ref[...], sn_ref[...]
    gla_ref[:, G_Q:G_K] = z[:, A_GQ:A_GK] * (GLA_DK ** -0.5)
    gla_ref[:, G_K:G_LG] = z[:, A_GK:A_GA]
    pre = _dot(z[:, A_GA:A_NQ].astype(BF16), wa2_ref[...]) + ba_ref[...]
    gla_ref[:, G_LG:G_END] = jax.nn.log_sigmoid(pre) / GLA_TAU
    qn = _rms_heads(z[:, A_NQ:A_NKV], gqk_ref[0:1, :])
    qc_ref[...] = (qn * QSCALE).astype(BF16)
    qr_ref[...] = (_rope(qn, cs, sn) * QSCALE).astype(BF16)
    gates_ref[...] = jax.nn.sigmoid(z[:, A_NG:A_END])
    kv = [z[:, A_NKV + i * LANES:A_NKV + (i + 1) * LANES] for i in range(6)]
    rc_ref[:, 0:LANES] = _rms_heads(kv[0], gqk_ref[1:2, 0:LANES])
    rc_ref[:, LANES:] = kv[1]
    ks = _rope(_rms_heads(kv[2], gqk_ref[2:3, 0:LANES]), cs, sn)
    rs_ref[:, 0:LANES] = ks
    rs_ref[:, LANES:] = kv[3]
    kw = _rope(_rms_heads(kv[4], gqk_ref[3:4, 0:LANES]), cs, sn)
    rw_ref[:, 0:LANES] = kw
    rw_ref[:, LANES:] = kv[5]
    sbf_ref[:, 0:LANES] = ks.astype(BF16)
    sbf_ref[:, LANES:] = kv[3].astype(BF16)
    wbf_ref[:, 0:LANES] = kw.astype(BF16)
    wbf_ref[:, LANES:] = kv[5].astype(BF16)
    for src, dst in ((rc_ref, rct_ref), (rs_ref, rst_ref), (rw_ref, rwt_ref)):
        dst[...] = src[...].T


def _const_spec(shape):
    nd = len(shape)
    return pl.BlockSpec(shape, lambda *_: (0,) * nd)


def _proj_a(x, gmix, w, wa2, ba, gqk, cs, sn, tm):
    n = x.shape[0]
    tiles_per_seq = cs.shape[0] // tm
    row = lambda width: pl.BlockSpec((tm, width), lambda i: (i, 0))
    tab = pl.BlockSpec((tm, LANES), lambda i: (i % tiles_per_seq, 0))
    widths = (G_END, 512, 512, LANES, 256, 256, 256, 256, 256)
    dtypes = (F32, BF16, BF16, F32, F32, F32, F32, BF16, BF16)
    return pl.pallas_call(
        _proj_a_kernel,
        grid=(n // tm,),
        in_specs=[row(D_MODEL), _const_spec(gmix.shape), _const_spec(w.shape), _const_spec(wa2.shape),
                  _const_spec(ba.shape), _const_spec(gqk.shape), tab, tab],
        out_specs=[row(wd) for wd in widths]
                  + [pl.BlockSpec((None, 256, tm), lambda i: (i // tiles_per_seq, 0, i % tiles_per_seq))] * 3,
        out_shape=[jax.ShapeDtypeStruct((n, wd), dt) for wd, dt in zip(widths, dtypes)]
                  + [jax.ShapeDtypeStruct((n // cs.shape[0], 256, cs.shape[0]), F32)] * 3,
        compiler_params=_cparams("parallel"),
        name="proj_a",
    )(x, gmix, w, wa2, ba, gqk, cs, sn)


def _gla_kernel(gla_ref, s0_ref, gnorm_ref, o_ref, sout_ref, s_scr, *, chunk, sub, n_chunks, n_valid):
    i = pl.program_id(1)
    n_pairs = GLA_HEADS // 2
    n_sub = chunk // sub
    dk2, dv2 = 2 * GLA_DK, 2 * GLA_DV

    @pl.when(i == 0)
    def _():
        s_scr[...] = jnp.zeros(s_scr.shape, F32)
        for h in range(GLA_HEADS):
            p, e = divmod(h, 2)
            s_scr[p, e * GLA_DK:(e + 1) * GLA_DK, e * GLA_DV:(e + 1) * GLA_DV] = s0_ref[h]

    tri = (lax.broadcasted_iota(jnp.int32, (chunk, chunk), 0)
           >= lax.broadcasted_iota(jnp.int32, (chunk, chunk), 1)).astype(F32)
    low = lax.broadcasted_iota(jnp.int32, (chunk, dk2), 1) < GLA_DK
    low_sub = lax.broadcasted_iota(jnp.int32, (sub, dk2), 1) < GLA_DK
    t_sub = lax.broadcasted_iota(jnp.int32, (sub, 1), 0)
    row_blk = lax.broadcasted_iota(jnp.int32, (chunk, chunk), 0) // sub
    col_idx = lax.broadcasted_iota(jnp.int32, (chunk, chunk), 1)
    diag_blocks = ((lax.broadcasted_iota(jnp.int32, (dk2, dv2), 0) < GLA_DK)
                   == (lax.broadcasted_iota(jnp.int32, (dk2, dv2), 1) < GLA_DV))
    eye = (lax.broadcasted_iota(jnp.int32, (dk2, dk2), 0) == lax.broadcasted_iota(jnp.int32, (dk2, dk2), 1))
    gnorm = gnorm_ref[...]

    def do_chunk(c, carry):
        rows = pl.ds(pl.multiple_of(c * chunk, chunk), chunk)
        b_all = jnp.dot(tri, gla_ref[rows, G_LG:G_END], precision=HIGHEST, preferred_element_type=F32)
        for p in range(n_pairs):
            q2 = gla_ref[rows, G_Q + p * dk2:G_Q + (p + 1) * dk2]
            k2 = gla_ref[rows, G_K + p * dk2:G_K + (p + 1) * dk2]
            v2 = gla_ref[rows, G_V + p * dv2:G_V + (p + 1) * dv2]
            b2 = b_all[:, p * dk2:(p + 1) * dk2]
            s_pair = s_scr[p]
            o2 = _dot((q2 * jnp.exp(b2)).astype(BF16), s_pair.astype(BF16))

            if n_sub > 1:
                bref_rows = jnp.concatenate(
                    [b2[0:sub]] + [jnp.broadcast_to(b2[i * sub - 1:i * sub], (sub, dk2)) for i in range(1, n_sub)], axis=0)
                qt = q2 * jnp.exp(jnp.minimum(b2 - bref_rows, 0.0))
                qt_even = jnp.where(low, qt, 0.0).astype(BF16)
                qt_odd = jnp.where(low, 0.0, qt).astype(BF16)
                a_even = jnp.zeros((chunk, chunk), F32)
                a_odd = jnp.zeros((chunk, chunk), F32)
                for i in range(1, n_sub):
                    kt = (k2 * jnp.exp(jnp.minimum(b2[i * sub - 1:i * sub] - b2, 0.0))).astype(BF16)
                    take = (row_blk == i) & (col_idx < i * sub)
                    a_even = jnp.where(take, _dot_nt(qt_even, kt), a_even)
                    a_odd = jnp.where(take, _dot_nt(qt_odd, kt), a_odd)
                o2 = o2 + jnp.concatenate([_dot(a_even.astype(BF16), v2[:, 0:GLA_DV].astype(BF16)),
                                           _dot(a_odd.astype(BF16), v2[:, GLA_DV:].astype(BF16))], axis=1)

            diag = []
            for i in range(n_sub):
                sl = slice(i * sub, (i + 1) * sub)
                qs, ks, bs, vs = q2[sl], k2[sl], b2[sl], v2[sl]
                od = jnp.zeros((sub, dv2), F32)
                for jj in range(max(0, min(sub, n_valid - i * sub))):
                    w = qs * ks[jj:jj + 1] * jnp.exp(jnp.minimum(bs - bs[jj:jj + 1], 0.0))
                    keep = t_sub >= jj
                    a_e = jnp.where(keep, jnp.sum(jnp.where(low_sub, w, 0.0), axis=-1, keepdims=True), 0.0)
                    a_o = jnp.where(keep, jnp.sum(jnp.where(low_sub, 0.0, w), axis=-1, keepdims=True), 0.0)
                    od = od + jnp.concatenate([a_e * vs[jj:jj + 1, 0:GLA_DV], a_o * vs[jj:jj + 1, GLA_DV:]], axis=1)
                diag.append(od)
            o2 = o2 + (jnp.concatenate(diag, axis=0) if n_sub > 1 else diag[0])

            b_last = b2[chunk - 1:chunk, :]
            kdec = k2 * jnp.exp(b_last - b2)
            upd = _dot_tn(kdec.astype(BF16), v2.astype(BF16))
            decay_col = jnp.sum(jnp.where(eye, jnp.exp(b_last), 0.0), axis=1, keepdims=True)
            s_scr[p] = decay_col * s_pair + jnp.where(diag_blocks, upd, 0.0)
            for e in range(2):
                h = 2 * p + e
                on = _rms_rows(o2[:, e * GLA_DV:(e + 1) * GLA_DV], gnorm)
                r = gla_ref[rows, G_R + h * GLA_DV:G_R + (h + 1) * GLA_DV]
                o_ref[rows, h * GLA_DV:(h + 1) * GLA_DV] = (on * (r * jax.nn.sigmoid(r))).astype(o_ref.dtype)
        return carry
    lax.fori_loop(0, n_chunks, do_chunk, 0)

    @pl.when(i == pl.num_programs(1) - 1)
    def _():
        for h in range(GLA_HEADS):
            p, e = divmod(h, 2)
            sout_ref[h] = s_scr[p, e * GLA_DK:(e + 1) * GLA_DK, e * GLA_DV:(e + 1) * GLA_DV]


def _gla(gla, s0, gnorm, *, nb, rows_per_seq, chunk, n_valid, out_dtype):
    step_rows = min(rows_per_seq, 8 * chunk)
    steps = rows_per_seq // step_rows
    kern = functools.partial(_gla_kernel, chunk=chunk, sub=min(chunk, GLA_SUB), n_chunks=step_rows // chunk,
                             n_valid=n_valid)
    return pl.pallas_call(
        kern,
        grid=(nb, steps),
        in_specs=[pl.BlockSpec((step_rows, G_END), lambda b, i: (b * steps + i, 0)),
                  pl.BlockSpec((None, GLA_HEADS, GLA_DK, GLA_DV), lambda b, i: (b, 0, 0, 0)),
                  _const_spec(gnorm.shape)],
        out_specs=[pl.BlockSpec((step_rows, GLA_HEADS * GLA_DV), lambda b, i: (b * steps + i, 0)),
                   pl.BlockSpec((None, GLA_HEADS, GLA_DK, GLA_DV), lambda b, i: (b, 0, 0, 0))],
        out_shape=[jax.ShapeDtypeStruct((nb * rows_per_seq, GLA_HEADS * GLA_DV), out_dtype),
                   jax.ShapeDtypeStruct((nb, GLA_HEADS, GLA_DK, GLA_DV), F32)],
        scratch_shapes=[pltpu.VMEM((GLA_HEADS // 2, 2 * GLA_DK, 2 * GLA_DV), F32)],
        compiler_params=_cparams("parallel", "arbitrary"),
        name="gla",
    )(gla, s0, gnorm)


def _compress_kernel(*refs, n_parts, n_prefetch, transposed_pages):
    refs = refs[n_prefetch:]
    x_refs = refs[:n_parts]
    row_w = 2 * LANES
    if transposed_pages:
        wk_ref, wv_ref, pek_ref, pev_ref, kc_ref, vc_ref, sh_scr, page_scr, x_scr = refs[n_parts:]
        mp = PAGE_SIZE // CMP_STRIDE
        m = n_parts * mp
        for pg, r in enumerate(x_refs):
            page = r[...].reshape(row_w, PAGE_SIZE).T
            for half in range(2):
                page_scr[half] = page[:, half * LANES:(half + 1) * LANES]
            for p in range(CMP_STRIDE):
                for half in range(2):
                    x_scr[p, pg * mp:(pg + 1) * mp, half * LANES:(half + 1) * LANES] = (
                        page_scr[half, pl.ds(p, mp, stride=CMP_STRIDE), :])
        chunk_rows = lambda p, off: x_scr[p, :, off:off + LANES]
    else:
        wk_ref, wv_ref, pek_ref, pev_ref, kc_ref, vc_ref, sh_scr = refs[n_parts:]
        x = x_refs[0][...]
        m = x.shape[0]
        chunk_rows = lambda p, off: x[:, p * row_w + off:p * row_w + off + LANES]

    def branch(off, w_ref, pe_ref, out_ref):
        lo = jnp.zeros((m, LANES), F32)
        hi = jnp.zeros((m, LANES), F32)
        for p in range(CMP_STRIDE):
            xp = chunk_rows(p, off)
            lo = lo + _dot((xp + pe_ref[p:p + 1, :]).astype(BF16), w_ref[p])
            hi = hi + _dot((xp + pe_ref[CMP_STRIDE + p:CMP_STRIDE + p + 1, :]).astype(BF16), w_ref[CMP_STRIDE + p])
        sh_scr[pl.ds(0, m), :] = hi
        sh_scr[pl.ds(m, 8), :] = jnp.zeros((8, LANES), F32)
        out_ref[...] = (lo + sh_scr[pl.ds(1, m), :]).astype(out_ref.dtype)
    branch(0, wk_ref, pek_ref, kc_ref)
    branch(LANES, wv_ref, pev_ref, vc_ref)


def _compress_call(n_parts, m, grid, x_specs, out_map, out_rows, prefetch, transposed_pages):
    kern = functools.partial(_compress_kernel, n_parts=n_parts, n_prefetch=prefetch, transposed_pages=transposed_pages)
    w_shape = (CMP_LEN, LANES, LANES)
    pe_shape = (CMP_LEN, LANES)
    in_specs = list(x_specs) + [_const_spec(w_shape), _const_spec(w_shape), _const_spec(pe_shape), _const_spec(pe_shape)]
    out_specs = [pl.BlockSpec((m, LANES), out_map), pl.BlockSpec((m, LANES), out_map)]
    scratch = [pltpu.VMEM((m + 8, LANES), F32)]
    if transposed_pages:
        scratch += [pltpu.VMEM((2, PAGE_SIZE, LANES), F32), pltpu.VMEM((CMP_STRIDE, m, 2 * LANES), F32)]
    gs = pltpu.PrefetchScalarGridSpec(num_scalar_prefetch=prefetch, grid=grid, in_specs=in_specs, out_specs=out_specs,
                                      scratch_shapes=scratch)
    return pl.pallas_call(
        kern, grid_spec=gs,
        out_shape=[jax.ShapeDtypeStruct((out_rows, LANES), BF16)] * 2,
        compiler_params=_cparams("parallel"),
        name="nsa_compress",
    )


def _compress_prompt(rows_c, nb, seq, wk, wv, pek, pev):
    m = seq // CMP_STRIDE
    x = rows_c.reshape(-1, CMP_STRIDE * 2 * LANES)
    call = _compress_call(1, m, (nb,), [pl.BlockSpec((m, CMP_STRIDE * 2 * LANES), lambda b: (b, 0))],
                          lambda b: (b, 0), nb * m, 0, False)
    return call(x, wk, wv, pek, pev)


def _compress_pages(pool_t, page_table, wk, wv, pek, pev):
    nb, n_pages = page_table.shape
    specs = [pl.BlockSpec((None,) + pool_t.shape[1:], functools.partial(lambda pg, b, pt: (pt[b, pg], 0, 0, 0, 0), pg))
             for pg in range(n_pages)]
    m = n_pages * (PAGE_SIZE // CMP_STRIDE)
    call = _compress_call(n_pages, m, (nb,), specs, lambda b, pt: (b, 0), nb * m, 1, True)
    return call(page_table, *([pool_t] * n_pages), wk, wv, pek, pev)


def _topk_mask(imp, qpos, n_slc):
    blk = lax.broadcasted_iota(jnp.int32, imp.shape, 1)
    cur = qpos // SLC_BLOCK
    val = jnp.where((blk == cur) | (blk == 0), BIG, jnp.where(blk <= cur, imp, NEG))
    val = jnp.where(blk < n_slc, val, REMOVED)
    sel = jnp.zeros(imp.shape, jnp.bool_)
    for _ in range(min(SLC_TOPK, n_slc)):
        mx = jnp.max(val, axis=-1, keepdims=True)
        idx = jnp.min(jnp.where(val == mx, blk, imp.shape[1]), axis=-1, keepdims=True)
        pick = blk == idx
        sel = sel | pick
        val = jnp.where(pick, REMOVED, val)
    return sel.astype(BF16)


def _topk_mask_by_rank(imp, qpos, n_slc):
    blk = lax.broadcasted_iota(jnp.int32, imp.shape, 1)
    cur = qpos // SLC_BLOCK
    val = jnp.where((blk == cur) | (blk == 0), BIG, jnp.where(blk <= cur, imp, NEG))
    rank = jnp.zeros(imp.shape, jnp.int32)
    for i in range(n_slc):
        col = val[:, i:i + 1]
        rank = rank + ((col > val) | ((col == val) & (blk > i))).astype(jnp.int32)
    return ((rank < min(SLC_TOPK, n_slc)) & (blk < n_slc)).astype(BF16)


def _block_expand(n_blocks_pad, k0, n_keys):
    blk = lax.broadcasted_iota(jnp.int32, (n_blocks_pad, n_keys), 0)
    col = lax.broadcasted_iota(jnp.int32, (n_blocks_pad, n_keys), 1)
    return (blk == (k0 + col) // SLC_BLOCK).astype(BF16)


def _nsa_prompt_kernel(qc_ref, qr_ref, gates_ref, kc_ref, vc_ref, ks_ref, kw_ref, mcs_ref, o_ref,
                       qa_scr, m_scr, l_scr, acc_scr, *, qb, seq, key_tile):
    t0 = pl.program_id(1) * qb
    qpos = t0 + lax.broadcasted_iota(jnp.int32, (qb, 1), 0)
    n_cmp_pad = kc_ref.shape[0]
    n_slc = seq // SLC_BLOCK
    hd = HEAD_DIM

    rows = NSA_HEADS * qb
    lane2 = lax.broadcasted_iota(jnp.int32, (qb, 2 * hd), 1)

    def stack_heads(q_ref):
        parts = []
        for h in range(NSA_HEADS):
            g = h // NSA_GROUP
            pair = q_ref[:, (h // 2) * 2 * hd:(h // 2 + 1) * 2 * hd].astype(F32)
            if h % 2 != g:
                pair = pltpu.roll(pair, hd, axis=1)
            parts.append(jnp.where((lane2 >= g * hd) & (lane2 < (g + 1) * hd), pair, 0.0).astype(BF16))
        return jnp.concatenate(parts, axis=0)

    def row_softmax(s, mask):
        s = jnp.where(mask, s, NEG)
        p = jnp.where(mask, jnp.exp(s - jnp.max(s, axis=-1, keepdims=True)), 0.0)
        return p, jnp.maximum(jnp.sum(p, axis=-1, keepdims=True), TINY)

    q_tok = t0 + lax.broadcasted_iota(jnp.int32, (rows, 1), 0) % qb
    qr_st = stack_heads(qr_ref)

    ncol = lax.broadcasted_iota(jnp.int32, (rows, n_cmp_pad), 1)
    p, l_cmp = row_softmax(_dot_nt(stack_heads(qc_ref), kc_ref[...]), (ncol * CMP_STRIDE + CMP_LEN - 1) <= q_tok)
    p = p * (1.0 / l_cmp)
    o_cmp = _dot(p.astype(BF16), vc_ref[...])
    psum = []
    for g in range(NSA_KV_HEADS):
        blocks = [p[(g * NSA_GROUP + j) * qb:(g * NSA_GROUP + j + 1) * qb] for j in range(NSA_GROUP)]
        psum.append(functools.reduce(lambda a, b: a + b, blocks))
    imp = jnp.dot(jnp.concatenate(psum, axis=0), mcs_ref[...], precision=HIGHEST, preferred_element_type=F32)
    sel = _topk_mask(imp, jnp.concatenate([qpos] * NSA_KV_HEADS, axis=0), n_slc)

    n_blk = sel.shape[1]
    qa_scr[:, 0:2 * hd] = qr_st
    for h in range(NSA_HEADS):
        g = h // NSA_GROUP
        qa_scr[h * qb:(h + 1) * qb, 2 * hd:] = 1.0 - sel[g * qb:(g + 1) * qb]
    m_scr[...] = jnp.full((rows, LANES), NEG, F32)
    l_scr[...] = jnp.zeros((rows, LANES), F32)
    acc_scr[...] = jnp.zeros((rows, 2 * hd), F32)

    def tile(k0, causal):
        kv = ks_ref[pl.ds(k0, key_tile), :]
        key_blk = (k0 + lax.broadcasted_iota(jnp.int32, (key_tile, n_blk), 0)) // SLC_BLOCK
        own_blk = key_blk == lax.broadcasted_iota(jnp.int32, (key_tile, n_blk), 1)
        k_aug = jnp.concatenate([kv[:, 0:2 * hd], jnp.where(own_blk, NEG, 0.0).astype(BF16)], axis=1)
        s = _dot_nt(qa_scr[...], k_aug)
        if causal:
            q_row = t0 + lax.broadcasted_iota(jnp.int32, (rows, key_tile), 0) % qb
            s = jnp.where(k0 + lax.broadcasted_iota(jnp.int32, (rows, key_tile), 1) <= q_row, s, NEG)
        m_old = m_scr[...]
        m_new = jnp.maximum(m_old, jnp.max(s, axis=-1, keepdims=True))
        alpha = jnp.exp(m_old - m_new)
        p = jnp.exp(s - jnp.concatenate([m_new] * (key_tile // LANES), axis=1))
        l_scr[...] = alpha * l_scr[...] + jnp.sum(p, axis=-1, keepdims=True)
        acc_scr[...] = alpha * acc_scr[...] + _dot(p.astype(BF16), kv[:, 2 * hd:4 * hd])
        m_scr[...] = m_new

    n_full = t0 // key_tile

    def full_tile(kt, carry):
        tile(pl.multiple_of(kt * key_tile, key_tile), False)
        return carry
    lax.fori_loop(0, n_full, full_tile, 0)
    tile(pl.multiple_of(n_full * key_tile, key_tile), True)

    wlen = min(seq, NSA_WINDOW + qb)
    start = pl.multiple_of(jnp.clip(t0 - NSA_WINDOW, 0, seq - wlen), qb)
    band = kw_ref[pl.ds(start, wlen), :]
    kpos = start + lax.broadcasted_iota(jnp.int32, (rows, wlen), 1)
    p, l_win = row_softmax(_dot_nt(qr_st, band[:, 0:2 * hd]), (kpos <= q_tok) & (kpos > q_tok - NSA_WINDOW))
    o_win = _dot(p.astype(BF16), band[:, 2 * hd:4 * hd])
    for h in range(NSA_HEADS):
        g = h // NSA_GROUP
        rs, ls = slice(h * qb, (h + 1) * qb), slice(g * hd, (g + 1) * hd)
        o_w = o_win[rs, ls] / l_win[rs]
        o_slc = acc_scr[rs, ls] / jnp.maximum(l_scr[rs, 0:hd], TINY)
        gate = gates_ref[:, 3 * h:3 * h + 3]
        o = gate[:, 0:1] * o_cmp[rs, ls] + gate[:, 1:2] * o_slc + gate[:, 2:3] * o_w
        o_ref[:, h * hd:(h + 1) * hd] = o.astype(o_ref.dtype)


def _nsa_prompt(qc, qr, gates, kc, vc, ks, kw, mcs, *, nb, seq, qb):
    nq = seq // qb
    n_cmp_pad = seq // CMP_STRIDE
    key_tile = min(seq, NSA_KEY_TILE)
    n_blk = _round_up(seq // SLC_BLOCK, LANES)
    kern = functools.partial(_nsa_prompt_kernel, qb=qb, seq=seq, key_tile=key_tile)
    qspec = lambda wd: pl.BlockSpec((qb, wd), lambda b, i: (b * nq + i, 0))
    seqspec = lambda rows, wd: pl.BlockSpec((rows, wd), lambda b, i: (b, 0))
    return pl.pallas_call(
        kern,
        grid=(nb, nq),
        in_specs=[qspec(512), qspec(512), qspec(LANES), seqspec(n_cmp_pad, LANES), seqspec(n_cmp_pad, LANES),
                  seqspec(seq, 256), seqspec(seq, 256), _const_spec(mcs.shape)],
        out_specs=qspec(512),
        out_shape=jax.ShapeDtypeStruct((nb * seq, 512), BF16),
        scratch_shapes=[pltpu.VMEM((NSA_HEADS * qb, 2 * HEAD_DIM + n_blk), BF16),
                        pltpu.VMEM((NSA_HEADS * qb, LANES), F32), pltpu.VMEM((NSA_HEADS * qb, LANES), F32),
                        pltpu.VMEM((NSA_HEADS * qb, 2 * HEAD_DIM), F32)],
        compiler_params=_cparams("parallel", "arbitrary"),
        name="nsa_prompt",
    )(qc, qr, gates, kc, vc, ks, kw, mcs)


def _attend_transposed(q_bf, q_f32, kt, vt, mask, newk, newv, mask_new, n_tok):
    m_rows = q_bf.shape[0]
    s = jnp.where(mask, _dot(q_bf, kt), NEG)
    lane = lax.broadcasted_iota(jnp.int32, (m_rows, TOK_PAD), 1)
    sn = jnp.full((m_rows, TOK_PAD), NEG, F32)
    for t in range(n_tok):
        sn = jnp.where(lane == t, jnp.sum(q_f32 * newk[t:t + 1], axis=-1, keepdims=True), sn)
    sn = jnp.where(mask_new, sn, NEG)
    m = jnp.maximum(jnp.max(s, axis=-1, keepdims=True), jnp.max(sn, axis=-1, keepdims=True))
    p = jnp.where(mask, jnp.exp(s - m), 0.0)
    pn = jnp.where(mask_new, jnp.exp(sn - m), 0.0)
    l = jnp.sum(p, axis=-1, keepdims=True) + jnp.sum(pn, axis=-1, keepdims=True)
    o = _dot_nt(p.astype(BF16), vt)
    for t in range(n_tok):
        o = o + pn[:, t:t + 1] * newv[t:t + 1]
    return o / jnp.maximum(l, TINY)


def _nsa_sample_kernel(*refs, n_pages, n_tok):
    (qc_ref, qr_ref, gates_ref, kc_ref, vc_ref) = refs[1:6]
    page_refs = refs[6:6 + n_pages]
    news_ref, win_ref, neww_ref, newwt_ref, mcs_ref, amat_ref, o_ref, owin_ref = refs[6 + n_pages:]
    hd = HEAD_DIM
    past = n_pages * PAGE_SIZE
    win = win_ref.shape[-1]
    m_rows = qc_ref.shape[1]
    tok = lax.broadcasted_iota(jnp.int32, (m_rows, 1), 0) % n_tok
    qpos = past + tok
    n_cmp_pad = kc_ref.shape[0]
    n_slc = (past + n_tok + SLC_BLOCK - 1) // SLC_BLOCK

    ncol = lax.broadcasted_iota(jnp.int32, (m_rows, n_cmp_pad), 1)
    cmask = ((ncol * CMP_STRIDE + CMP_LEN - 1) <= qpos) & (ncol < n_cmp_pad - 1)
    new_pos = past + lax.broadcasted_iota(jnp.int32, (m_rows, TOK_PAD), 1)
    new_ok = (new_pos <= qpos) & (new_pos < past + n_tok)
    wpos = past - win + lax.broadcasted_iota(jnp.int32, (m_rows, win), 1)
    wmask = (wpos <= qpos) & (wpos > qpos - NSA_WINDOW) & (wpos >= 0)
    wmask_new = new_ok & (new_pos > qpos - NSA_WINDOW)

    o_cmp, imps = [], []
    for g in range(NSA_KV_HEADS):
        ksl = slice(g * hd, (g + 1) * hd)
        p, l = _softmax_full(_dot_nt(qc_ref[g], kc_ref[:, ksl]), cmask)
        p = p / jnp.maximum(l, TINY)
        o_cmp.append(_dot(p.astype(BF16), vc_ref[:, ksl]))
        psum = jnp.dot(amat_ref[...], p, precision=HIGHEST, preferred_element_type=F32)
        imps.append(jnp.dot(psum, mcs_ref[...], precision=HIGHEST, preferred_element_type=F32))
    sel_all = _topk_mask_by_rank(jnp.concatenate(imps, axis=0), jnp.concatenate([qpos] * NSA_KV_HEADS, axis=0), n_slc)
    expand = _block_expand(LANES, 0, past)
    new_blk = lax.broadcasted_iota(jnp.int32, (m_rows, LANES), 1) == past // SLC_BLOCK

    for g in range(NSA_KV_HEADS):
        ksl = slice(g * hd, (g + 1) * hd)
        vsl = slice(LANES + g * hd, LANES + (g + 1) * hd)
        qr = qr_ref[g]
        qr32 = qr.astype(F32)
        sel = sel_all[g * m_rows:(g + 1) * m_rows]
        sel_new = jnp.sum(jnp.where(new_blk, sel.astype(F32), 0.0), axis=-1, keepdims=True) > 0.5
        kt = jnp.concatenate([r[0, g].astype(BF16) for r in page_refs], axis=1)
        vt = jnp.concatenate([r[1, g].astype(BF16) for r in page_refs], axis=1)
        o_slc = _attend_transposed(qr, qr32, kt, vt, _dot(sel, expand) > 0.5,
                                   news_ref[:, ksl], news_ref[:, vsl], new_ok & sel_new, n_tok)
        o_win = _attend_transposed(qr, qr32, win_ref[0, g].astype(BF16), win_ref[1, g].astype(BF16), wmask,
                                   neww_ref[:, ksl], neww_ref[:, vsl], wmask_new, n_tok)
        gate = gates_ref[g]
        o_ref[g] = gate[:, 0:1] * o_cmp[g] + gate[:, 1:2] * o_slc + gate[:, 2:3] * o_win

        lane = lax.broadcasted_iota(jnp.int32, (hd, LANES), 1)
        for kv in range(2):
            shifted = pltpu.roll(win_ref[kv, g], win - n_tok, axis=1)
            last = shifted[:, win - LANES:]
            for t in range(n_tok):
                last = jnp.where(lane == LANES - n_tok + t, newwt_ref[kv, g, :, t:t + 1], last)
            if win > LANES:
                owin_ref[kv, g, :, 0:win - LANES] = shifted[:, 0:win - LANES]
            owin_ref[kv, g, :, win - LANES:] = last


def _nsa_sample(page_table, qc, qr, gates, kc, vc, pool_s, news, win_t, neww, neww_t, mcs, amat, *, n_tok):
    nb, n_pages = page_table.shape
    m_rows = qc.shape[2]
    kern = functools.partial(_nsa_sample_kernel, n_pages=n_pages, n_tok=n_tok)
    per_b = lambda a: pl.BlockSpec((None,) + a.shape[1:], lambda b, pt: (b,) + (0,) * (a.ndim - 1))
    n_cmp_pad = kc.shape[0] // nb
    cmp_spec = pl.BlockSpec((n_cmp_pad, LANES), lambda b, pt: (b, 0))
    page_specs = [pl.BlockSpec((None,) + pool_s.shape[1:], functools.partial(lambda pg, b, pt: (pt[b, pg], 0, 0, 0, 0), pg))
                  for pg in range(n_pages)]
    gs = pltpu.PrefetchScalarGridSpec(
        num_scalar_prefetch=1, grid=(nb,),
        in_specs=[per_b(qc), per_b(qr), per_b(gates), cmp_spec, cmp_spec] + page_specs +
                 [per_b(news), per_b(win_t), per_b(neww), per_b(neww_t),
                  pl.BlockSpec(mcs.shape, lambda b, pt: (0, 0)), pl.BlockSpec(amat.shape, lambda b, pt: (0, 0))],
        out_specs=[per_b(qc), per_b(win_t)])
    return pl.pallas_call(
        kern, grid_spec=gs,
        out_shape=[jax.ShapeDtypeStruct((nb, NSA_KV_HEADS, m_rows, HEAD_DIM), F32),
                   jax.ShapeDtypeStruct(win_t.shape, F32)],
        compiler_params=_cparams("parallel"),
        name="nsa_sample",
    )(page_table, qc, qr, gates, kc, vc, *([pool_s] * n_pages), news, win_t, neww, neww_t, mcs, amat)


C_GROUP_COLS = 3 * DIL_HEADS * HEAD_DIM


def _proj_c_kernel(x_ref, gmix_ref, w_ref, gq_ref, gk_ref, cs_ref, sn_ref, *out_refs, tails):
    q_refs, r_refs, t_refs = out_refs[:N_DIL], out_refs[N_DIL:2 * N_DIL], out_refs[2 * N_DIL:]
    hn = _rms_rows(x_ref[...], gmix_ref[...]).astype(BF16)
    cs, sn = cs_ref[...], sn_ref[...]
    wd = DIL_HEADS * HEAD_DIM
    for g in range(N_DIL):
        z = _dot(hn, w_ref[:, g * C_GROUP_COLS:(g + 1) * C_GROUP_COLS])
        q = _rope(_rms_heads(z[:, 0:wd], gq_ref[g:g + 1, :]), cs, sn)
        q_refs[g][...] = (q * QSCALE).astype(BF16)
        r_refs[g][:, 0:wd] = _rope(_rms_heads(z[:, wd:2 * wd], gk_ref[g:g + 1, :]), cs, sn)
        r_refs[g][:, wd:] = z[:, 2 * wd:]
        if tails is not None:
            first, cols = tails[g]
            tm = x_ref.shape[0]

            i_seq = pl.program_id(0) % tails[N_DIL]

            @pl.when(i_seq >= first)
            def _():
                t_refs[g][...] = r_refs[g][tm - cols:tm, :].T

            if first > 0:
                @pl.when(i_seq == 0)
                def _():
                    t_refs[g][...] = jnp.zeros(t_refs[g].shape, F32)


def _proj_c(x, gmix, w, gq, gk, cs, sn, tm, with_tails):
    n = x.shape[0]
    seq = cs.shape[0]
    tiles_per_seq = seq // tm
    row = lambda width: pl.BlockSpec((tm, width), lambda i: (i, 0))
    tab = pl.BlockSpec((tm, LANES), lambda i: (i % tiles_per_seq, 0))
    wd = DIL_HEADS * HEAD_DIM
    out_specs = [row(wd)] * N_DIL + [row(2 * wd)] * N_DIL
    out_shape = [jax.ShapeDtypeStruct((n, wd), BF16)] * N_DIL + [jax.ShapeDtypeStruct((n, 2 * wd), F32)] * N_DIL
    tails = None
    if with_tails:
        tails = []
        for win, _ in DIL_PAIRS:
            rows = min(win, seq)
            cols = min(rows, tm)
            assert rows % cols == 0 and seq % tm == 0
            first = tiles_per_seq - max(rows // tm, 1)
            tails.append((first, cols))
            out_specs.append(pl.BlockSpec(
                (None, 2 * wd, cols),
                functools.partial(lambda first, i: (i // tiles_per_seq, 0, jnp.maximum(i % tiles_per_seq - first, 0)), first)))
            out_shape.append(jax.ShapeDtypeStruct((n // seq, 2 * wd, rows), F32))
        tails = tuple(tails) + (tiles_per_seq,)
    return pl.pallas_call(
        functools.partial(_proj_c_kernel, tails=tails),
        grid=(n // tm,),
        in_specs=[row(D_MODEL), _const_spec(gmix.shape), _const_spec(w.shape), _const_spec(gq.shape),
                  _const_spec(gk.shape), tab, tab],
        out_specs=out_specs,
        out_shape=out_shape,
        compiler_params=_cparams("arbitrary" if with_tails else "parallel"),
        name="proj_c",
    )(x, gmix, w, gq, gk, cs, sn)


def _dil_prompt_kernel(q_ref, kp_ref, vp_ref, kc_ref, vc_ref, o_ref, ml_ref, *, tq):
    i = pl.program_id(2)
    row = lax.broadcasted_iota(jnp.int32, (tq, 2 * tq), 0)
    col = lax.broadcasted_iota(jnp.int32, (tq, 2 * tq), 1)
    delta = tq + row - col
    mask = (delta >= 0) & (delta < DIL_KEYS) & ((col >= tq) | (i > 0))
    k = jnp.concatenate([kp_ref[...], kc_ref[...]], axis=0).astype(BF16)
    v = jnp.concatenate([vp_ref[...], vc_ref[...]], axis=0).astype(BF16)
    lane = lax.broadcasted_iota(jnp.int32, (tq, LANES), 1)
    ml = jnp.zeros((tq, LANES), F32)
    hd = HEAD_DIM
    low = lane < hd
    for pair in range(DIL_HEADS // 2):
        ps = slice(pair * LANES, (pair + 1) * LANES)
        q_pair = q_ref[:, ps]
        outs = []
        for e in range(2):
            h = 2 * pair + e
            q_own = jnp.where(low if e == 0 else jnp.logical_not(low), q_pair, jnp.zeros_like(q_pair))
            s = jnp.where(mask, _dot_nt(q_own, k[:, ps]), NEG)
            m = jnp.max(s, axis=-1, keepdims=True)
            p = jnp.where(mask, jnp.exp(s - m), 0.0)
            l = jnp.sum(p, axis=-1, keepdims=True)
            outs.append(_dot(p.astype(BF16), v[:, ps]))
            ml = jnp.where(lane == h, m, jnp.where(lane == DIL_HEADS + h, l, ml))
        o_ref[:, ps] = jnp.where(low, outs[0], outs[1])
    ml_ref[...] = ml


def _dil_prompt(q, rows, *, nb, seq, dil, tq):
    wd = DIL_HEADS * HEAD_DIM
    assert tq >= DIL_KEYS - 1 and seq % (dil * tq) == 0
    nu = seq // dil // tq
    qv = q.reshape(nb * seq // dil, dil * wd)
    rv = rows.reshape(nb * seq // dil, dil * 2 * wd)
    cur = lambda off: (lambda b, r, i: (b * nu + i, 2 * r + off))
    prev = lambda off: (lambda b, r, i: (b * nu + jnp.maximum(i - 1, 0), 2 * r + off))
    blk = lambda imap: pl.BlockSpec((tq, wd), imap)
    o, ml = pl.pallas_call(
        functools.partial(_dil_prompt_kernel, tq=tq),
        grid=(nb, dil, nu),
        in_specs=[blk(lambda b, r, i: (b * nu + i, r)), blk(prev(0)), blk(prev(1)), blk(cur(0)), blk(cur(1))],
        out_specs=[blk(lambda b, r, i: (b * nu + i, r)), pl.BlockSpec((tq, LANES), lambda b, r, i: (b * nu + i, r))],
        out_shape=[jax.ShapeDtypeStruct(qv.shape, F32), jax.ShapeDtypeStruct((qv.shape[0], dil * LANES), F32)],
        compiler_params=_cparams("parallel", "parallel", "arbitrary"),
        name="dil_prompt",
    )(qv, rv, rv, rv, rv)
    return o.reshape(nb * seq, wd), ml.reshape(nb * seq, LANES)


DIL_SAMPLE_HEADS = 4


def _dil_sample_kernel(q_ref, n0_ref, n1_ref, n2_ref, c0_ref, c1_ref, c2_ref, o_ref, oc0_ref, oc1_ref, oc2_ref,
                       *, n_tok):
    n_refs, c_refs, oc_refs = (n0_ref, n1_ref, n2_ref), (c0_ref, c1_ref, c2_ref), (oc0_ref, oc1_ref, oc2_ref)
    tp = lax.broadcasted_iota(jnp.int32, (1, TOK_PAD), 1)
    lane_t = lax.broadcasted_iota(jnp.int32, (HEAD_DIM, TOK_PAD), 1)
    for hl in range(q_ref.shape[1]):
        stats = [[None] * N_DIL for _ in range(n_tok)]
        for g, (w, dil) in enumerate(DIL_PAIRS):
            nk = n_refs[g][0, hl]
            nv = n_refs[g][1, hl]
            row = lax.broadcasted_iota(jnp.int32, (1, w), 1)
            n_ch = w // LANES
            q_cols = [q_ref[g, hl, :, t:t + 1] for t in range(n_tok)]
            q_wide = [jnp.broadcast_to(qc, (HEAD_DIM, LANES)) for qc in q_cols]
            shared = dil >= n_tok
            groups = [list(range(n_tok))] if shared else [[t] for t in range(n_tok)]
            lane_tok = lax.broadcasted_iota(jnp.int32, (HEAD_DIM, LANES), 1) & (dil - 1)
            row_tok = row & (dil - 1)
            for toks in groups:
                if shared:
                    q_sel = jnp.zeros((HEAD_DIM, LANES), F32)
                    for t in toks:
                        q_sel = jnp.where(lane_tok == t, q_wide[t], q_sel)
                else:
                    q_sel = q_wide[toks[0]]
                s_parts = [jnp.sum(c_refs[g][0, hl, :, c * LANES:(c + 1) * LANES] * q_sel, axis=0, keepdims=True)
                           for c in range(n_ch)]
                s_all = jnp.concatenate(s_parts, axis=1) if n_ch > 1 else s_parts[0]
                m_lanes = jnp.zeros((1, w), F32)
                any_valid = jnp.zeros((1, w), jnp.bool_)
                valids, pnews = [], []
                for t in toks:
                    valid = (row_tok == (t & (dil - 1))) & (w + t - row <= (DIL_KEYS - 1) * dil)
                    dn = t - tp
                    valid_n = (dn >= 0) & ((dn & (dil - 1)) == 0)
                    sn = jnp.where(valid_n, jnp.sum(nk * q_cols[t], axis=0, keepdims=True), NEG)
                    m = jnp.maximum(jnp.max(jnp.where(valid, s_all, NEG), axis=-1, keepdims=True),
                                    jnp.max(sn, axis=-1, keepdims=True))
                    m_lanes = jnp.where(valid, m, m_lanes)
                    any_valid = any_valid | valid
                    valids.append(valid)
                    pnews.append(jnp.where(valid_n, jnp.exp(sn - m), 0.0))
                    stats[t][g] = (None, m, None)
                p_all = jnp.where(any_valid, jnp.exp(s_all - m_lanes), 0.0)
                acc = jnp.zeros((HEAD_DIM, LANES), F32)
                for c in range(n_ch):
                    acc = acc + c_refs[g][1, hl, :, c * LANES:(c + 1) * LANES] * p_all[:, c * LANES:(c + 1) * LANES]
                for t, valid, pn in zip(toks, valids, pnews):
                    l = jnp.sum(jnp.where(valid, p_all, 0.0), axis=-1, keepdims=True) + jnp.sum(pn, axis=-1, keepdims=True)
                    own = acc if not shared else jnp.where(lane_tok == t, acc, 0.0)
                    o = jnp.sum(own, axis=-1, keepdims=True) + jnp.sum(nv * pn, axis=-1, keepdims=True)
                    stats[t][g] = (o, stats[t][g][1], l)
            lane = lax.broadcasted_iota(jnp.int32, (HEAD_DIM, LANES), 1)
            for kv, new in ((0, nk), (1, nv)):
                shifted = pltpu.roll(c_refs[g][kv, hl], w - n_tok, axis=1)
                last = shifted[:, w - LANES:]
                for t in range(n_tok):
                    last = jnp.where(lane == LANES - n_tok + t, new[:, t:t + 1], last)
                if w > LANES:
                    oc_refs[g][kv, hl, :, 0:w - LANES] = shifted[:, 0:w - LANES]
                oc_refs[g][kv, hl, :, w - LANES:] = last
        out = jnp.zeros((HEAD_DIM, TOK_PAD), F32)
        for t in range(n_tok):
            parts = stats[t]
            mx = functools.reduce(jnp.maximum, [m for _, m, _ in parts])
            num = sum(jnp.exp(m - mx) * o for o, m, _ in parts)
            den = sum(jnp.exp(m - mx) * l for _, m, l in parts)
            out = jnp.where(lane_t == t, num / den, out)
        o_ref[hl] = out


def _dil_sample(q, news, caches, *, n_tok):
    nb = q.shape[0]
    hs = DIL_SAMPLE_HEADS
    for g, (w, d) in enumerate(DIL_PAIRS):
        assert caches[g].shape[-1] == w and w // d == DIL_KEYS - 1
        assert d & (d - 1) == 0
    blk = lambda a: pl.BlockSpec((None, a.shape[1], hs) + a.shape[3:], lambda b, hh: (b, 0, hh, 0, 0))
    o_spec = pl.BlockSpec((None, hs, HEAD_DIM, TOK_PAD), lambda b, hh: (b, hh, 0, 0))
    return pl.pallas_call(
        functools.partial(_dil_sample_kernel, n_tok=n_tok),
        grid=(nb, DIL_HEADS // hs),
        in_specs=[blk(q)] + [blk(a) for a in news] + [blk(a) for a in caches],
        out_specs=[o_spec] + [blk(a) for a in caches],
        out_shape=[jax.ShapeDtypeStruct((nb, DIL_HEADS, HEAD_DIM, TOK_PAD), F32)]
                  + [jax.ShapeDtypeStruct(a.shape, F32) for a in caches],
        compiler_params=_cparams("parallel", "parallel"),
        name="dil_sample",
    )(q, *news, *caches)


FF_CHUNK = 1024


def _post_kernel(*refs, n_cat, combine):
    h_ref = refs[0]
    mix_refs = refs[1:1 + (2 * N_DIL if combine else n_cat)]
    (wo_ref, gmlp_ref, w1_ref, w2_ref, gple_ref, wg_ref, p_ref, wp_ref, out_ref) = refs[1 + len(mix_refs):1 + len(mix_refs) + 9]
    hd = HEAD_DIM
    if combine:
        comb_scr = refs[-1]
        o_refs, ml_refs = mix_refs[:N_DIL], mix_refs[N_DIL:]
        for h in range(DIL_HEADS):
            ms = [r[:, h:h + 1] for r in ml_refs]
            ls = [r[:, DIL_HEADS + h:DIL_HEADS + h + 1] for r in ml_refs]
            mx = functools.reduce(jnp.maximum, ms)
            ws = [jnp.exp(m - mx) for m in ms]
            num = sum(w * r[:, h * hd:(h + 1) * hd] for w, r in zip(ws, o_refs))
            den = sum(w * l for w, l in zip(ws, ls))
            comb_scr[:, h * hd:(h + 1) * hd] = (num / den).astype(BF16)
        y = _dot(comb_scr[...], wo_ref[...])
    else:
        y = None
        for k, r in enumerate(mix_refs):
            part = _dot(r[...].astype(BF16), wo_ref[k * r.shape[1]:(k + 1) * r.shape[1], :])
            y = part if y is None else y + part
    h1 = h_ref[...] + y
    hn = _rms_rows(h1, gmlp_ref[...]).astype(BF16)
    acc = jnp.zeros(h1.shape, F32)
    for c in range(D_FF // FF_CHUNK):
        u = _dot(hn, w1_ref[:, c * FF_CHUNK:(c + 1) * FF_CHUNK])
        acc = acc + _dot(jnp.square(jnp.maximum(u, 0.0)).astype(BF16), w2_ref[c * FF_CHUNK:(c + 1) * FF_CHUNK, :])
    h2 = h1 + acc
    gate = jax.nn.sigmoid(_dot(_rms_rows(h2, gple_ref[...]).astype(BF16), wg_ref[...]))
    out_ref[...] = h2 + gate * _dot(p_ref[...].astype(BF16), wp_ref[...])


def _post(h, mix, wo, gmlp, w1, w2, gple, wg, p, wp, *, combine, tm):
    n = h.shape[0]
    row = lambda a: pl.BlockSpec((tm, a.shape[1]), lambda i: (i, 0))
    weights = (wo, gmlp, w1, w2, gple, wg)
    wspec = lambda a: pl.BlockSpec(a.shape, lambda i: (0, 0), pipeline_mode=pl.Buffered(1))
    kern = functools.partial(_post_kernel, n_cat=len(mix), combine=combine)
    return pl.pallas_call(
        kern,
        grid=(n // tm,),
        in_specs=[row(h)] + [row(a) for a in mix] + [wspec(a) for a in weights] + [row(p), wspec(wp)],
        out_specs=row(h),
        out_shape=jax.ShapeDtypeStruct(h.shape, F32),
        scratch_shapes=[pltpu.VMEM((tm, DIL_HEADS * HEAD_DIM), BF16)] if combine else [],
        compiler_params=_cparams("parallel"),
        name="post",
    )(h, *mix, *weights, p, wp)


def _rope_tables(pos):
    half = HEAD_DIM // 2
    freq = ROPE_THETA ** (-jnp.arange(half, dtype=F32) / half)
    ang = pos.astype(F32)[:, None] * freq[None, :]
    cos, sin = jnp.cos(ang), jnp.sin(ang)
    return jnp.tile(jnp.concatenate([cos, cos], axis=1), (1, 2)), jnp.tile(jnp.concatenate([-sin, sin], axis=1), (1, 2))


def _cmp_to_slc(n_cmp, n_slc, rows_pad, cols_pad):
    cs = jnp.arange(n_cmp)[:, None] * CMP_STRIDE
    ss = jnp.arange(n_slc)[None, :] * SLC_BLOCK
    ov = jnp.maximum(jnp.minimum(cs + CMP_LEN, ss + SLC_BLOCK) - jnp.maximum(cs, ss), 0)
    return jnp.pad(ov.astype(F32) / CMP_LEN, ((0, rows_pad - n_cmp), (0, cols_pad - n_slc)))


def _round_up(x, m):
    return (x + m - 1) // m * m


def _block_diag2(w):
    z = jnp.zeros_like(w)
    return jnp.concatenate([jnp.concatenate([w, z], axis=2), jnp.concatenate([z, w], axis=2)], axis=1)


def kernel(x_prompt, x_sample, state_gla, cache_nsa_cmp, cache_nsa_slc, cache_nsa_win, cache_dil_0, cache_dil_1,
           cache_dil_2, page_table, p_prompt, p_sample, norm_mix, norm_mlp, norm_ple, a_w_in, a_w_out, gla_w_a2,
           gla_b_a, gla_g_norm, nsa_g_qk, nsa_w_phi, nsa_pe, c_w_in, c_g_qk, c_w_out, mlp_w1, mlp_w2, ple_w_proj,
           ple_w_gate):
    nb_p, seq, _ = x_prompt.shape
    nb_s, n_tok, _ = x_sample.shape
    depth = norm_mix.shape[0]
    past = page_table.shape[1] * PAGE_SIZE
    assert n_tok < CMP_STRIDE and n_tok <= 8 and seq % 512 == 0
    tm_p = 256
    tm_s = min(256, nb_s * n_tok)
    cs_p, sn_p = _rope_tables(jnp.arange(seq))
    cs_s, sn_s = _rope_tables(past + jnp.arange(nb_s * n_tok) % n_tok)
    hp = x_prompt.reshape(nb_p * seq, D_MODEL)
    hs = x_sample.reshape(nb_s * n_tok, D_MODEL)
    dil_caches = (cache_dil_0, cache_dil_1, cache_dil_2)
    outs = {k: [] for k in ("gla_p", "gla_s", "cmp_p", "cmp_s", "slc_p", "slc_s", "win_p", "win_s")}
    dil_p = [[] for _ in DIL_PAIRS]
    dil_s = [[] for _ in DIL_PAIRS]
    kvrow = (2, NSA_KV_HEADS, HEAD_DIM)
    wd = DIL_HEADS * HEAD_DIM

    for i in range(depth):
        j = i // 2
        gmix = norm_mix[i][None, :]
        if i % 2 == 0:
            w_in = a_w_in[j]
            pad_cols = lambda a, width: jnp.pad(a, ((0, 0), (0, width - a.shape[1])))
            w_pad = jnp.concatenate([w_in[:, :1536], pad_cols(w_in[:, 1536:1552], LANES), w_in[:, 1552:2832],
                                     pad_cols(w_in[:, 2832:], LANES)], axis=1).astype(BF16)
            wa2 = jnp.pad(gla_w_a2[j], ((0, LANES - GLA_LOWRANK), (0, 0))).astype(BF16)
            ba = gla_b_a[j][None, :]
            gqk = jnp.tile(nsa_g_qk[j], (1, NSA_HEADS))
            gnorm = gla_g_norm[j][None, :]
            wk, wv = (_block_diag2(nsa_w_phi[j, c]).astype(BF16) for c in range(2))
            pek, pev = (jnp.tile(nsa_pe[j, c], (1, 2)) for c in range(2))
            wo = a_w_out[j].astype(BF16)

            gla, qc, qr, gates, rc, _, _, sbf, wbf, rc_t, rs_t, rw_t = _proj_a(hp, gmix, w_pad, wa2, ba, gqk, cs_p, sn_p,
                                                                               tm_p)
            s0 = jnp.zeros((nb_p, GLA_HEADS, GLA_DK, GLA_DV), F32)
            chunk = min(GLA_CHUNK, seq)
            o_gla, s_p = _gla(gla, s0, gnorm, nb=nb_p, rows_per_seq=seq, chunk=chunk, n_valid=chunk, out_dtype=BF16)
            kc, vc = _compress_prompt(rc, nb_p, seq, wk, wv, pek, pev)
            n_cmp_pad = seq // CMP_STRIDE
            mcs = _cmp_to_slc(n_cmp_pad - 1, seq // SLC_BLOCK, n_cmp_pad, _round_up(seq // SLC_BLOCK, LANES))
            o_nsa = _nsa_prompt(qc, qr, gates, kc, vc, sbf, wbf, mcs, nb=nb_p, seq=seq, qb=NSA_Q_BLOCK)
            mix_p = (o_gla, o_nsa)
            outs["gla_p"].append(s_p)
            rows_out = lambda a: a.reshape((nb_p,) + kvrow + (a.shape[-1],)).transpose(0, 4, 1, 2, 3)
            outs["cmp_p"].append(rows_out(rc_t))
            outs["slc_p"].append(rows_out(rs_t))
            outs["win_p"].append(rows_out(rw_t[:, :, seq - min(NSA_WINDOW, seq):]))

            gla, qc, qr, gates, rc, rs, rw = _proj_a(hs, gmix, w_pad, wa2, ba, gqk, cs_s, sn_s, tm_s)[:7]
            gla8 = jnp.pad(gla.reshape(nb_s, n_tok, G_END), ((0, 0), (0, 8 - n_tok), (0, 0))).reshape(nb_s * 8, G_END)
            o_gla8, s_s = _gla(gla8, state_gla[j], gnorm, nb=nb_s, rows_per_seq=8, chunk=8, n_valid=n_tok, out_dtype=F32)
            o_gla = o_gla8.reshape(nb_s, 8, -1)[:, :n_tok].reshape(nb_s * n_tok, -1)
            kc, vc = _compress_pages(cache_nsa_cmp[j].transpose(0, 2, 3, 4, 1), page_table, wk, wv, pek, pev)
            n_cmp_pad = past // CMP_STRIDE
            n_slc = (past + n_tok + SLC_BLOCK - 1) // SLC_BLOCK
            mcs = _cmp_to_slc(n_cmp_pad - 1, n_slc, n_cmp_pad, LANES)
            m_rows = NSA_GROUP * n_tok
            regroup = lambda a: a.reshape(nb_s, n_tok, NSA_KV_HEADS, NSA_GROUP, -1).transpose(0, 2, 3, 1, 4).reshape(
                nb_s, NSA_KV_HEADS, m_rows, -1)
            g16 = jnp.pad(regroup(gates[:, :3 * NSA_HEADS]), ((0, 0), (0, 0), (0, 0), (0, LANES - 3)))
            tok_id = jnp.arange(m_rows) % n_tok
            amat = (tok_id[:, None] == tok_id[None, :]).astype(F32)
            pad_rows = lambda a: jnp.pad(a.reshape(nb_s, n_tok, 256), ((0, 0), (0, TOK_PAD - n_tok), (0, 0)))
            keyt = lambda a: a.transpose(0, 2, 3, 4, 1)
            neww_t = jnp.pad(keyt(rw.reshape((nb_s, n_tok) + kvrow)), [(0, 0)] * 4 + [(0, TOK_PAD - n_tok)])
            o16, win_new = _nsa_sample(page_table, regroup(qc), regroup(qr), g16, kc, vc, keyt(cache_nsa_slc[j]),
                                       pad_rows(rs), keyt(cache_nsa_win[j]), pad_rows(rw), neww_t, mcs, amat,
                                       n_tok=n_tok)
            o_nsa = o16.reshape(nb_s, NSA_KV_HEADS, NSA_GROUP, n_tok, HEAD_DIM).transpose(0, 3, 1, 2, 4).reshape(
                nb_s * n_tok, NSA_HEADS * HEAD_DIM)
            mix_s = (o_gla, o_nsa)
            outs["gla_s"].append(s_s)
            outs["cmp_s"].append(rc.reshape((nb_s, n_tok) + kvrow))
            outs["slc_s"].append(rs.reshape((nb_s, n_tok) + kvrow))
            outs["win_s"].append(win_new.transpose(0, 4, 1, 2, 3))
            combine = False
        else:
            w_c = c_w_in[j].astype(BF16)
            gq = jnp.tile(c_g_qk[j][:, 0], (1, DIL_HEADS))
            gk = jnp.tile(c_g_qk[j][:, 1], (1, DIL_HEADS))
            wo = c_w_out[j].astype(BF16)
            dilrow = (2, DIL_HEADS, HEAD_DIM)

            res = _proj_c(hp, gmix, w_c, gq, gk, cs_p, sn_p, tm_p, True)
            qs, rows, tails = res[:N_DIL], res[N_DIL:2 * N_DIL], res[2 * N_DIL:]
            o_parts, ml_parts = [], []
            for g, (w, d) in enumerate(DIL_PAIRS):
                o_g, ml_g = _dil_prompt(qs[g], rows[g], nb=nb_p, seq=seq, dil=d, tq=128)
                o_parts.append(o_g)
                ml_parts.append(ml_g)
                dil_p[g].append(tails[g].reshape((nb_p,) + dilrow + (tails[g].shape[-1],)).transpose(0, 4, 1, 2, 3))
            mix_p = tuple(o_parts) + tuple(ml_parts)

            res = _proj_c(hs, gmix, w_c, gq, gk, cs_s, sn_s, tm_s, False)
            qs, rows = res[:N_DIL], res[N_DIL:]
            pad_tok = lambda a: jnp.pad(a, [(0, 0)] * (a.ndim - 1) + [(0, TOK_PAD - n_tok)])
            q_t = pad_tok(jnp.stack([q.astype(F32).reshape(nb_s, n_tok, DIL_HEADS, HEAD_DIM).transpose(0, 2, 3, 1)
                                     for q in qs], axis=1))
            news_t = [pad_tok(r.reshape((nb_s, n_tok) + dilrow).transpose(0, 2, 3, 4, 1)) for r in rows]
            caches_t = [c[j].transpose(0, 2, 3, 4, 1) for c in dil_caches]
            o_t, *new_caches = _dil_sample(q_t, news_t, caches_t, n_tok=n_tok)
            mix_s = (o_t[..., :n_tok].transpose(0, 3, 1, 2).reshape(nb_s * n_tok, wd),)
            for g in range(N_DIL):
                dil_s[g].append(new_caches[g].transpose(0, 4, 1, 2, 3))
            combine = True

        lw = (norm_mlp[i][None, :], mlp_w1[i].astype(BF16), mlp_w2[i].astype(BF16), norm_ple[i][None, :],
              ple_w_gate[i].astype(BF16))
        wp = ple_w_proj[i].astype(BF16)
        hp = _post(hp, mix_p, wo, *lw, p_prompt[i].reshape(nb_p * seq, PLE_DIM), wp, combine=combine, tm=tm_p)
        hs = _post(hs, mix_s, wo, *lw, p_sample[i].reshape(nb_s * n_tok, PLE_DIM), wp, combine=False, tm=tm_s)

    st = jnp.stack
    return (hp.reshape(x_prompt.shape), hs.reshape(x_sample.shape),
            st(outs["gla_p"]), st(outs["gla_s"]), st(outs["cmp_p"]), st(outs["cmp_s"]),
            st(outs["slc_p"]), st(outs["slc_s"]), st(outs["win_p"]), st(outs["win_s"]),
            st(dil_p[0]), st(dil_s[0]), st(dil_p[1]), st(dil_s[1]), st(dil_p[2]), st(dil_s[2]))
```
